```python
import math
import jax, jax.numpy as jnp
from jax import lax
import numpy as np

D_MODEL = 1024
BATCH = 16
SEQ = 2048
DEPTH = 1
DEC_BATCH = 16
DEC_SEQ = 16
PAST_LEN = 1024

CHUNK = 64
HEAD_DIM = 64
N_HEADS_A = 8
BAND_PAST_CHUNKS = 8
BAND_PAST = BAND_PAST_CHUNKS * CHUNK
BAND_LEN = (BAND_PAST_CHUNKS + 1) * CHUNK
REL_FUT = CHUNK - 1
REL_PAST = 256
N_REL = REL_FUT + REL_PAST + 1
N_HEADS_B = 4
WIDTH_A = N_HEADS_A * HEAD_DIM
WIDTH_B_QK = N_HEADS_B * 2 * HEAD_DIM
WIDTH_B_V = N_HEADS_B * 2 * HEAD_DIM
IN_WIDTH = 3 * WIDTH_A + 2 * WIDTH_B_QK + WIDTH_B_V
SPLITS = (WIDTH_A, 2 * WIDTH_A, 3 * WIDTH_A, 3 * WIDTH_A + WIDTH_B_QK, 3 * WIDTH_A + 2 * WIDTH_B_QK)
D_FF = 4 * D_MODEL
ROPE_THETA = 10000.0
Q_BLOCK = 128
EPS = 1e-6
NEG_INF = -1e30
ATTN_SCALE = HEAD_DIM ** -0.5

kernel_name = "hybrid_chunkband_diffattn_stream_step"


def rms_norm(x, g):
    xf = x.astype(jnp.float32)
    y = xf * lax.rsqrt(jnp.mean(xf * xf, axis=-1, keepdims=True) + EPS)
    return (y * g.astype(jnp.float32)).astype(x.dtype)


def rope(x, pos):
    half = HEAD_DIM // 2
    inv_freq = ROPE_THETA ** (-jnp.arange(half, dtype=jnp.float32) / half)
    ang = pos.astype(jnp.float32)[:, None] * inv_freq[None, :]
    cos = jnp.cos(ang)[:, None, None, :]
    sin = jnp.sin(ang)[:, None, None, :]
    xf = x.astype(jnp.float32)
    x1, x2 = xf[..., :half], xf[..., half:]
    return jnp.concatenate([x1 * cos - x2 * sin, x2 * cos + x1 * sin], axis=-1).astype(x.dtype)


def project_qkv(h, pos, w_in, qn_a, kn_a, qn_b, kn_b):
    b, s, _ = h.shape
    z = h @ w_in
    qa, ka, va, qb, kb, vb = jnp.split(z, SPLITS, axis=-1)
    qa = rms_norm(qa.reshape(b, s, N_HEADS_A, HEAD_DIM), qn_a)
    ka = rms_norm(ka.reshape(b, s, N_HEADS_A, HEAD_DIM), kn_a)
    va = va.reshape(b, s, N_HEADS_A, HEAD_DIM)
    qb = rope(rms_norm(qb.reshape(b, s, N_HEADS_B, 2, HEAD_DIM), qn_b), pos)
    kb = rope(rms_norm(kb.reshape(b, s, N_HEADS_B, 2, HEAD_DIM), kn_b), pos)
    vb = vb.reshape(b, s, N_HEADS_B, 2 * HEAD_DIM)
    return qa, ka, va, qb, kb, vb


def rel_bias_lookup(table, dist):
    idx = jnp.clip(dist, -REL_FUT, REL_PAST) + REL_FUT
    return table[:, idx].astype(jnp.float32)


def chunk_band_attn_prompt(q, k, v, rel_table):
    b, s, h, d = q.shape
    nc = s // CHUNK
    qc = q.reshape(b, nc, CHUNK, h, d)
    pad = ((0, 0), (BAND_PAST_CHUNKS, 0), (0, 0), (0, 0), (0, 0))
    kp = jnp.pad(k.reshape(b, nc, CHUNK, h, d), pad)
    vp = jnp.pad(v.reshape(b, nc, CHUNK, h, d), pad)
    band_idx = jnp.arange(nc)[:, None] + jnp.arange(BAND_PAST_CHUNKS + 1)[None, :]
    kband = kp[:, band_idx].reshape(b, nc, BAND_LEN, h, d)
    vband = vp[:, band_idx].reshape(b, nc, BAND_LEN, h, d)
    sc = jnp.einsum('bcqhd,bckhd->bhcqk', qc, kband).astype(jnp.float32) * ATTN_SCALE
    qi = jnp.arange(CHUNK)
    kj = jnp.arange(BAND_LEN)
    bias = rel_bias_lookup(rel_table, qi[:, None] + BAND_PAST - kj[None, :])
    k_valid = (jnp.arange(nc)[:, None] * CHUNK - BAND_PAST + kj[None, :]) >= 0
    sc = jnp.where(k_valid[None, None, :, None, :], sc + bias[None, :, None], NEG_INF)
    p = jax.nn.softmax(sc, axis=-1).astype(v.dtype)
    o = jnp.einsum('bhcqk,bckhd->bcqhd', p, vband)
    return o.reshape(b, s, h * d)


def chunk_band_attn_sample(q, k_new, v_new, k_cache, v_cache, rel_table):
    b, t, h, d = q.shape
    lc = k_cache.shape[1]
    keys = jnp.concatenate([k_cache, k_new], axis=1)
    vals = jnp.concatenate([v_cache, v_new], axis=1)
    q_pos = PAST_LEN + jnp.arange(t)
    k_pos = jnp.concatenate([PAST_LEN - lc + jnp.arange(lc), PAST_LEN + jnp.arange(t)])
    bias = rel_bias_lookup(rel_table, q_pos[:, None] - k_pos[None, :])
    sc = jnp.einsum('bqhd,bkhd->bhqk', q, keys).astype(jnp.float32) * ATTN_SCALE + bias[None]
    p = jax.nn.softmax(sc, axis=-1).astype(v_new.dtype)
    o = jnp.einsum('bhqk,bkhd->bqhd', p, vals)
    return o.reshape(b, t, h * d)


def diff_lambda(lq1, lk1, lq2, lk2, layer_idx):
    lam_init = 0.8 - 0.6 * math.exp(-0.3 * layer_idx)
    f = jnp.float32
    lam = (jnp.exp(jnp.sum(lq1.astype(f) * lk1.astype(f)))
           - jnp.exp(jnp.sum(lq2.astype(f) * lk2.astype(f))) + lam_init)
    return lam, lam_init


def diff_combine(sc, v, lam):
    p = jax.nn.softmax(sc, axis=-1)
    a = (p[:, :, 0] - lam * p[:, :, 1]).astype(v.dtype)
    return jnp.einsum('bhqk,bkhe->bqhe', a, v)


def diff_attn_prompt(q, k, v, lam):
    b, s, h, _, d = q.shape
    nb = s // Q_BLOCK
    q_blocks = jnp.moveaxis(q.reshape(b, nb, Q_BLOCK, h, 2, d), 1, 0)
    k_chunk = jnp.arange(s) // CHUNK

    def one_block(args):
        q_blk, start = args
        sc = jnp.einsum('bqhrd,bkhrd->bhrqk', q_blk, k).astype(jnp.float32) * ATTN_SCALE
        q_chunk = (start + jnp.arange(Q_BLOCK)) // CHUNK
        visible = k_chunk[None, :] <= q_chunk[:, None]
        sc = jnp.where(visible, sc, NEG_INF)
        return diff_combine(sc, v, lam)

    o = lax.map(one_block, (q_blocks, jnp.arange(nb) * Q_BLOCK))
    return jnp.moveaxis(o, 0, 1).reshape(b, s, h, 2 * d)


def diff_attn_sample(q, k_new, v_new, k_cache, v_cache, lam):
    keys = jnp.concatenate([k_cache, k_new], axis=1)
    vals = jnp.concatenate([v_cache, v_new], axis=1)
    sc = jnp.einsum('bqhrd,bkhrd->bhrqk', q, keys).astype(jnp.float32) * ATTN_SCALE
    return diff_combine(sc, vals, lam)


def diff_finish(o, subln_g, lam_init):
    b, s = o.shape[:2]
    return (rms_norm(o, subln_g) * (1.0 - lam_init)).reshape(b, s, WIDTH_B_V)


def gated_merge(h, ya, yb, w_gate, b_gate, w_proj_a, w_proj_b, w_out):
    g = jax.nn.sigmoid(h @ w_gate + b_gate)
    g_a, g_b = g[..., :D_MODEL], g[..., D_MODEL:]
    return (g_a * (ya @ w_proj_a) + g_b * (yb @ w_proj_b)) @ w_out


def sq_relu_mlp(x, g, w1, w2):
    u = jax.nn.relu(rms_norm(x, g) @ w1)
    return (u * u) @ w2


def setup_inputs(seed: int = 0) -> dict:
    key = jax.random.key(seed)
    ks = jax.random.split(key, 32)
    f32 = jnp.float32

    def nrm(k, shape, scale=1.0):
        return jax.random.normal(k, shape, f32) * scale

    def gain(k, shape):
        return 1.0 + 0.01 * jax.random.normal(k, shape, f32)

    la = min(BAND_PAST, PAST_LEN)
    return {
        "x_prompt": nrm(ks[0], (BATCH, SEQ, D_MODEL)),
        "x_sample": nrm(ks[1], (DEC_BATCH, DEC_SEQ, D_MODEL)),
        "cache_a_k": nrm(ks[2], (DEPTH, DEC_BATCH, la, N_HEADS_A, HEAD_DIM)),
        "cache_a_v": nrm(ks[3], (DEPTH, DEC_BATCH, la, N_HEADS_A, HEAD_DIM)),
        "cache_b_k": nrm(ks[4], (DEPTH, DEC_BATCH, PAST_LEN, N_HEADS_B, 2, HEAD_DIM)),
        "cache_b_v": nrm(ks[5], (DEPTH, DEC_BATCH, PAST_LEN, N_HEADS_B, 2 * HEAD_DIM)),
        "ln1_g": gain(ks[6], (DEPTH, D_MODEL)),
        "w_in": nrm(ks[7], (DEPTH, D_MODEL, IN_WIDTH), D_MODEL ** -0.5),
        "qn_a": gain(ks[8], (DEPTH, HEAD_DIM)),
        "kn_a": gain(ks[9], (DEPTH, HEAD_DIM)),
        "rel_bias": nrm(ks[10], (DEPTH, N_HEADS_A, N_REL), 0.1),
        "qn_b": gain(ks[11], (DEPTH, HEAD_DIM)),
        "kn_b": gain(ks[12], (DEPTH, HEAD_DIM)),
        "lam_q1": nrm(ks[13], (DEPTH, HEAD_DIM), 0.1),
        "lam_k1": nrm(ks[14], (DEPTH, HEAD_DIM), 0.1),
        "lam_q2": nrm(ks[15], (DEPTH, HEAD_DIM), 0.1),
        "lam_k2": nrm(ks[16], (DEPTH, HEAD_DIM), 0.1),
        "subln_g": gain(ks[17], (DEPTH, 2 * HEAD_DIM)),
        "w_gate": nrm(ks[18], (DEPTH, D_MODEL, 2 * D_MODEL), D_MODEL ** -0.5),
        "b_gate": nrm(ks[19], (DEPTH, 2 * D_MODEL), 0.01),
        "w_proj_a": nrm(ks[20], (DEPTH, WIDTH_A, D_MODEL), WIDTH_A ** -0.5),
        "w_proj_b": nrm(ks[21], (DEPTH, WIDTH_B_V, D_MODEL), WIDTH_B_V ** -0.5),
        "w_out": nrm(ks[22], (DEPTH, D_MODEL, D_MODEL), D_MODEL ** -0.5),
        "ln2_g": gain(ks[23], (DEPTH, D_MODEL)),
        "w_ff1": nrm(ks[24], (DEPTH, D_MODEL, D_FF), D_MODEL ** -0.5),
        "w_ff2": nrm(ks[25], (DEPTH, D_FF, D_MODEL), D_FF ** -0.5),
    }


def reference(x_prompt, x_sample, cache_a_k, cache_a_v, cache_b_k, cache_b_v,
              ln1_g, w_in, qn_a, kn_a, rel_bias, qn_b, kn_b,
              lam_q1, lam_k1, lam_q2, lam_k2, subln_g,
              w_gate, b_gate, w_proj_a, w_proj_b, w_out,
              ln2_g, w_ff1, w_ff2):
    s_p = x_prompt.shape[1]
    t_s = x_sample.shape[1]
    pos_p = jnp.arange(s_p)
    pos_s = PAST_LEN + jnp.arange(t_s)
    keep_p = min(BAND_PAST, s_p)
    xp, xs = x_prompt, x_sample
    ak_p, av_p, bk_p, bv_p = [], [], [], []
    ak_s, av_s, bk_s, bv_s = [], [], [], []
    for l in range(DEPTH):
        lam, lam_init = diff_lambda(lam_q1[l], lam_k1[l], lam_q2[l], lam_k2[l], l)

        hp = rms_norm(xp, ln1_g[l])
        qa, ka, va, qb, kb, vb = project_qkv(hp, pos_p, w_in[l], qn_a[l], kn_a[l], qn_b[l], kn_b[l])
        ya = chunk_band_attn_prompt(qa, ka, va, rel_bias[l])
        yb = diff_finish(diff_attn_prompt(qb, kb, vb, lam), subln_g[l], lam_init)
        xp = xp + gated_merge(hp, ya, yb, w_gate[l], b_gate[l], w_proj_a[l], w_proj_b[l], w_out[l])
        xp = xp + sq_relu_mlp(xp, ln2_g[l], w_ff1[l], w_ff2[l])
        ak_p.append(ka[:, s_p - keep_p:])
        av_p.append(va[:, s_p - keep_p:])
        bk_p.append(kb)
        bv_p.append(vb)

        hs = rms_norm(xs, ln1_g[l])
        qa, ka, va, qb, kb, vb = project_qkv(hs, pos_s, w_in[l], qn_a[l], kn_a[l], qn_b[l], kn_b[l])
        ya = chunk_band_attn_sample(qa, ka, va, cache_a_k[l], cache_a_v[l], rel_bias[l])
        yb = diff_finish(diff_attn_sample(qb, kb, vb, cache_b_k[l], cache_b_v[l], lam), subln_g[l], lam_init)
        xs = xs + gated_merge(hs, ya, yb, w_gate[l], b_gate[l], w_proj_a[l], w_proj_b[l], w_out[l])
        xs = xs + sq_relu_mlp(xs, ln2_g[l], w_ff1[l], w_ff2[l])
        ak_s.append(ka)
        av_s.append(va)
        bk_s.append(kb)
        bv_s.append(vb)

    return (xp, xs,
            jnp.stack(ak_p), jnp.stack(av_p), jnp.stack(bk_p), jnp.stack(bv_p),
            jnp.stack(ak_s), jnp.stack(av_s), jnp.stack(bk_s), jnp.stack(bv_s))
```

```python
import functools
import math

import jax
import jax.numpy as jnp
import numpy as np
from jax import lax
from jax.experimental import pallas as pl
from jax.experimental.pallas import tpu as pltpu

D_MODEL = 1024
CHUNK = 64
HEAD_DIM = 64
N_HEADS_A = 8
N_HEADS_B = 4
BAND_PAST_CHUNKS = 8
BAND_PAST = BAND_PAST_CHUNKS * CHUNK
REL_FUT = CHUNK - 1
REL_PAST = 256
WIDTH = 512
N_GROUPS = 6
D_FF = 4 * D_MODEL
ROPE_THETA = 10000.0
EPS = 1e-6
NEG_INF = -1e30
ATTN_SCALE = HEAD_DIM ** -0.5

LANES = 128
MXU_DIM = 256
GROUP_CHUNKS = 2
GROUP_Q = GROUP_CHUNKS * CHUNK
GROUP_KEYS = BAND_PAST + GROUP_Q
Q_BLOCK_B = 256
VMEM_LIMIT = 56 * 1024 * 1024

BF16 = jnp.bfloat16
F32 = jnp.float32


def _dot(a, b):
    return jnp.dot(a, b, preferred_element_type=F32)


def _dot_nt(a, b):
    return lax.dot_general(a, b, (((1,), (1,)), ((), ())), preferred_element_type=F32)


def _rms(x, g):
    return x * lax.rsqrt(jnp.mean(x * x, axis=-1, keepdims=True) + EPS) * g


def _const_spec(shape):
    nd = len(shape)
    return pl.BlockSpec(shape, lambda *_: (0,) * nd, pipeline_mode=pl.Buffered(1))


def _lane_halves(q):
    lane = lax.broadcasted_iota(jnp.int32, q.shape, 1)
    zero = jnp.zeros_like(q)
    return jnp.concatenate([jnp.where(lane < HEAD_DIM, q, zero), jnp.where(lane >= HEAD_DIM, q, zero)], axis=0)


def _softmax_parts(parts):
    m = functools.reduce(jnp.maximum, [jnp.max(s, axis=-1, keepdims=True) for s in parts])
    ps = [jnp.exp(s - m) for s in parts]
    l = functools.reduce(lambda a, b: a + b, [jnp.sum(p, axis=-1, keepdims=True) for p in ps])
    return ps, 1.0 / l


def _proj_kernel(x_ref, g1_ref, w_ref, gains_ref, cos_ref, sin_ref, pool_ref,
                 qa_ref, ka_ref, va_ref, qb_ref, kb_ref, vb_ref):
    h = _rms(x_ref[...], g1_ref[...]).astype(BF16)
    cos = jnp.concatenate([cos_ref[...]] * (WIDTH // LANES), axis=1)
    sin = jnp.concatenate([sin_ref[...]] * (WIDTH // LANES), axis=1)
    pool = pool_ref[...]

    def group(i):
        return _dot(h, w_ref[:, i * WIDTH:(i + 1) * WIDTH])

    def head_norm(z, i):
        zz = (z * z).astype(BF16)
        ms = jnp.concatenate([_dot(zz[:, k * MXU_DIM:(k + 1) * MXU_DIM], pool) for k in range(WIDTH // MXU_DIM)],
                             axis=1)
        return z * lax.rsqrt(ms + EPS) * gains_ref[i:i + 1, :]

    def rope(z):
        lane = lax.broadcasted_iota(jnp.int32, z.shape, 1)
        half = HEAD_DIM // 2
        partner = jnp.where(lane % HEAD_DIM < half, pltpu.roll(z, WIDTH - half, 1), pltpu.roll(z, half, 1))
        return z * cos + partner * sin

    qa_ref[...] = (head_norm(group(0), 0) * ATTN_SCALE).astype(BF16)
    ka_ref[...] = head_norm(group(1), 1)
    va_ref[...] = group(2)
    qb_ref[...] = (rope(head_norm(group(3), 2)) * ATTN_SCALE).astype(BF16)
    kb_ref[...] = rope(head_norm(group(4), 3))
    vb_ref[...] = group(5)


def _proj(x2d, tm, n_pos_blocks, g1, w_in, gains, cos, sin, pool):
    n = x2d.shape[0]
    row = lambda w: pl.BlockSpec((tm, w), lambda i: (i, 0))
    pos = pl.BlockSpec((tm, LANES), lambda i: (i % n_pos_blocks, 0))
    out_dtypes = (BF16, F32, F32, BF16, F32, F32)
    return pl.pallas_call(
        _proj_kernel,
        grid=(n // tm,),
        in_specs=[row(D_MODEL), _const_spec(g1.shape), _const_spec(w_in.shape), _const_spec(gains.shape),
                  pos, pos, _const_spec(pool.shape)],
        out_specs=[row(WIDTH)] * N_GROUPS,
        out_shape=[jax.ShapeDtypeStruct((n, WIDTH), dt) for dt in out_dtypes],
        compiler_params=pltpu.CompilerParams(dimension_semantics=("arbitrary",), vmem_limit_bytes=VMEM_LIMIT),
        name="proj",
    )(x2d, g1, w_in, gains, cos, sin, pool)


def _band_prompt_kernel(q_ref, k_ref, v_ref, bias_ref, o_ref, kpad_ref, vpad_ref):
    s_len = q_ref.shape[1]
    kpad_ref[0:BAND_PAST, :] = jnp.zeros((BAND_PAST, LANES), BF16)
    vpad_ref[0:BAND_PAST, :] = jnp.zeros((BAND_PAST, LANES), BF16)
    kpad_ref[BAND_PAST:, :] = k_ref[0].astype(BF16)
    vpad_ref[BAND_PAST:, :] = v_ref[0].astype(BF16)
    bias = bias_ref[0]

    def group(g, carry):
        r0 = pl.multiple_of(g * GROUP_Q, GROUP_Q)
        qm = _lane_halves(q_ref[0, pl.ds(r0, GROUP_Q), :])
        s = _dot_nt(qm, kpad_ref[pl.ds(r0, GROUP_KEYS), :]) + bias
        col = lax.broadcasted_iota(jnp.int32, s.shape, 1)
        s = jnp.where(col + r0 >= BAND_PAST, s, NEG_INF)
        (p,), inv = _softmax_parts([s])
        o = _dot(p.astype(BF16), vpad_ref[pl.ds(r0, GROUP_KEYS), :]) * inv
        lane = lax.broadcasted_iota(jnp.int32, (GROUP_Q, LANES), 1)
        o_ref[0, pl.ds(r0, GROUP_Q), :] = jnp.where(lane < HEAD_DIM, o[:GROUP_Q], o[GROUP_Q:]).astype(o_ref.dtype)
        return carry

    lax.fori_loop(0, s_len // GROUP_Q, group, 0)


def _band_prompt(qa, ka, va, bias):
    b, s, _ = qa.shape
    pairs = WIDTH // LANES
    blk = pl.BlockSpec((1, s, LANES), lambda i, j: (i, 0, j))
    return pl.pallas_call(
        _band_prompt_kernel,
        grid=(b, pairs),
        in_specs=[blk, blk, blk, pl.BlockSpec((1,) + bias.shape[1:], lambda i, j: (j, 0, 0))],
        out_specs=blk,
        out_shape=jax.ShapeDtypeStruct((b, s, WIDTH), BF16),
        scratch_shapes=[pltpu.VMEM((BAND_PAST + s, LANES), BF16)] * 2,
        compiler_params=pltpu.CompilerParams(dimension_semantics=("arbitrary", "arbitrary"),
                                             vmem_limit_bytes=VMEM_LIMIT),
        name="band_prompt",
    )(qa, ka, va, bias)


def _lambda(lam_ref, lam_init):
    e1 = jnp.exp(jnp.sum(lam_ref[0:1, :] * lam_ref[1:2, :], axis=-1, keepdims=True))
    e2 = jnp.exp(jnp.sum(lam_ref[2:3, :] * lam_ref[3:4, :], axis=-1, keepdims=True))
    return e1 - e2 + lam_init


def _diff_out(o, subln, lam_init):
    return _rms(o, subln) * (1.0 - lam_init)


def _diff_prompt_kernel(q_ref, k_ref, v_ref, lam_ref, subln_ref, o_ref, kbf_ref, vbf_ref, *, lam_init):
    s_len = q_ref.shape[1]
    tq = Q_BLOCK_B
    kbf_ref[...] = k_ref[0].astype(BF16)
    vbf_ref[...] = v_ref[0].astype(BF16)
    lam = _lambda(lam_ref, lam_init)
    row = lax.broadcasted_iota(jnp.int32, (2 * tq, tq), 0) % tq
    col = lax.broadcasted_iota(jnp.int32, (2 * tq, tq), 1)
    diag_mask = jnp.where(col // CHUNK <= row // CHUNK, 0.0, NEG_INF).astype(F32)

    for i in range(s_len // tq):
        r0 = i * tq
        qs = _lane_halves(q_ref[0, r0:r0 + tq, :])
        parts = [_dot_nt(qs, kbf_ref[r0:r0 + tq, :]) + diag_mask]
        if i > 0:
            parts.append(_dot_nt(qs, kbf_ref[0:r0, :]))
        ps, inv = _softmax_parts(parts)
        w1 = inv[:tq]
        w2 = inv[tq:] * lam
        a = [(p[:tq] * w1 - p[tq:] * w2).astype(BF16) for p in ps]
        o = _dot(a[0], vbf_ref[r0:r0 + tq, :])
        if i > 0:
            o = o + _dot(a[1], vbf_ref[0:r0, :])
        o_ref[0, r0:r0 + tq, :] = _diff_out(o, subln_ref[...], lam_init).astype(o_ref.dtype)


def _diff_prompt(qb, kb, vb, lam_vecs, subln, lam_init):
    b, s, _ = qb.shape
    heads = WIDTH // LANES
    blk = pl.BlockSpec((1, s, LANES), lambda i, j: (i, 0, j))
    return pl.pallas_call(
        functools.partial(_diff_prompt_kernel, lam_init=lam_init),
        grid=(b, heads),
        in_specs=[blk, blk, blk, _const_spec(lam_vecs.shape), _const_spec(subln.shape)],
        out_specs=blk,
        out_shape=jax.ShapeDtypeStruct((b, s, WIDTH), BF16),
        scratch_shapes=[pltpu.VMEM((s, LANES), BF16)] * 2,
        compiler_params=pltpu.CompilerParams(dimension_semantics=("arbitrary", "arbitrary"),
                                             vmem_limit_bytes=VMEM_LIMIT),
        name="diff_prompt",
    )(qb, kb, vb, lam_vecs, subln)


def _sample_attn_kernel(qa_ref, ka_ref, va_ref, qb_ref, kb_ref, vb_ref, cak_ref, cav_ref, cbk_ref, cbv_ref,
                        bias_c_ref, bias_n_ref, lam_ref, subln_ref, ya_ref, yb_ref, *, lam_init):
    t = qa_ref.shape[0]
    lam = _lambda(lam_ref, lam_init)
    lane = lax.broadcasted_iota(jnp.int32, (t, LANES), 1)
    for j in range(WIDTH // LANES):
        cols = slice(j * LANES, (j + 1) * LANES)
        qm = _lane_halves(qa_ref[:, cols])
        kn = ka_ref[:, cols].astype(BF16)
        vn = va_ref[:, cols].astype(BF16)
        kc = cak_ref[0, :, cols].astype(BF16)
        vc = cav_ref[0, :, cols].astype(BF16)
        (pc, pn), inv = _softmax_parts([_dot_nt(qm, kc) + bias_c_ref[j], _dot_nt(qm, kn) + bias_n_ref[j]])
        o = (_dot(pc.astype(BF16), vc) + _dot(pn.astype(BF16), vn)) * inv
        ya_ref[:, cols] = jnp.where(lane < HEAD_DIM, o[:t], o[t:]).astype(ya_ref.dtype)
        qs = _lane_halves(qb_ref[:, cols])
        kn = kb_ref[:, cols].astype(BF16)
        vn = vb_ref[:, cols].astype(BF16)
        kc = cbk_ref[0, :, cols].astype(BF16)
        vc = cbv_ref[0, :, cols].astype(BF16)
        (pc, pn), inv = _softmax_parts([_dot_nt(qs, kc), _dot_nt(qs, kn)])
        w1 = inv[:t]
        w2 = inv[t:] * lam
        ac = (pc[:t] * w1 - pc[t:] * w2).astype(BF16)
        an = (pn[:t] * w1 - pn[t:] * w2).astype(BF16)
        o = _dot(ac, vc) + _dot(an, vn)
        yb_ref[:, cols] = _diff_out(o, subln_ref[...], lam_init).astype(yb_ref.dtype)


def _sample_attn(new, caches, bias_c, bias_n, lam_vecs, subln, lam_init, t):
    n = new[0].shape[0]
    b = n // t
    row = pl.BlockSpec((t, WIDTH), lambda i: (i, 0))
    cache = lambda c: pl.BlockSpec((1,) + c.shape[1:], lambda i: (i, 0, 0))
    return pl.pallas_call(
        functools.partial(_sample_attn_kernel, lam_init=lam_init),
        grid=(b,),
        in_specs=[row] * 6 + [cache(c) for c in caches]
        + [_const_spec(bias_c.shape), _const_spec(bias_n.shape), _const_spec(lam_vecs.shape), _const_spec(subln.shape)],
        out_specs=[row, row],
        out_shape=[jax.ShapeDtypeStruct((n, WIDTH), BF16)] * 2,
        compiler_params=pltpu.CompilerParams(dimension_semantics=("arbitrary",), vmem_limit_bytes=VMEM_LIMIT),
        name="sample_attn",
    )(*new, *caches, bias_c, bias_n, lam_vecs, subln)


def _merge_mlp_kernel(x_ref, ya_ref, yb_ref, g1_ref, wg_ref, bg_ref, wpa_ref, wpb_ref, wo_ref,
                      g2_ref, w1_ref, w2_ref, o_ref):
    x = x_ref[...]
    h = _rms(x, g1_ref[...]).astype(BF16)
    m = jax.nn.sigmoid(_dot(h, wg_ref[:, :D_MODEL]) + bg_ref[:, :D_MODEL]) * _dot(ya_ref[...], wpa_ref[...])
    m = m + jax.nn.sigmoid(_dot(h, wg_ref[:, D_MODEL:]) + bg_ref[:, D_MODEL:]) * _dot(yb_ref[...], wpb_ref[...])
    x1 = x + _dot(m.astype(BF16), wo_ref[...])
    hn = _rms(x1, g2_ref[...]).astype(BF16)
    acc = x1
    for c in range(D_FF // D_MODEL):
        cols = slice(c * D_MODEL, (c + 1) * D_MODEL)
        u = jnp.maximum(_dot(hn, w1_ref[:, cols]), 0.0)
        acc = acc + _dot((u * u).astype(BF16), w2_ref[cols, :])
    o_ref[...] = acc


def _merge_mlp(x2d, ya, yb, tm, g1, wg, bg, wpa, wpb, wo, g2, w1, w2):
    n = x2d.shape[0]
    row = lambda w: pl.BlockSpec((tm, w), lambda i: (i, 0))
    consts = (g1, wg, bg, wpa, wpb, wo, g2, w1, w2)
    return pl.pallas_call(
        _merge_mlp_kernel,
        grid=(n // tm,),
        in_specs=[row(D_MODEL), row(WIDTH), row(WIDTH)] + [_const_spec(c.shape) for c in consts],
        out_specs=row(D_MODEL),
        out_shape=jax.ShapeDtypeStruct((n, D_MODEL), F32),
        compiler_params=pltpu.CompilerParams(dimension_semantics=("arbitrary",), vmem_limit_bytes=VMEM_LIMIT),
        name="merge_mlp",
    )(x2d, ya, yb, *consts)


def _rope_tables(pos):
    half = HEAD_DIM // 2
    inv_freq = ROPE_THETA ** (-jnp.arange(half, dtype=F32) / half)
    ang = pos.astype(F32)[:, None] * inv_freq[None, :]
    cos = jnp.cos(ang)
    sin = jnp.sin(ang)
    reps = LANES // HEAD_DIM
    return (jnp.tile(jnp.concatenate([cos, cos], axis=1), (1, reps)),
            jnp.tile(jnp.concatenate([-sin, sin], axis=1), (1, reps)))


def _pair_rows(bias):
    h, q, k = bias.shape
    return bias.reshape(h // 2, 2 * q, k)


def _band_bias_prompt(table):
    q = np.arange(GROUP_Q)[:, None]
    k = np.arange(GROUP_KEYS)[None, :]
    idx = np.clip(q + BAND_PAST - k, -REL_FUT, REL_PAST) + REL_FUT
    qc, kc = q // CHUNK, k // CHUNK
    visible = (kc >= qc) & (kc <= qc + BAND_PAST_CHUNKS)
    bias = jnp.where(jnp.asarray(visible)[None], table[:, idx].astype(F32), NEG_INF)
    return _pair_rows(bias)


def _band_bias_sample(table, past_len, lc, t):
    q_pos = past_len + np.arange(t)
    k_pos = np.concatenate([past_len - lc + np.arange(lc), past_len + np.arange(t)])
    idx = np.clip(q_pos[:, None] - k_pos[None, :], -REL_FUT, REL_PAST) + REL_FUT
    bias = _pair_rows(table[:, idx].astype(F32))
    return bias[:, :, :lc], bias[:, :, lc:]


def _pool_matrix():
    i = np.arange(MXU_DIM)
    return jnp.asarray((i[:, None] // HEAD_DIM == i[None, :] // HEAD_DIM) / HEAD_DIM, dtype=BF16)


def _row_tile(n, target):
    tm = min(n, target)
    assert n % tm == 0
    return tm


def kernel(x_prompt, x_sample, cache_a_k, cache_a_v, cache_b_k, cache_b_v, ln1_g, w_in, qn_a, kn_a, rel_bias, qn_b, kn_b, lam_q1, lam_k1, lam_q2, lam_k2, subln_g, w_gate, b_gate, w_proj_a, w_proj_b, w_out, ln2_g, w_ff1, w_ff2):
    depth = w_in.shape[0]
    b, s, d = x_prompt.shape
    db, t, _ = x_sample.shape
    past_len = cache_b_k.shape[2]
    lc = cache_a_k.shape[2]
    keep = min(BAND_PAST, s)
    assert d == D_MODEL and s % Q_BLOCK_B == 0 and s % GROUP_Q == 0

    tm_p = _row_tile(s, 512)
    cos_p, sin_p = _rope_tables(jnp.arange(s))
    cos_s, sin_s = (jnp.tile(a, (db, 1)) for a in _rope_tables(past_len + jnp.arange(t)))
    pool = _pool_matrix()
    tile_heads = lambda g: jnp.tile(g, WIDTH // HEAD_DIM)

    xp = x_prompt.reshape(b * s, d)
    xs = x_sample.reshape(db * t, d)
    outs = [[] for _ in range(8)]
    for l in range(depth):
        lam_init = 0.8 - 0.6 * math.exp(-0.3 * l)
        g1 = ln1_g[l][None]
        g2 = ln2_g[l][None]
        gains = jnp.stack([tile_heads(qn_a[l]), tile_heads(kn_a[l]), tile_heads(qn_b[l]), tile_heads(kn_b[l])])
        lam_vecs = jnp.stack([lam_q1[l], lam_k1[l], lam_q2[l], lam_k2[l]])
        subln = subln_g[l][None]
        bias_p = _band_bias_prompt(rel_bias[l])
        bias_c, bias_n = _band_bias_sample(rel_bias[l], past_len, lc, t)
        w_in_l = w_in[l].astype(BF16)
        merge_w = (g1, w_gate[l].astype(BF16), b_gate[l][None], w_proj_a[l].astype(BF16), w_proj_b[l].astype(BF16),
                   w_out[l].astype(BF16), g2, w_ff1[l].astype(BF16), w_ff2[l].astype(BF16))

        qa, ka, va, qb, kb, vb = _proj(xp, tm_p, s // tm_p, g1, w_in_l, gains, cos_p, sin_p, pool)
        as3 = lambda a: a.reshape(b, s, WIDTH)
        ya = _band_prompt(as3(qa), as3(ka), as3(va), bias_p)
        yb = _diff_prompt(as3(qb), as3(kb), as3(vb), lam_vecs, subln, lam_init)
        xp = _merge_mlp(xp, ya.reshape(b * s, WIDTH), yb.reshape(b * s, WIDTH), tm_p, *merge_w)
        outs[0].append(as3(ka)[:, s - keep:].reshape(b, keep, N_HEADS_A, HEAD_DIM))
        outs[1].append(as3(va)[:, s - keep:].reshape(b, keep, N_HEADS_A, HEAD_DIM))
        outs[2].append(kb.reshape(b, s, N_HEADS_B, 2, HEAD_DIM))
        outs[3].append(vb.reshape(b, s, N_HEADS_B, 2 * HEAD_DIM))

        new = _proj(xs, db * t, 1, g1, w_in_l, gains, cos_s, sin_s, pool)
        caches = (cache_a_k[l].reshape(db, lc, WIDTH), cache_a_v[l].reshape(db, lc, WIDTH),
                  cache_b_k[l].reshape(db, past_len, WIDTH), cache_b_v[l].reshape(db, past_len, WIDTH))
        ya_s, yb_s = _sample_attn(new, caches, bias_c, bias_n, lam_vecs, subln, lam_init, t)
        xs = _merge_mlp(xs, ya_s, yb_s, db * t, *merge_w)
        outs[4].append(new[1].reshape(db, t, N_HEADS_A, HEAD_DIM))
        outs[5].append(new[2].reshape(db, t, N_HEADS_A, HEAD_DIM))
        outs[6].append(new[4].reshape(db, t, N_HEADS_B, 2, HEAD_DIM))
        outs[7].append(new[5].reshape(db, t, N_HEADS_B, 2 * HEAD_DIM))

    return (xp.reshape(b, s, d), xs.reshape(db, t, d)) + tuple(jnp.stack(o) for o in outs)
```

```python
import functools
import math

import jax
import jax.numpy as jnp
import numpy as np
from jax import lax
from jax.experimental import pallas as pl
from jax.experimental.pallas import tpu as pltpu

D_MODEL = 1024
CHUNK = 64
HEAD_DIM = 64
N_HEADS_A = 8
N_HEADS_B = 4
BAND_PAST_CHUNKS = 8
BAND_PAST = BAND_PAST_CHUNKS * CHUNK
REL_FUT = CHUNK - 1
REL_PAST = 256
N_REL = REL_FUT + REL_PAST + 1
WIDTH = 512
N_GROUPS = 6
D_FF = 4 * D_MODEL
ROPE_THETA = 10000.0
EPS = 1e-6
NEG_INF = -1e30
ATTN_SCALE = HEAD_DIM ** -0.5

LANES = 128
N_BLOCKS = WIDTH // LANES
MXU_DIM = 256
GROUP_CHUNKS = 2
GROUP_Q = GROUP_CHUNKS * CHUNK
GROUP_KEYS = BAND_PAST + GROUP_Q
BIAS_SPAN = GROUP_KEYS + GROUP_Q
Q_BLOCK_B = 256
ROW_TILE = 512
VMEM_LIMIT = 56 * 1024 * 1024

BF16 = jnp.bfloat16
F32 = jnp.float32


def _dot(a, b):
    return jnp.dot(a, b, preferred_element_type=F32)


def _dot_nt(a, b):
    return lax.dot_general(a, b, (((1,), (1,)), ((), ())), preferred_element_type=F32)


def _rms(x, g):
    return x * lax.rsqrt(jnp.mean(x * x, axis=-1, keepdims=True) + EPS) * g


def _const_spec(shape):
    nd = len(shape)
    return pl.BlockSpec(shape, lambda *_: (0,) * nd, pipeline_mode=pl.Buffered(1))


def _params(n_grid_axes):
    return pltpu.CompilerParams(dimension_semantics=("arbitrary",) * n_grid_axes, vmem_limit_bytes=VMEM_LIMIT)


def _lane_halves(q):
    lane = lax.broadcasted_iota(jnp.int32, q.shape, 1)
    zero = jnp.zeros_like(q)
    return jnp.concatenate([jnp.where(lane < HEAD_DIM, q, zero), jnp.where(lane >= HEAD_DIM, q, zero)], axis=0)


def _pick_halves(o):
    m = o.shape[0] // 2
    lane = lax.broadcasted_iota(jnp.int32, (m, LANES), 1)
    return jnp.where(lane < HEAD_DIM, o[:m], o[m:])


def _softmax_parts(parts):
    m = functools.reduce(jnp.maximum, [jnp.max(s, axis=-1, keepdims=True) for s in parts])
    ps = [jnp.exp(s - m) for s in parts]
    l = functools.reduce(lambda a, b: a + b, [jnp.sum(p, axis=-1, keepdims=True) for p in ps])
    return ps, 1.0 / l


def _proj_kernel(x_ref, g1_ref, w_ref, gains_ref, cos_ref, sin_ref, pool_ref, *out_refs, tiles_per_seq):
    h = _rms(x_ref[...], g1_ref[...]).astype(BF16)
    cos = jnp.concatenate([cos_ref[...]] * N_BLOCKS, axis=1)
    sin = jnp.concatenate([sin_ref[...]] * N_BLOCKS, axis=1)
    pool = pool_ref[...]

    def group(i):
        return _dot(h, w_ref[:, i * WIDTH:(i + 1) * WIDTH])

    def head_norm(z, i):
        zz = (z * z).astype(BF16)
        ms = jnp.concatenate([_dot(zz[:, k * MXU_DIM:(k + 1) * MXU_DIM], pool) for k in range(WIDTH // MXU_DIM)],
                             axis=1)
        return z * lax.rsqrt(ms + EPS) * gains_ref[i:i + 1, :]

    def rope(z):
        lane = lax.broadcasted_iota(jnp.int32, z.shape, 1)
        half = HEAD_DIM // 2
        partner = jnp.where(lane % HEAD_DIM < half, pltpu.roll(z, WIDTH - half, 1), pltpu.roll(z, half, 1))
        return z * cos + partner * sin

    qa = (head_norm(group(0), 0) * ATTN_SCALE).astype(BF16)
    ka = head_norm(group(1), 1)
    va = group(2)
    qb = (rope(head_norm(group(3), 2)) * ATTN_SCALE).astype(BF16)
    kb = rope(head_norm(group(4), 3))
    vb = group(5)

    if tiles_per_seq == 0:
        for ref, val in zip(out_refs, (qa, ka, va, qb, kb, vb)):
            ref[...] = val
        return

    qa_ref, kat_ref, va_ref, qb_ref, kbt_ref, vb4_ref, kat_tail_ref, vat_tail_ref = out_refs
    tm = x_ref.shape[0]
    qa_ref[...] = qa
    qb_ref[...] = qb
    va_ref[...] = va.astype(BF16)
    kat = ka.T
    kat_ref[0] = kat.astype(BF16)
    kbt_ref[0] = kb.T
    for hd in range(N_BLOCKS):
        vb4_ref[pl.ds(hd, tm, stride=N_BLOCKS), :] = vb[:, hd * LANES:(hd + 1) * LANES]

    @pl.when(pl.program_id(0) % tiles_per_seq == tiles_per_seq - 1)
    def _():
        kat_tail_ref[0] = kat
        vat_tail_ref[0] = va.T


def _proj_specs(tm, n_pos_blocks, consts):
    row = lambda w: pl.BlockSpec((tm, w), lambda i: (i, 0))
    pos = pl.BlockSpec((tm, LANES), lambda i: (i % n_pos_blocks, 0))
    g1, w_in, gains, pool = consts
    return row, [row(D_MODEL), _const_spec(g1.shape), _const_spec(w_in.shape), _const_spec(gains.shape),
                 pos, pos, _const_spec(pool.shape)]


def _proj_sample(x2d, g1, w_in, gains, cos, sin, pool):
    n = x2d.shape[0]
    row, in_specs = _proj_specs(n, 1, (g1, w_in, gains, pool))
    out_dtypes = (BF16, F32, F32, BF16, F32, F32)
    return pl.pallas_call(
        functools.partial(_proj_kernel, tiles_per_seq=0),
        grid=(1,),
        in_specs=in_specs,
        out_specs=[row(WIDTH)] * N_GROUPS,
        out_shape=[jax.ShapeDtypeStruct((n, WIDTH), dt) for dt in out_dtypes],
        compiler_params=_params(1),
        name="proj_sample",
    )(x2d, g1, w_in, gains, cos, sin, pool)


def _proj_prompt(x2d, b, s, g1, w_in, gains, cos, sin, pool):
    n = x2d.shape[0]
    tm = ROW_TILE
    tps = s // tm
    row, in_specs = _proj_specs(tm, tps, (g1, w_in, gains, pool))
    feat = pl.BlockSpec((1, WIDTH, tm), lambda i: (i // tps, 0, i % tps))
    tail = pl.BlockSpec((1, WIDTH, tm), lambda i: (i // tps, 0, 0))
    out_specs = [row(WIDTH), feat, row(WIDTH), row(WIDTH), feat,
                 pl.BlockSpec((tm * N_BLOCKS, LANES), lambda i: (i, 0)), tail, tail]
    out_shape = [jax.ShapeDtypeStruct((n, WIDTH), BF16),
                 jax.ShapeDtypeStruct((b, WIDTH, s), BF16),
                 jax.ShapeDtypeStruct((n, WIDTH), BF16),
                 jax.ShapeDtypeStruct((n, WIDTH), BF16),
                 jax.ShapeDtypeStruct((b, WIDTH, s), F32),
                 jax.ShapeDtypeStruct((n * N_BLOCKS, LANES), F32),
                 jax.ShapeDtypeStruct((b, WIDTH, tm), F32),
                 jax.ShapeDtypeStruct((b, WIDTH, tm), F32)]
    return pl.pallas_call(
        functools.partial(_proj_kernel, tiles_per_seq=tps),
        grid=(n // tm,),
        in_specs=in_specs,
        out_specs=out_specs,
        out_shape=out_shape,
        compiler_params=_params(1),
        name="proj_prompt",
    )(x2d, g1, w_in, gains, cos, sin, pool)


def _group_bias(f_ref):
    def one(hd):
        x = jnp.broadcast_to(f_ref[hd:hd + 1, :], (GROUP_Q, BIAS_SPAN))
        return pltpu.roll(x, BIAS_SPAN - GROUP_Q + 1, 1, stride=1, stride_axis=0)[:, :GROUP_KEYS]

    bias = jnp.concatenate([one(0), one(1)], axis=0)
    qc = (lax.broadcasted_iota(jnp.int32, bias.shape, 0) % GROUP_Q) // CHUNK
    kc = lax.broadcasted_iota(jnp.int32, bias.shape, 1) // CHUNK
    return jnp.where((kc >= qc) & (kc <= qc + BAND_PAST_CHUNKS), bias, NEG_INF)


def _band_prompt_kernel(q_ref, kt_ref, v_ref, f_ref, o_ref, kpad_ref, vpad_ref, bias_ref):
    s_len = q_ref.shape[1]
    kpad_ref[:, 0:BAND_PAST] = jnp.zeros((LANES, BAND_PAST), BF16)
    vpad_ref[0:BAND_PAST, :] = jnp.zeros((BAND_PAST, LANES), BF16)
    kpad_ref[:, BAND_PAST:] = kt_ref[0]
    vpad_ref[BAND_PAST:, :] = v_ref[0]
    bias_ref[...] = _group_bias(f_ref.at[0])

    def group(g, carry):
        r0 = pl.multiple_of(g * GROUP_Q, GROUP_Q)
        qm = _lane_halves(q_ref[0, pl.ds(r0, GROUP_Q), :])
        s = _dot(qm, kpad_ref[:, pl.ds(r0, GROUP_KEYS)]) + bias_ref[...]
        col = lax.broadcasted_iota(jnp.int32, s.shape, 1)
        s = jnp.where(col + r0 >= BAND_PAST, s, NEG_INF)
        (p,), inv = _softmax_parts([s])
        o = _dot(p.astype(BF16), vpad_ref[pl.ds(r0, GROUP_KEYS), :]) * inv
        o_ref[0, pl.ds(r0, GROUP_Q), :] = _pick_halves(o).astype(o_ref.dtype)
        return carry

    lax.fori_loop(0, s_len // GROUP_Q, group, 0, unroll=2)


def _band_prompt(qa, kat, va, f):
    b, s, _ = qa.shape
    blk = pl.BlockSpec((1, s, LANES), lambda i, j: (i, 0, j))
    blk_t = pl.BlockSpec((1, LANES, s), lambda i, j: (i, j, 0))
    return pl.pallas_call(
        _band_prompt_kernel,
        grid=(b, N_BLOCKS),
        in_specs=[blk, blk_t, blk, pl.BlockSpec((1, 2, BIAS_SPAN), lambda i, j: (j, 0, 0))],
        out_specs=blk,
        out_shape=jax.ShapeDtypeStruct((b, s, WIDTH), BF16),
        scratch_shapes=[pltpu.VMEM((LANES, BAND_PAST + s), BF16), pltpu.VMEM((BAND_PAST + s, LANES), BF16),
                        pltpu.VMEM((2 * GROUP_Q, GROUP_KEYS), F32)],
        compiler_params=_params(2),
        name="band_prompt",
    )(qa, kat, va, f)


def _lambda(lam_ref, lam_init):
    e1 = jnp.exp(jnp.sum(lam_ref[0:1, :] * lam_ref[1:2, :], axis=-1, keepdims=True))
    e2 = jnp.exp(jnp.sum(lam_ref[2:3, :] * lam_ref[3:4, :], axis=-1, keepdims=True))
    return e1 - e2 + lam_init


def _diff_out(o, subln, lam_init):
    return _rms(o, subln) * (1.0 - lam_init)


def _diff_prompt_kernel(q_ref, kt_ref, v4_ref, lam_ref, subln_ref, o_ref, kbf_ref, vbf_ref, *, lam_init):
    s_len = q_ref.shape[1]
    tq = Q_BLOCK_B
    kbf_ref[...] = kt_ref[0].astype(BF16)
    vbf_ref[...] = v4_ref[0, pl.ds(pl.program_id(1), s_len, stride=N_BLOCKS), :].astype(BF16)
    lam = _lambda(lam_ref, lam_init)
    row = lax.broadcasted_iota(jnp.int32, (2 * tq, tq), 0) % tq
    col = lax.broadcasted_iota(jnp.int32, (2 * tq, tq), 1)
    diag_mask = jnp.where(col // CHUNK <= row // CHUNK, 0.0, NEG_INF).astype(F32)

    for i in range(s_len // tq):
        r0 = i * tq
        qs = _lane_halves(q_ref[0, r0:r0 + tq, :])
        parts = [_dot(qs, kbf_ref[:, r0:r0 + tq]) + diag_mask]
        if i > 0:
            parts.append(_dot(qs, kbf_ref[:, 0:r0]))
        ps, inv = _softmax_parts(parts)
        w1 = inv[:tq]
        w2 = inv[tq:] * lam
        a = [(p[:tq] * w1 - p[tq:] * w2).astype(BF16) for p in ps]
        o = _dot(a[0], vbf_ref[r0:r0 + tq, :])
        if i > 0:
            o = o + _dot(a[1], vbf_ref[0:r0, :])
        o_ref[0, r0:r0 + tq, :] = _diff_out(o, subln_ref[...], lam_init).astype(o_ref.dtype)


def _diff_prompt(qb, kbt, vb4, lam_vecs, subln, lam_init):
    b, s, _ = qb.shape
    blk = pl.BlockSpec((1, s, LANES), lambda i, j: (i, 0, j))
    return pl.pallas_call(
        functools.partial(_diff_prompt_kernel, lam_init=lam_init),
        grid=(b, N_BLOCKS),
        in_specs=[blk, pl.BlockSpec((1, LANES, s), lambda i, j: (i, j, 0)),
                  pl.BlockSpec((1, s * N_BLOCKS, LANES), lambda i, j: (i, 0, 0)),
                  _const_spec(lam_vecs.shape), _const_spec(subln.shape)],
        out_specs=blk,
        out_shape=jax.ShapeDtypeStruct((b, s, WIDTH), BF16),
        scratch_shapes=[pltpu.VMEM((LANES, s), BF16), pltpu.VMEM((s, LANES), BF16)],
        compiler_params=_params(2),
        name="diff_prompt",
    )(qb, kbt, vb4, lam_vecs, subln)


def _sample_attn_kernel(qa_ref, ka_ref, va_ref, qb_ref, kb_ref, vb_ref, cakt_ref, cavt_ref, cbkt_ref, cbv4_ref,
                        bias_c_ref, bias_n_ref, lam_ref, subln_ref, ya_ref, yb_ref, *, lam_init):
    t = qa_ref.shape[0]
    past = cbkt_ref.shape[2]
    lam = _lambda(lam_ref, lam_init)
    for j in range(N_BLOCKS):
        cols = slice(j * LANES, (j + 1) * LANES)
        qm = _lane_halves(qa_ref[:, cols])
        kn = ka_ref[:, cols].astype(BF16)
        vn = va_ref[:, cols].astype(BF16)
        kct = cakt_ref[0, cols, :].astype(BF16)
        vct = cavt_ref[0, cols, :].astype(BF16)
        (pc, pn), inv = _softmax_parts([_dot(qm, kct) + bias_c_ref[j], _dot_nt(qm, kn) + bias_n_ref[j]])
        o = (_dot_nt(pc.astype(BF16), vct) + _dot(pn.astype(BF16), vn)) * inv
        ya_ref[:, cols] = _pick_halves(o).astype(ya_ref.dtype)
        qs = _lane_halves(qb_ref[:, cols])
        kn = kb_ref[:, cols].astype(BF16)
        vn = vb_ref[:, cols].astype(BF16)
        kct = cbkt_ref[0, cols, :].astype(BF16)
        vc = cbv4_ref[0, pl.ds(j, past, stride=N_BLOCKS), :].astype(BF16)
        (pc, pn), inv = _softmax_parts([_dot(qs, kct), _dot_nt(qs, kn)])
        w1 = inv[:t]
        w2 = inv[t:] * lam
        ac = (pc[:t] * w1 - pc[t:] * w2).astype(BF16)
        an = (pn[:t] * w1 - pn[t:] * w2).astype(BF16)
        o = _dot(ac, vc) + _dot(an, vn)
        yb_ref[:, cols] = _diff_out(o, subln_ref[...], lam_init).astype(yb_ref.dtype)


def _sample_attn(new, caches, bias_c, bias_n, lam_vecs, subln, lam_init, t):
    n = new[0].shape[0]
    b = n // t
    row = pl.BlockSpec((t, WIDTH), lambda i: (i, 0))
    cache = lambda c: pl.BlockSpec((1,) + c.shape[1:], lambda i: (i, 0, 0))
    return pl.pallas_call(
        functools.partial(_sample_attn_kernel, lam_init=lam_init),
        grid=(b,),
        in_specs=[row] * 6 + [cache(c) for c in caches]
        + [_const_spec(bias_c.shape), _const_spec(bias_n.shape), _const_spec(lam_vecs.shape), _const_spec(subln.shape)],
        out_specs=[row, row],
        out_shape=[jax.ShapeDtypeStruct((n, WIDTH), BF16)] * 2,
        compiler_params=_params(1),
        name="sample_attn",
    )(*new, *caches, bias_c, bias_n, lam_vecs, subln)


def _merge_mlp_kernel(x_ref, ya_ref, yb_ref, g1_ref, wg_ref, bg_ref, wpa_ref, wpb_ref, wo_ref,
                      g2_ref, w1_ref, w2_ref, o_ref):
    x = x_ref[...]
    h = _rms(x, g1_ref[...]).astype(BF16)
    m = jax.nn.sigmoid(_dot(h, wg_ref[:, :D_MODEL]) + bg_ref[:, :D_MODEL]) * _dot(ya_ref[...], wpa_ref[...])
    m = m + jax.nn.sigmoid(_dot(h, wg_ref[:, D_MODEL:]) + bg_ref[:, D_MODEL:]) * _dot(yb_ref[...], wpb_ref[...])
    x1 = x + _dot(m.astype(BF16), wo_ref[...])
    hn = _rms(x1, g2_ref[...]).astype(BF16)
    acc = x1
    for c in range(D_FF // D_MODEL):
        cols = slice(c * D_MODEL, (c + 1) * D_MODEL)
        u = jnp.maximum(_dot(hn, w1_ref[:, cols]), 0.0)
        acc = acc + _dot((u * u).astype(BF16), w2_ref[cols, :])
    o_ref[...] = acc


def _merge_mlp(x2d, ya, yb, tm, g1, wg, bg, wpa, wpb, wo, g2, w1, w2):
    n = x2d.shape[0]
    row = lambda w: pl.BlockSpec((tm, w), lambda i: (i, 0))
    consts = (g1, wg, bg, wpa, wpb, wo, g2, w1, w2)
    return pl.pallas_call(
        _merge_mlp_kernel,
        grid=(n // tm,),
        in_specs=[row(D_MODEL), row(WIDTH), row(WIDTH)] + [_const_spec(c.shape) for c in consts],
        out_specs=row(D_MODEL),
        out_shape=jax.ShapeDtypeStruct((n, D_MODEL), F32),
        compiler_params=_params(1),
        name="merge_mlp",
    )(x2d, ya, yb, *consts)


def _rope_tables(pos):
    half = HEAD_DIM // 2
    inv_freq = ROPE_THETA ** (-jnp.arange(half, dtype=F32) / half)
    ang = pos.astype(F32)[:, None] * inv_freq[None, :]
    cos = jnp.cos(ang)
    sin = jnp.sin(ang)
    reps = LANES // HEAD_DIM
    return (jnp.tile(jnp.concatenate([cos, cos], axis=1), (1, reps)),
            jnp.tile(jnp.concatenate([-sin, sin], axis=1), (1, reps)))


def _pair_rows(bias):
    h, q, k = bias.shape
    return bias.reshape(h // 2, 2 * q, k)


def _band_offset_table(table):
    n_far = BAND_PAST + GROUP_Q - 1 - REL_PAST + 1
    n_fut = BIAS_SPAN - n_far - (N_REL - 1)
    h = table.shape[0]
    f = jnp.concatenate([jnp.broadcast_to(table[:, N_REL - 1:], (h, n_far)), table[:, N_REL - 2::-1],
                         jnp.broadcast_to(table[:, :1], (h, n_fut))], axis=1).astype(F32)
    return f.reshape(h // 2, 2, BIAS_SPAN)


def _band_bias_sample(table, past_len, lc, t):
    q_pos = past_len + np.arange(t)
    k_pos = np.concatenate([past_len - lc + np.arange(lc), past_len + np.arange(t)])
    idx = np.clip(q_pos[:, None] - k_pos[None, :], -REL_FUT, REL_PAST) + REL_FUT
    bias = _pair_rows(table[:, idx].astype(F32))
    return bias[:, :, :lc], bias[:, :, lc:]


def _pool_matrix():
    i = np.arange(MXU_DIM)
    return jnp.asarray((i[:, None] // HEAD_DIM == i[None, :] // HEAD_DIM) / HEAD_DIM, dtype=BF16)


def kernel(x_prompt, x_sample, cache_a_k, cache_a_v, cache_b_k, cache_b_v, ln1_g, w_in, qn_a, kn_a, rel_bias, qn_b, kn_b, lam_q1, lam_k1, lam_q2, lam_k2, subln_g, w_gate, b_gate, w_proj_a, w_proj_b, w_out, ln2_g, w_ff1, w_ff2):
    depth = w_in.shape[0]
    b, s, d = x_prompt.shape
    db, t, _ = x_sample.shape
    past_len = cache_b_k.shape[2]
    lc = cache_a_k.shape[2]
    keep = min(BAND_PAST, s)
    assert d == D_MODEL and s % ROW_TILE == 0 and keep == ROW_TILE

    cos_p, sin_p = _rope_tables(jnp.arange(s))
    cos_s, sin_s = (jnp.tile(a, (db, 1)) for a in _rope_tables(past_len + jnp.arange(t)))
    pool = _pool_matrix()
    tile_heads = lambda g: jnp.tile(g, WIDTH // HEAD_DIM)

    xp = x_prompt.reshape(b * s, d)
    xs = x_sample.reshape(db * t, d)
    outs = [[] for _ in range(8)]
    for l in range(depth):
        lam_init = 0.8 - 0.6 * math.exp(-0.3 * l)
        g1 = ln1_g[l][None]
        g2 = ln2_g[l][None]
        gains = jnp.stack([tile_heads(qn_a[l]), tile_heads(kn_a[l]), tile_heads(qn_b[l]), tile_heads(kn_b[l])])
        lam_vecs = jnp.stack([lam_q1[l], lam_k1[l], lam_q2[l], lam_k2[l]])
        subln = subln_g[l][None]
        bias_f = _band_offset_table(rel_bias[l])
        bias_c, bias_n = _band_bias_sample(rel_bias[l], past_len, lc, t)
        w_in_l = w_in[l].astype(BF16)
        merge_w = (g1, w_gate[l].astype(BF16), b_gate[l][None], w_proj_a[l].astype(BF16), w_proj_b[l].astype(BF16),
                   w_out[l].astype(BF16), g2, w_ff1[l].astype(BF16), w_ff2[l].astype(BF16))

        qa, kat, va, qb, kbt, vb4, kat_tail, vat_tail = _proj_prompt(xp, b, s, g1, w_in_l, gains, cos_p, sin_p, pool)
        as3 = lambda a: a.reshape(b, s, WIDTH)
        ya = _band_prompt(as3(qa), kat, as3(va), bias_f)
        yb = _diff_prompt(as3(qb), kbt, vb4.reshape(b, s * N_BLOCKS, LANES), lam_vecs, subln, lam_init)
        xp = _merge_mlp(xp, ya.reshape(b * s, WIDTH), yb.reshape(b * s, WIDTH), ROW_TILE, *merge_w)
        token_major_a = lambda a: a.reshape(b, N_HEADS_A, HEAD_DIM, keep).transpose(0, 3, 1, 2)
        outs[0].append(token_major_a(kat_tail))
        outs[1].append(token_major_a(vat_tail))
        outs[2].append(kbt.reshape(b, N_HEADS_B, 2, HEAD_DIM, s).transpose(0, 4, 1, 2, 3))
        outs[3].append(vb4.reshape(b, s, N_HEADS_B, 2 * HEAD_DIM))

        new = _proj_sample(xs, g1, w_in_l, gains, cos_s, sin_s, pool)
        caches = (cache_a_k[l].transpose(0, 2, 3, 1).reshape(db, WIDTH, lc),
                  cache_a_v[l].transpose(0, 2, 3, 1).reshape(db, WIDTH, lc),
                  cache_b_k[l].transpose(0, 2, 3, 4, 1).reshape(db, WIDTH, past_len),
                  cache_b_v[l].reshape(db, past_len * N_BLOCKS, LANES))
        ya_s, yb_s = _sample_attn(new, caches, bias_c, bias_n, lam_vecs, subln, lam_init, t)
        xs = _merge_mlp(xs, ya_s, yb_s, db * t, *merge_w)
        outs[4].append(new[1].reshape(db, t, N_HEADS_A, HEAD_DIM))
        outs[5].append(new[2].reshape(db, t, N_HEADS_A, HEAD_DIM))
        outs[6].append(new[4].reshape(db, t, N_HEADS_B, 2, HEAD_DIM))
        outs[7].append(new[5].reshape(db, t, N_HEADS_B, 2 * HEAD_DIM))

    return (xp.reshape(b, s, d), xs.reshape(db, t, d)) + tuple(jnp.stack(o) for o in outs)
```

```python
import functools
import math

import jax
import jax.numpy as jnp
import numpy as np
from jax import lax
from jax.experimental import pallas as pl
from jax.experimental.pallas import tpu as pltpu

D_MODEL = 1024
CHUNK = 64
HEAD_DIM = 64
N_HEADS_A = 8
N_HEADS_B = 4
BAND_PAST_CHUNKS = 8
BAND_PAST = BAND_PAST_CHUNKS * CHUNK
REL_FUT = CHUNK - 1
REL_PAST = 256
N_REL = REL_FUT + REL_PAST + 1
WIDTH = 512
N_GROUPS = 6
D_FF = 4 * D_MODEL
ROPE_THETA = 10000.0
EPS = 1e-6
NEG_INF = -1e30
LOG2E = math.log2(math.e)
Q_SCALE = HEAD_DIM ** -0.5 * LOG2E

LANES = 128
N_BLOCKS = WIDTH // LANES
MXU_DIM = 256
GROUP_CHUNKS = 2
GROUP_Q = GROUP_CHUNKS * CHUNK
GROUP_KEYS = BAND_PAST + GROUP_Q
BIAS_SPAN = GROUP_KEYS + GROUP_Q
Q_BLOCK_B = 256
ROW_TILE = 512
VMEM_LIMIT = 56 * 1024 * 1024

BF16 = jnp.bfloat16
F32 = jnp.float32


def _dot(a, b):
    return jnp.dot(a, b, preferred_element_type=F32)


def _dot_nt(a, b):
    return lax.dot_general(a, b, (((1,), (1,)), ((), ())), preferred_element_type=F32)


def _rms(x, g):
    return x * lax.rsqrt(jnp.mean(x * x, axis=-1, keepdims=True) + EPS) * g


def _const_spec(shape):
    nd = len(shape)
    return pl.BlockSpec(shape, lambda *_: (0,) * nd, pipeline_mode=pl.Buffered(1))


def _params(n_grid_axes):
    return pltpu.CompilerParams(dimension_semantics=("arbitrary",) * n_grid_axes, vmem_limit_bytes=VMEM_LIMIT)


def _lane_halves(q):
    lane = lax.broadcasted_iota(jnp.int32, q.shape, 1)
    zero = jnp.zeros_like(q)
    return jnp.concatenate([jnp.where(lane < HEAD_DIM, q, zero), jnp.where(lane >= HEAD_DIM, q, zero)], axis=0)


def _pick_halves(o):
    m = o.shape[0] // 2
    lane = lax.broadcasted_iota(jnp.int32, (m, LANES), 1)
    return jnp.where(lane < HEAD_DIM, o[:m], o[m:])


def _softmax_parts(parts):
    m = functools.reduce(jnp.maximum, [jnp.max(s, axis=-1, keepdims=True) for s in parts])
    ps = [jnp.exp2(s - m) for s in parts]
    l = functools.reduce(lambda a, b: a + b, [jnp.sum(p, axis=-1, keepdims=True) for p in ps])
    return ps, l


def _diff_combine(ps, l, lam, t):
    c = lam * l[:t] / l[t:]
    return [(p[:t] - p[t:] * c).astype(BF16) for p in ps], 1.0 / l[:t]


def _proj_kernel(x_ref, g1_ref, w_ref, gains_ref, cos_ref, sin_ref, pool_ref, *out_refs, tiles_per_seq):
    h = _rms(x_ref[...], g1_ref[...]).astype(BF16)
    cos = jnp.concatenate([cos_ref[...]] * N_BLOCKS, axis=1)
    sin = jnp.concatenate([sin_ref[...]] * N_BLOCKS, axis=1)
    pool = pool_ref[...]

    def group(i):
        return _dot(h, w_ref[:, i * WIDTH:(i + 1) * WIDTH])

    def head_norm(z, i):
        zz = (z * z).astype(BF16)
        ms = jnp.concatenate([_dot(zz[:, k * MXU_DIM:(k + 1) * MXU_DIM], pool) for k in range(WIDTH // MXU_DIM)],
                             axis=1)
        return z * lax.rsqrt(ms + EPS) * gains_ref[i:i + 1, :]

    def rope(z):
        lane = lax.broadcasted_iota(jnp.int32, z.shape, 1)
        half = HEAD_DIM // 2
        partner = jnp.where(lane % HEAD_DIM < half, pltpu.roll(z, WIDTH - half, 1), pltpu.roll(z, half, 1))
        return z * cos + partner * sin

    if tiles_per_seq == 0:
        qa_ref, ka_ref, va_ref, qb_ref, kb_ref, vb_ref = out_refs
        qa_ref[...] = (head_norm(group(0), 0) * Q_SCALE).astype(BF16)
        ka_ref[...] = head_norm(group(1), 1)
        va_ref[...] = group(2)
        qb_ref[...] = (rope(head_norm(group(3), 2)) * Q_SCALE).astype(BF16)
        kb_ref[...] = rope(head_norm(group(4), 3))
        vb_ref[...] = group(5)
        return

    qa_ref, kat_ref, va_ref, qb_ref, kbt_ref, vb4_ref, kat_tail_ref, vat_tail_ref = out_refs
    tm = x_ref.shape[0]
    kbt_ref[0] = rope(head_norm(group(4), 3)).T
    kat = head_norm(group(1), 1).T
    kat_ref[0] = kat.astype(BF16)
    va = group(2)
    va_ref[...] = va.astype(BF16)
    qb_ref[...] = (rope(head_norm(group(3), 2)) * Q_SCALE).astype(BF16)
    qa_ref[...] = (head_norm(group(0), 0) * Q_SCALE).astype(BF16)
    vb = group(5)
    for hd in range(N_BLOCKS):
        vb4_ref[pl.ds(hd, tm, stride=N_BLOCKS), :] = vb[:, hd * LANES:(hd + 1) * LANES]

    @pl.when(pl.program_id(0) % tiles_per_seq == tiles_per_seq - 1)
    def _():
        kat_tail_ref[0] = kat
        vat_tail_ref[0] = va.T


def _proj_specs(tm, n_pos_blocks, consts):
    row = lambda w: pl.BlockSpec((tm, w), lambda i: (i, 0))
    pos = pl.BlockSpec((tm, LANES), lambda i: (i % n_pos_blocks, 0))
    g1, w_in, gains, pool = consts
    return row, [row(D_MODEL), _const_spec(g1.shape), _const_spec(w_in.shape), _const_spec(gains.shape),
                 pos, pos, _const_spec(pool.shape)]


def _proj_sample(x2d, g1, w_in, gains, cos, sin, pool):
    n = x2d.shape[0]
    row, in_specs = _proj_specs(n, 1, (g1, w_in, gains, pool))
    out_dtypes = (BF16, F32, F32, BF16, F32, F32)
    return pl.pallas_call(
        functools.partial(_proj_kernel, tiles_per_seq=0),
        grid=(1,),
        in_specs=in_specs,
        out_specs=[row(WIDTH)] * N_GROUPS,
        out_shape=[jax.ShapeDtypeStruct((n, WIDTH), dt) for dt in out_dtypes],
        compiler_params=_params(1),
        name="proj_sample",
    )(x2d, g1, w_in, gains, cos, sin, pool)


def _proj_prompt(x2d, b, s, g1, w_in, gains, cos, sin, pool):
    n = x2d.shape[0]
    tm = ROW_TILE
    tps = s // tm
    row, in_specs = _proj_specs(tm, tps, (g1, w_in, gains, pool))
    feat = pl.BlockSpec((1, WIDTH, tm), lambda i: (i // tps, 0, i % tps))
    tail = pl.BlockSpec((1, WIDTH, tm), lambda i: (i // tps, 0, 0))
    out_specs = [row(WIDTH), feat, row(WIDTH), row(WIDTH), feat,
                 pl.BlockSpec((tm * N_BLOCKS, LANES), lambda i: (i, 0)), tail, tail]
    out_shape = [jax.ShapeDtypeStruct((n, WIDTH), BF16),
                 jax.ShapeDtypeStruct((b, WIDTH, s), BF16),
                 jax.ShapeDtypeStruct((n, WIDTH), BF16),
                 jax.ShapeDtypeStruct((n, WIDTH), BF16),
                 jax.ShapeDtypeStruct((b, WIDTH, s), F32),
                 jax.ShapeDtypeStruct((n * N_BLOCKS, LANES), F32),
                 jax.ShapeDtypeStruct((b, WIDTH, tm), F32),
                 jax.ShapeDtypeStruct((b, WIDTH, tm), F32)]
    return pl.pallas_call(
        functools.partial(_proj_kernel, tiles_per_seq=tps),
        grid=(n // tm,),
        in_specs=in_specs,
        out_specs=out_specs,
        out_shape=out_shape,
        compiler_params=_params(1),
        name="proj_prompt",
    )(x2d, g1, w_in, gains, cos, sin, pool)


def _group_bias(f_ref):
    def one(hd):
        x = jnp.broadcast_to(f_ref[hd:hd + 1, :], (GROUP_Q, BIAS_SPAN))
        return pltpu.roll(x, BIAS_SPAN - GROUP_Q + 1, 1, stride=1, stride_axis=0)[:, :GROUP_KEYS]

    bias = jnp.concatenate([one(0), one(1)], axis=0)
    qc = (lax.broadcasted_iota(jnp.int32, bias.shape, 0) % GROUP_Q) // CHUNK
    kc = lax.broadcasted_iota(jnp.int32, bias.shape, 1) // CHUNK
    return jnp.where((kc >= qc) & (kc <= qc + BAND_PAST_CHUNKS), bias, NEG_INF)


def _band_prompt_kernel(q_ref, kt_ref, v_ref, f_ref, o_ref, bias_ref):
    n_groups = q_ref.shape[1] // GROUP_Q
    bias_ref[...] = _group_bias(f_ref.at[0]) * LOG2E

    def window(g):
        return max(0, g * GROUP_Q - BAND_PAST), (g + 1) * GROUP_Q

    def scores(g):
        lo, hi = window(g)
        qm = _lane_halves(q_ref[0, g * GROUP_Q:(g + 1) * GROUP_Q, :])
        return _dot(qm, kt_ref[0, :, lo:hi]) + bias_ref[:, GROUP_KEYS - (hi - lo):]

    nxt = scores(0)
    for g in range(n_groups):
        s = nxt
        if g + 1 < n_groups:
            nxt = scores(g + 1)
        lo, hi = window(g)
        (p,), l = _softmax_parts([s])
        o = _dot(p.astype(BF16), v_ref[0, lo:hi, :]) * (1.0 / l)
        o_ref[0, g * GROUP_Q:(g + 1) * GROUP_Q, :] = _pick_halves(o).astype(o_ref.dtype)


def _band_prompt(qa, kat, va, f):
    b, s, _ = qa.shape
    blk = pl.BlockSpec((1, s, LANES), lambda i, j: (i, 0, j))
    blk_t = pl.BlockSpec((1, LANES, s), lambda i, j: (i, j, 0))
    return pl.pallas_call(
        _band_prompt_kernel,
        grid=(b, N_BLOCKS),
        in_specs=[blk, blk_t, blk, pl.BlockSpec((1, 2, BIAS_SPAN), lambda i, j: (j, 0, 0))],
        out_specs=blk,
        out_shape=jax.ShapeDtypeStruct((b, s, WIDTH), BF16),
        scratch_shapes=[pltpu.VMEM((2 * GROUP_Q, GROUP_KEYS), F32)],
        compiler_params=_params(2),
        name="band_prompt",
    )(qa, kat, va, f)


def _lambda(lam_ref, lam_init):
    e1 = jnp.exp(jnp.sum(lam_ref[0:1, :] * lam_ref[1:2, :], axis=-1, keepdims=True))
    e2 = jnp.exp(jnp.sum(lam_ref[2:3, :] * lam_ref[3:4, :], axis=-1, keepdims=True))
    return e1 - e2 + lam_init


def _diff_out(o, subln, lam_init):
    return _rms(o, subln) * (1.0 - lam_init)


def _diff_prompt_kernel(q_ref, kt_ref, v4_ref, lam_ref, subln_ref, o_ref, kbf_ref, vbf_ref, *, lam_init):
    s_len = q_ref.shape[1]
    tq = Q_BLOCK_B
    kbf_ref[...] = kt_ref[0].astype(BF16)
    vbf_ref[...] = v4_ref[0, pl.ds(pl.program_id(1), s_len, stride=N_BLOCKS), :].astype(BF16)
    lam = _lambda(lam_ref, lam_init)
    row = lax.broadcasted_iota(jnp.int32, (2 * tq, tq), 0) % tq
    col = lax.broadcasted_iota(jnp.int32, (2 * tq, tq), 1)
    diag_mask = jnp.where(col // CHUNK <= row // CHUNK, 0.0, NEG_INF).astype(F32)

    def scores(i):
        r0 = i * tq
        qs = _lane_halves(q_ref[0, r0:r0 + tq, :])
        parts = [_dot(qs, kbf_ref[:, r0:r0 + tq]) + diag_mask]
        if i > 0:
            parts.append(_dot(qs, kbf_ref[:, 0:r0]))
        return parts

    n_blocks = s_len // tq
    nxt = scores(0)
    for i in range(n_blocks):
        r0 = i * tq
        parts = nxt
        if i + 1 < n_blocks:
            nxt = scores(i + 1)
        a, inv = _diff_combine(*_softmax_parts(parts), lam, tq)
        o = _dot(a[0], vbf_ref[r0:r0 + tq, :])
        if i > 0:
            o = o + _dot(a[1], vbf_ref[0:r0, :])
        o_ref[0, r0:r0 + tq, :] = _diff_out(o * inv, subln_ref[...], lam_init).astype(o_ref.dtype)


def _diff_prompt(qb, kbt, vb4, lam_vecs, subln, lam_init):
    b, s, _ = qb.shape
    blk = pl.BlockSpec((1, s, LANES), lambda i, j: (i, 0, j))
    return pl.pallas_call(
        functools.partial(_diff_prompt_kernel, lam_init=lam_init),
        grid=(b, N_BLOCKS),
        in_specs=[blk, pl.BlockSpec((1, LANES, s), lambda i, j: (i, j, 0)),
                  pl.BlockSpec((1, s * N_BLOCKS, LANES), lambda i, j: (i, 0, 0)),
                  _const_spec(lam_vecs.shape), _const_spec(subln.shape)],
        out_specs=blk,
        out_shape=jax.ShapeDtypeStruct((b, s, WIDTH), BF16),
        scratch_shapes=[pltpu.VMEM((LANES, s), BF16), pltpu.VMEM((s, LANES), BF16)],
        compiler_params=_params(2),
        name="diff_prompt",
    )(qb, kbt, vb4, lam_vecs, subln)


def _sample_attn_kernel(qa_ref, ka_ref, va_ref, qb_ref, kb_ref, vb_ref, cakt_ref, cavt_ref, cbkt_ref, cbv4_ref,
                        bias_c_ref, bias_n_ref, lam_ref, subln_ref, ya_ref, yb_ref, *, lam_init):
    t = qa_ref.shape[0]
    past = cbkt_ref.shape[2]
    lam = _lambda(lam_ref, lam_init)
    for j in range(N_BLOCKS):
        cols = slice(j * LANES, (j + 1) * LANES)
        qm = _lane_halves(qa_ref[:, cols])
        kn = ka_ref[:, cols].astype(BF16)
        vn = va_ref[:, cols].astype(BF16)
        kct = cakt_ref[0, cols, :].astype(BF16)
        vct = cavt_ref[0, cols, :].astype(BF16)
        (pc, pn), l = _softmax_parts([_dot(qm, kct) + bias_c_ref[j] * LOG2E, _dot_nt(qm, kn) + bias_n_ref[j] * LOG2E])
        o = (_dot_nt(pc.astype(BF16), vct) + _dot(pn.astype(BF16), vn)) * (1.0 / l)
        ya_ref[:, cols] = _pick_halves(o).astype(ya_ref.dtype)
        qs = _lane_halves(qb_ref[:, cols])
        kn = kb_ref[:, cols].astype(BF16)
        vn = vb_ref[:, cols].astype(BF16)
        kct = cbkt_ref[0, cols, :].astype(BF16)
        vc = cbv4_ref[0, pl.ds(j, past, stride=N_BLOCKS), :].astype(BF16)
        (ac, an), inv = _diff_combine(*_softmax_parts([_dot(qs, kct), _dot_nt(qs, kn)]), lam, t)
        o = (_dot(ac, vc) + _dot(an, vn)) * inv
        yb_ref[:, cols] = _diff_out(o, subln_ref[...], lam_init).astype(yb_ref.dtype)


def _sample_attn(new, caches, bias_c, bias_n, lam_vecs, subln, lam_init, t):
    n = new[0].shape[0]
    b = n // t
    row = pl.BlockSpec((t, WIDTH), lambda i: (i, 0))
    cache = lambda c: pl.BlockSpec((1,) + c.shape[1:], lambda i: (i, 0, 0))
    return pl.pallas_call(
        functools.partial(_sample_attn_kernel, lam_init=lam_init),
        grid=(b,),
        in_specs=[row] * 6 + [cache(c) for c in caches]
        + [_const_spec(bias_c.shape), _const_spec(bias_n.shape), _const_spec(lam_vecs.shape), _const_spec(subln.shape)],
        out_specs=[row, row],
        out_shape=[jax.ShapeDtypeStruct((n, WIDTH), BF16)] * 2,
        compiler_params=_params(1),
        name="sample_attn",
    )(*new, *caches, bias_c, bias_n, lam_vecs, subln)


def _merge_mlp_kernel(x_ref, ya_ref, yb_ref, g1_ref, wg_ref, bg_ref, wpa_ref, wpb_ref, wo_ref,
                      g2_ref, w1_ref, w2_ref, o_ref):
    x = x_ref[...]
    h = _rms(x, g1_ref[...]).astype(BF16)
    m = jax.nn.sigmoid(_dot(h, wg_ref[:, :D_MODEL]) + bg_ref[:, :D_MODEL]) * _dot(ya_ref[...], wpa_ref[...])
    m = m + jax.nn.sigmoid(_dot(h, wg_ref[:, D_MODEL:]) + bg_ref[:, D_MODEL:]) * _dot(yb_ref[...], wpb_ref[...])
    x1 = x + _dot(m.astype(BF16), wo_ref[...])
    hn = _rms(x1, g2_ref[...]).astype(BF16)
    acc = x1
    for c in range(D_FF // D_MODEL):
        cols = slice(c * D_MODEL, (c + 1) * D_MODEL)
        u = jnp.maximum(_dot(hn, w1_ref[:, cols]), 0.0)
        acc = acc + _dot((u * u).astype(BF16), w2_ref[cols, :])
    o_ref[...] = acc


def _merge_mlp(x2d, ya, yb, tm, g1, wg, bg, wpa, wpb, wo, g2, w1, w2):
    n = x2d.shape[0]
    row = lambda w: pl.BlockSpec((tm, w), lambda i: (i, 0))
    consts = (g1, wg, bg, wpa, wpb, wo, g2, w1, w2)
    return pl.pallas_call(
        _merge_mlp_kernel,
        grid=(n // tm,),
        in_specs=[row(D_MODEL), row(WIDTH), row(WIDTH)] + [_const_spec(c.shape) for c in consts],
        out_specs=row(D_MODEL),
        out_shape=jax.ShapeDtypeStruct((n, D_MODEL), F32),
        compiler_params=_params(1),
        name="merge_mlp",
    )(x2d, ya, yb, *consts)


def _rope_tables(pos):
    half = HEAD_DIM // 2
    inv_freq = ROPE_THETA ** (-jnp.arange(half, dtype=F32) / half)
    ang = pos.astype(F32)[:, None] * inv_freq[None, :]
    cos = jnp.cos(ang)
    sin = jnp.sin(ang)
    reps = LANES // HEAD_DIM
    return (jnp.tile(jnp.concatenate([cos, cos], axis=1), (1, reps)),
            jnp.tile(jnp.concatenate([-sin, sin], axis=1), (1, reps)))


def _pair_rows(bias):
    h, q, k = bias.shape
    return bias.reshape(h // 2, 2 * q, k)


def _band_offset_table(table):
    n_far = BAND_PAST + GROUP_Q - 1 - REL_PAST + 1
    n_fut = BIAS_SPAN - n_far - (N_REL - 1)
    h = table.shape[0]
    f = jnp.concatenate([jnp.broadcast_to(table[:, N_REL - 1:], (h, n_far)), table[:, N_REL - 2::-1],
                         jnp.broadcast_to(table[:, :1], (h, n_fut))], axis=1).astype(F32)
    return f.reshape(h // 2, 2, BIAS_SPAN)


def _band_bias_sample(table, past_len, lc, t):
    q_pos = past_len + np.arange(t)
    k_pos = np.concatenate([past_len - lc + np.arange(lc), past_len + np.arange(t)])
    idx = np.clip(q_pos[:, None] - k_pos[None, :], -REL_FUT, REL_PAST) + REL_FUT
    bias = _pair_rows(table[:, idx].astype(F32))
    return bias[:, :, :lc], bias[:, :, lc:]


def _pool_matrix():
    i = np.arange(MXU_DIM)
    return jnp.asarray((i[:, None] // HEAD_DIM == i[None, :] // HEAD_DIM) / HEAD_DIM, dtype=BF16)


def kernel(x_prompt, x_sample, cache_a_k, cache_a_v, cache_b_k, cache_b_v, ln1_g, w_in, qn_a, kn_a, rel_bias, qn_b, kn_b, lam_q1, lam_k1, lam_q2, lam_k2, subln_g, w_gate, b_gate, w_proj_a, w_proj_b, w_out, ln2_g, w_ff1, w_ff2):
    depth = w_in.shape[0]
    b, s, d = x_prompt.shape
    db, t, _ = x_sample.shape
    past_len = cache_b_k.shape[2]
    lc = cache_a_k.shape[2]
    keep = min(BAND_PAST, s)
    assert d == D_MODEL and s % ROW_TILE == 0 and keep == ROW_TILE

    cos_p, sin_p = _rope_tables(jnp.arange(s))
    cos_s, sin_s = (jnp.tile(a, (db, 1)) for a in _rope_tables(past_len + jnp.arange(t)))
    pool = _pool_matrix()
    tile_heads = lambda g: jnp.tile(g, WIDTH // HEAD_DIM)

    xp = x_prompt.reshape(b * s, d)
    xs = x_sample.reshape(db * t, d)
    outs = [[] for _ in range(8)]
    for l in range(depth):
        lam_init = 0.8 - 0.6 * math.exp(-0.3 * l)
        g1 = ln1_g[l][None]
        g2 = ln2_g[l][None]
        gains = jnp.stack([tile_heads(qn_a[l]), tile_heads(kn_a[l]), tile_heads(qn_b[l]), tile_heads(kn_b[l])])
        lam_vecs = jnp.stack([lam_q1[l], lam_k1[l], lam_q2[l], lam_k2[l]])
        subln = subln_g[l][None]
        bias_f = _band_offset_table(rel_bias[l])
        bias_c, bias_n = _band_bias_sample(rel_bias[l], past_len, lc, t)
        w_in_l = w_in[l].astype(BF16)
        merge_w = (g1, w_gate[l].astype(BF16), b_gate[l][None], w_proj_a[l].astype(BF16), w_proj_b[l].astype(BF16),
                   w_out[l].astype(BF16), g2, w_ff1[l].astype(BF16), w_ff2[l].astype(BF16))

        qa, kat, va, qb, kbt, vb4, kat_tail, vat_tail = _proj_prompt(xp, b, s, g1, w_in_l, gains, cos_p, sin_p, pool)
        as3 = lambda a: a.reshape(b, s, WIDTH)
        ya = _band_prompt(as3(qa), kat, as3(va), bias_f)
        yb = _diff_prompt(as3(qb), kbt, vb4.reshape(b, s * N_BLOCKS, LANES), lam_vecs, subln, lam_init)
        xp = _merge_mlp(xp, ya.reshape(b * s, WIDTH), yb.reshape(b * s, WIDTH), ROW_TILE, *merge_w)
        token_major_a = lambda a: a.reshape(b, N_HEADS_A, HEAD_DIM, keep).transpose(0, 3, 1, 2)
        outs[0].append(token_major_a(kat_tail))
        outs[1].append(token_major_a(vat_tail))
        outs[2].append(kbt.reshape(b, N_HEADS_B, 2, HEAD_DIM, s).transpose(0, 4, 1, 2, 3))
        outs[3].append(vb4.reshape(b, s, N_HEADS_B, 2 * HEAD_DIM))

        new = _proj_sample(xs, g1, w_in_l, gains, cos_s, sin_s, pool)
        caches = (cache_a_k[l].transpose(0, 2, 3, 1).reshape(db, WIDTH, lc),
                  cache_a_v[l].transpose(0, 2, 3, 1).reshape(db, WIDTH, lc),
                  cache_b_k[l].transpose(0, 2, 3, 4, 1).reshape(db, WIDTH, past_len),
                  cache_b_v[l].reshape(db, past_len * N_BLOCKS, LANES))
        ya_s, yb_s = _sample_attn(new, caches, bias_c, bias_n, lam_vecs, subln, lam_init, t)
        xs = _merge_mlp(xs, ya_s, yb_s, db * t, *merge_w)
        outs[4].append(new[1].reshape(db, t, N_HEADS_A, HEAD_DIM))
        outs[5].append(new[2].reshape(db, t, N_HEADS_A, HEAD_DIM))
        outs[6].append(new[4].reshape(db, t, N_HEADS_B, 2, HEAD_DIM))
        outs[7].append(new[5].reshape(db, t, N_HEADS_B, 2 * HEAD_DIM))

    return (xp.reshape(b, s, d), xs.reshape(db, t, d)) + tuple(jnp.stack(o) for o in outs)
```

```python
import functools
import math

import jax
import jax.numpy as jnp
import numpy as np
from jax import lax
from jax.experimental import pallas as pl
from jax.experimental.pallas import tpu as pltpu

D_MODEL = 1024
CHUNK = 64
HEAD_DIM = 64
N_HEADS_A = 8
N_HEADS_B = 4
BAND_PAST_CHUNKS = 8
BAND_PAST = BAND_PAST_CHUNKS * CHUNK
REL_FUT = CHUNK - 1
REL_PAST = 256
N_REL = REL_FUT + REL_PAST + 1
WIDTH = 512
N_GROUPS = 6
D_FF = 4 * D_MODEL
ROPE_THETA = 10000.0
EPS = 1e-6
NEG_INF = -1e30
LOG2E = math.log2(math.e)
Q_SCALE = HEAD_DIM ** -0.5 * LOG2E

LANES = 128
N_BLOCKS = WIDTH // LANES
MXU_DIM = 256
GROUP_CHUNKS = 2
GROUP_Q = GROUP_CHUNKS * CHUNK
GROUP_KEYS = BAND_PAST + GROUP_Q
BIAS_SPAN = GROUP_KEYS + GROUP_Q
Q_BLOCK_B = 256
ROW_TILE = 512
VMEM_LIMIT = 56 * 1024 * 1024

BF16 = jnp.bfloat16
F32 = jnp.float32


def _dot(a, b):
    return jnp.dot(a, b, preferred_element_type=F32)


def _dot_nt(a, b):
    return lax.dot_general(a, b, (((1,), (1,)), ((), ())), preferred_element_type=F32)


def _rms(x, g):
    return x * lax.rsqrt(jnp.mean(x * x, axis=-1, keepdims=True) + EPS) * g


def _const_spec(shape):
    nd = len(shape)
    return pl.BlockSpec(shape, lambda *_: (0,) * nd, pipeline_mode=pl.Buffered(1))


def _params(n_grid_axes):
    return pltpu.CompilerParams(dimension_semantics=("arbitrary",) * n_grid_axes, vmem_limit_bytes=VMEM_LIMIT)


def _lane_halves(q):
    lane = lax.broadcasted_iota(jnp.int32, q.shape, 1)
    zero = jnp.zeros_like(q)
    return jnp.concatenate([jnp.where(lane < HEAD_DIM, q, zero), jnp.where(lane >= HEAD_DIM, q, zero)], axis=0)


def _pick_halves(o):
    m = o.shape[0] // 2
    lane = lax.broadcasted_iota(jnp.int32, (m, LANES), 1)
    return jnp.where(lane < HEAD_DIM, o[:m], o[m:])


def _softmax_parts(parts):
    m = functools.reduce(jnp.maximum, [jnp.max(s, axis=-1, keepdims=True) for s in parts])
    ps = [jnp.exp2(s - m) for s in parts]
    l = functools.reduce(lambda a, b: a + b, [jnp.sum(p, axis=-1, keepdims=True) for p in ps])
    return ps, l


def _diff_combine(ps, l, lam, t):
    c = lam * l[:t] / l[t:]
    return [(p[:t] - p[t:] * c).astype(BF16) for p in ps], 1.0 / l[:t]


def _pair_bias(f_ref, rows):
    def one(hd):
        x = jnp.broadcast_to(f_ref[hd:hd + 1, :], (rows, BIAS_SPAN))
        return pltpu.roll(x, BIAS_SPAN - GROUP_Q + 1, 1, stride=1, stride_axis=0)

    return jnp.concatenate([one(0), one(1)], axis=0) * LOG2E


def _proj_kernel(x_ref, g1_ref, w_ref, gains_ref, cos_ref, sin_ref, pool_ref, *refs, tiles_per_seq):
    h = _rms(x_ref[...], g1_ref[...]).astype(BF16)
    cos = jnp.concatenate([cos_ref[...]] * N_BLOCKS, axis=1)
    sin = jnp.concatenate([sin_ref[...]] * N_BLOCKS, axis=1)
    pool = pool_ref[...]

    def group(i):
        return _dot(h, w_ref[:, i * WIDTH:(i + 1) * WIDTH])

    def head_norm(z, i):
        zz = (z * z).astype(BF16)
        ms = jnp.concatenate([_dot(zz[:, k * MXU_DIM:(k + 1) * MXU_DIM], pool) for k in range(WIDTH // MXU_DIM)],
                             axis=1)
        return z * lax.rsqrt(ms + EPS) * gains_ref[i:i + 1, :]

    def rope(z):
        lane = lax.broadcasted_iota(jnp.int32, z.shape, 1)
        half = HEAD_DIM // 2
        partner = jnp.where(lane % HEAD_DIM < half, pltpu.roll(z, WIDTH - half, 1), pltpu.roll(z, half, 1))
        return z * cos + partner * sin

    if tiles_per_seq == 0:
        qa_ref, ka_ref, va_ref, qb_ref, kb_ref, vb_ref = refs
        qa_ref[...] = (head_norm(group(0), 0) * Q_SCALE).astype(BF16)
        ka_ref[...] = head_norm(group(1), 1)
        va_ref[...] = group(2)
        qb_ref[...] = (rope(head_norm(group(3), 2)) * Q_SCALE).astype(BF16)
        kb_ref[...] = rope(head_norm(group(4), 3))
        vb_ref[...] = group(5)
        return

    kgain_ref, cos_t_ref, sin_t_ref = refs[:3]
    qa_ref, kat_ref, va_ref, qb_ref, kbt_ref, vb4_ref, kat_tail_ref, vat_tail_ref = refs[3:]
    tm = x_ref.shape[0]
    n_heads = WIDTH // HEAD_DIM

    def head_norm_t(zt, i):
        z3 = zt.reshape(n_heads, HEAD_DIM, tm)
        ms = jnp.mean(z3 * z3, axis=1, keepdims=True)
        gain = jnp.concatenate([kgain_ref[i]] * (tm // LANES), axis=1)
        return z3 * lax.rsqrt(ms + EPS) * gain[None]

    def rope_t(z3):
        half = HEAD_DIM // 2
        cos_t = cos_t_ref[...][None]
        sin_t = sin_t_ref[...][None]
        x1, x2 = z3[:, :half], z3[:, half:]
        return jnp.concatenate([x1 * cos_t - x2 * sin_t, x2 * cos_t + x1 * sin_t], axis=1)

    kbt_ref[0] = rope_t(head_norm_t(group(4).T, 1)).reshape(WIDTH, tm)
    kat = head_norm_t(group(1).T, 0).reshape(WIDTH, tm)
    kat_ref[0] = kat.astype(BF16)
    va = group(2)
    va_ref[...] = va.astype(BF16)
    qb_ref[...] = (rope(head_norm(group(3), 2)) * Q_SCALE).astype(BF16)
    qa_ref[...] = (head_norm(group(0), 0) * Q_SCALE).astype(BF16)
    vb = group(5)
    for hd in range(N_BLOCKS):
        vb4_ref[pl.ds(hd, tm, stride=N_BLOCKS), :] = vb[:, hd * LANES:(hd + 1) * LANES]

    @pl.when(pl.program_id(0) % tiles_per_seq == tiles_per_seq - 1)
    def _():
        kat_tail_ref[0] = kat
        vat_tail_ref[0] = va.T


def _proj_specs(tm, n_pos_blocks, consts):
    row = lambda w: pl.BlockSpec((tm, w), lambda i: (i, 0))
    pos = pl.BlockSpec((tm, LANES), lambda i: (i % n_pos_blocks, 0))
    g1, w_in, gains, pool = consts
    return row, [row(D_MODEL), _const_spec(g1.shape), _const_spec(w_in.shape), _const_spec(gains.shape),
                 pos, pos, _const_spec(pool.shape)]


def _proj_sample(x2d, g1, w_in, gains, cos, sin, pool):
    n = x2d.shape[0]
    row, in_specs = _proj_specs(n, 1, (g1, w_in, gains, pool))
    out_dtypes = (BF16, F32, F32, BF16, F32, F32)
    return pl.pallas_call(
        functools.partial(_proj_kernel, tiles_per_seq=0),
        grid=(1,),
        in_specs=in_specs,
        out_specs=[row(WIDTH)] * N_GROUPS,
        out_shape=[jax.ShapeDtypeStruct((n, WIDTH), dt) for dt in out_dtypes],
        compiler_params=_params(1),
        name="proj_sample",
    )(x2d, g1, w_in, gains, cos, sin, pool)


def _proj_prompt(x2d, b, s, g1, w_in, gains, cos, sin, pool, kgain_t, cos_t, sin_t):
    n = x2d.shape[0]
    tm = ROW_TILE
    tps = s // tm
    row, in_specs = _proj_specs(tm, tps, (g1, w_in, gains, pool))
    pos_t = pl.BlockSpec((HEAD_DIM // 2, tm), lambda i: (0, i % tps))
    in_specs = in_specs + [_const_spec(kgain_t.shape), pos_t, pos_t]
    feat = pl.BlockSpec((1, WIDTH, tm), lambda i: (i // tps, 0, i % tps))
    tail = pl.BlockSpec((1, WIDTH, tm), lambda i: (i // tps, 0, 0))
    out_specs = [row(WIDTH), feat, row(WIDTH), row(WIDTH), feat,
                 pl.BlockSpec((tm * N_BLOCKS, LANES), lambda i: (i, 0)), tail, tail]
    out_shape = [jax.ShapeDtypeStruct((n, WIDTH), BF16),
                 jax.ShapeDtypeStruct((b, WIDTH, s), BF16),
                 jax.ShapeDtypeStruct((n, WIDTH), BF16),
                 jax.ShapeDtypeStruct((n, WIDTH), BF16),
                 jax.ShapeDtypeStruct((b, WIDTH, s), F32),
                 jax.ShapeDtypeStruct((n * N_BLOCKS, LANES), F32),
                 jax.ShapeDtypeStruct((b, WIDTH, tm), F32),
                 jax.ShapeDtypeStruct((b, WIDTH, tm), F32)]
    return pl.pallas_call(
        functools.partial(_proj_kernel, tiles_per_seq=tps),
        grid=(n // tm,),
        in_specs=in_specs,
        out_specs=out_specs,
        out_shape=out_shape,
        compiler_params=_params(1),
        name="proj_prompt",
    )(x2d, g1, w_in, gains, cos, sin, pool, kgain_t, cos_t, sin_t)


def _group_bias(f_ref):
    bias = _pair_bias(f_ref, GROUP_Q)[:, :GROUP_KEYS]
    qc = (lax.broadcasted_iota(jnp.int32, bias.shape, 0) % GROUP_Q) // CHUNK
    kc = lax.broadcasted_iota(jnp.int32, bias.shape, 1) // CHUNK
    return jnp.where((kc >= qc) & (kc <= qc + BAND_PAST_CHUNKS), bias, NEG_INF)


def _band_prompt_kernel(q_ref, kt_ref, v_ref, f_ref, o_ref, bias_ref):
    n_groups = q_ref.shape[1] // GROUP_Q
    bias_ref[...] = _group_bias(f_ref.at[0])

    def window(g):
        return max(0, g * GROUP_Q - BAND_PAST), (g + 1) * GROUP_Q

    def scores(g):
        lo, hi = window(g)
        qm = _lane_halves(q_ref[0, g * GROUP_Q:(g + 1) * GROUP_Q, :])
        return _dot(qm, kt_ref[0, :, lo:hi]) + bias_ref[:, GROUP_KEYS - (hi - lo):]

    nxt = scores(0)
    for g in range(n_groups):
        s = nxt
        if g + 1 < n_groups:
            nxt = scores(g + 1)
        lo, hi = window(g)
        (p,), l = _softmax_parts([s])
        o = _dot(p.astype(BF16), v_ref[0, lo:hi, :]) * (1.0 / l)
        o_ref[0, g * GROUP_Q:(g + 1) * GROUP_Q, :] = _pick_halves(o).astype(o_ref.dtype)


def _band_prompt(qa, kat, va, f):
    b, s, _ = qa.shape
    blk = pl.BlockSpec((1, s, LANES), lambda i, j: (i, 0, j))
    blk_t = pl.BlockSpec((1, LANES, s), lambda i, j: (i, j, 0))
    return pl.pallas_call(
        _band_prompt_kernel,
        grid=(b, N_BLOCKS),
        in_specs=[blk, blk_t, blk, pl.BlockSpec((1, 2, BIAS_SPAN), lambda i, j: (j, 0, 0))],
        out_specs=blk,
        out_shape=jax.ShapeDtypeStruct((b, s, WIDTH), BF16),
        scratch_shapes=[pltpu.VMEM((2 * GROUP_Q, GROUP_KEYS), F32)],
        compiler_params=_params(2),
        name="band_prompt",
    )(qa, kat, va, f)


def _lambda(lam_ref, lam_init):
    e1 = jnp.exp(jnp.sum(lam_ref[0:1, :] * lam_ref[1:2, :], axis=-1, keepdims=True))
    e2 = jnp.exp(jnp.sum(lam_ref[2:3, :] * lam_ref[3:4, :], axis=-1, keepdims=True))
    return e1 - e2 + lam_init


def _diff_out(o, subln, lam_init):
    return _rms(o, subln) * (1.0 - lam_init)


def _diff_prompt_kernel(q_ref, kt_ref, v4_ref, lam_ref, subln_ref, o_ref, kbf_ref, vbf_ref, *, lam_init):
    s_len = q_ref.shape[1]
    tq = Q_BLOCK_B
    kbf_ref[...] = kt_ref[0].astype(BF16)
    vbf_ref[...] = v4_ref[0, pl.ds(pl.program_id(1), s_len, stride=N_BLOCKS), :].astype(BF16)
    lam = _lambda(lam_ref, lam_init)
    row = lax.broadcasted_iota(jnp.int32, (2 * tq, tq), 0) % tq
    col = lax.broadcasted_iota(jnp.int32, (2 * tq, tq), 1)
    diag_mask = jnp.where(col // CHUNK <= row // CHUNK, 0.0, NEG_INF).astype(F32)

    def scores(i):
        r0 = i * tq
        qs = _lane_halves(q_ref[0, r0:r0 + tq, :])
        parts = [_dot(qs, kbf_ref[:, r0:r0 + tq]) + diag_mask]
        if i > 0:
            parts.append(_dot(qs, kbf_ref[:, 0:r0]))
        return parts

    n_blocks = s_len // tq
    nxt = scores(0)
    for i in range(n_blocks):
        r0 = i * tq
        parts = nxt
        if i + 1 < n_blocks:
            nxt = scores(i + 1)
        a, inv = _diff_combine(*_softmax_parts(parts), lam, tq)
        o = _dot(a[0], vbf_ref[r0:r0 + tq, :])
        if i > 0:
            o = o + _dot(a[1], vbf_ref[0:r0, :])
        o_ref[0, r0:r0 + tq, :] = _diff_out(o * inv, subln_ref[...], lam_init).astype(o_ref.dtype)


def _diff_prompt(qb, kbt, vb4, lam_vecs, subln, lam_init):
    b, s, _ = qb.shape
    blk = pl.BlockSpec((1, s, LANES), lambda i, j: (i, 0, j))
    return pl.pallas_call(
        functools.partial(_diff_prompt_kernel, lam_init=lam_init),
        grid=(b, N_BLOCKS),
        in_specs=[blk, pl.BlockSpec((1, LANES, s), lambda i, j: (i, j, 0)),
                  pl.BlockSpec((1, s * N_BLOCKS, LANES), lambda i, j: (i, 0, 0)),
                  _const_spec(lam_vecs.shape), _const_spec(subln.shape)],
        out_specs=blk,
        out_shape=jax.ShapeDtypeStruct((b, s, WIDTH), BF16),
        scratch_shapes=[pltpu.VMEM((LANES, s), BF16), pltpu.VMEM((s, LANES), BF16)],
        compiler_params=_params(2),
        name="diff_prompt",
    )(qb, kbt, vb4, lam_vecs, subln)


def _sample_attn_kernel(qa_ref, ka_ref, va_ref, qb_ref, kb_ref, vb_ref, cakt_ref, cavt_ref, cbkt_ref, cbv4_ref,
                        f_ref, lam_ref, subln_ref, ya_ref, yb_ref, *, lam_init):
    t = qa_ref.shape[0]
    lc = cakt_ref.shape[2]
    past = cbkt_ref.shape[2]
    lam = _lambda(lam_ref, lam_init)
    for j in range(N_BLOCKS):
        cols = slice(j * LANES, (j + 1) * LANES)
        bias = _pair_bias(f_ref.at[j], t)
        qm = _lane_halves(qa_ref[:, cols])
        kn = ka_ref[:, cols].astype(BF16)
        vn = va_ref[:, cols].astype(BF16)
        kct = cakt_ref[0, cols, :].astype(BF16)
        vct = cavt_ref[0, cols, :].astype(BF16)
        (pc, pn), l = _softmax_parts([_dot(qm, kct) + bias[:, :lc], _dot_nt(qm, kn) + bias[:, lc:lc + t]])
        o = (_dot_nt(pc.astype(BF16), vct) + _dot(pn.astype(BF16), vn)) * (1.0 / l)
        ya_ref[:, cols] = _pick_halves(o).astype(ya_ref.dtype)
        qs = _lane_halves(qb_ref[:, cols])
        kn = kb_ref[:, cols].astype(BF16)
        vn = vb_ref[:, cols].astype(BF16)
        kct = cbkt_ref[0, cols, :].astype(BF16)
        vc = cbv4_ref[0, pl.ds(j, past, stride=N_BLOCKS), :].astype(BF16)
        (ac, an), inv = _diff_combine(*_softmax_parts([_dot(qs, kct), _dot_nt(qs, kn)]), lam, t)
        o = (_dot(ac, vc) + _dot(an, vn)) * inv
        yb_ref[:, cols] = _diff_out(o, subln_ref[...], lam_init).astype(yb_ref.dtype)


def _sample_attn(new, caches, f, lam_vecs, subln, lam_init, t):
    n = new[0].shape[0]
    b = n // t
    row = pl.BlockSpec((t, WIDTH), lambda i: (i, 0))
    cache = lambda c: pl.BlockSpec((1,) + c.shape[1:], lambda i: (i, 0, 0))
    return pl.pallas_call(
        functools.partial(_sample_attn_kernel, lam_init=lam_init),
        grid=(b,),
        in_specs=[row] * 6 + [cache(c) for c in caches]
        + [_const_spec(f.shape), _const_spec(lam_vecs.shape), _const_spec(subln.shape)],
        out_specs=[row, row],
        out_shape=[jax.ShapeDtypeStruct((n, WIDTH), BF16)] * 2,
        compiler_params=_params(1),
        name="sample_attn",
    )(*new, *caches, f, lam_vecs, subln)


def _merge_mlp_kernel(x_ref, ya_ref, yb_ref, g1_ref, wg_ref, bg_ref, wpa_ref, wpb_ref, wo_ref,
                      g2_ref, w1_ref, w2_ref, o_ref):
    x = x_ref[...]
    h = _rms(x, g1_ref[...]).astype(BF16)
    m = jax.nn.sigmoid(_dot(h, wg_ref[:, :D_MODEL]) + bg_ref[:, :D_MODEL]) * _dot(ya_ref[...], wpa_ref[...])
    m = m + jax.nn.sigmoid(_dot(h, wg_ref[:, D_MODEL:]) + bg_ref[:, D_MODEL:]) * _dot(yb_ref[...], wpb_ref[...])
    x1 = x + _dot(m.astype(BF16), wo_ref[...])
    hn = _rms(x1, g2_ref[...]).astype(BF16)
    acc = x1
    for c in range(D_FF // D_MODEL):
        cols = slice(c * D_MODEL, (c + 1) * D_MODEL)
        u = jnp.maximum(_dot(hn, w1_ref[:, cols]), 0.0)
        acc = acc + _dot((u * u).astype(BF16), w2_ref[cols, :])
    o_ref[...] = acc


def _merge_mlp(x2d, ya, yb, tm, g1, wg, bg, wpa, wpb, wo, g2, w1, w2):
    n = x2d.shape[0]
    row = lambda w: pl.BlockSpec((tm, w), lambda i: (i, 0))
    consts = (g1, wg, bg, wpa, wpb, wo, g2, w1, w2)
    return pl.pallas_call(
        _merge_mlp_kernel,
        grid=(n // tm,),
        in_specs=[row(D_MODEL), row(WIDTH), row(WIDTH)] + [_const_spec(c.shape) for c in consts],
        out_specs=row(D_MODEL),
        out_shape=jax.ShapeDtypeStruct((n, D_MODEL), F32),
        compiler_params=_params(1),
        name="merge_mlp",
    )(x2d, ya, yb, *consts)


def _rope_tables(pos):
    half = HEAD_DIM // 2
    inv_freq = ROPE_THETA ** (-jnp.arange(half, dtype=F32) / half)
    ang = pos.astype(F32)[:, None] * inv_freq[None, :]
    cos = jnp.cos(ang)
    sin = jnp.sin(ang)
    reps = LANES // HEAD_DIM
    return (jnp.tile(jnp.concatenate([cos, cos], axis=1), (1, reps)),
            jnp.tile(jnp.concatenate([-sin, sin], axis=1), (1, reps)), cos.T, sin.T)


def _band_offset_table(table):
    n_far = BAND_PAST + GROUP_Q - 1 - REL_PAST + 1
    n_fut = BIAS_SPAN - n_far - (N_REL - 1)
    h = table.shape[0]
    f = jnp.concatenate([jnp.broadcast_to(table[:, N_REL - 1:], (h, n_far)), table[:, N_REL - 2::-1],
                         jnp.broadcast_to(table[:, :1], (h, n_fut))], axis=1).astype(F32)
    return f.reshape(h // 2, 2, BIAS_SPAN)


def _pool_matrix():
    i = np.arange(MXU_DIM)
    return jnp.asarray((i[:, None] // HEAD_DIM == i[None, :] // HEAD_DIM) / HEAD_DIM, dtype=BF16)


def kernel(x_prompt, x_sample, cache_a_k, cache_a_v, cache_b_k, cache_b_v, ln1_g, w_in, qn_a, kn_a, rel_bias, qn_b, kn_b, lam_q1, lam_k1, lam_q2, lam_k2, subln_g, w_gate, b_gate, w_proj_a, w_proj_b, w_out, ln2_g, w_ff1, w_ff2):
    depth = w_in.shape[0]
    b, s, d = x_prompt.shape
    db, t, _ = x_sample.shape
    past_len = cache_b_k.shape[2]
    lc = cache_a_k.shape[2]
    keep = min(BAND_PAST, s)
    assert d == D_MODEL and s % ROW_TILE == 0 and keep == ROW_TILE
    assert lc == BAND_PAST and t <= GROUP_Q

    cos_p, sin_p, cos_pt, sin_pt = _rope_tables(jnp.arange(s))
    cos_s, sin_s = (jnp.tile(a, (db, 1)) for a in _rope_tables(past_len + jnp.arange(t))[:2])
    lane_const = lambda g: jnp.broadcast_to(g[:, None], (HEAD_DIM, LANES))
    pool = _pool_matrix()
    tile_heads = lambda g: jnp.tile(g, WIDTH // HEAD_DIM)

    xp = x_prompt.reshape(b * s, d)
    xs = x_sample.reshape(db * t, d)
    outs = [[] for _ in range(8)]
    for l in range(depth):
        lam_init = 0.8 - 0.6 * math.exp(-0.3 * l)
        g1 = ln1_g[l][None]
        g2 = ln2_g[l][None]
        gains = jnp.stack([tile_heads(qn_a[l]), tile_heads(kn_a[l]), tile_heads(qn_b[l]), tile_heads(kn_b[l])])
        lam_vecs = jnp.stack([lam_q1[l], lam_k1[l], lam_q2[l], lam_k2[l]])
        subln = subln_g[l][None]
        bias_f = _band_offset_table(rel_bias[l])
        w_in_l = w_in[l].astype(BF16)
        merge_w = (g1, w_gate[l].astype(BF16), b_gate[l][None], w_proj_a[l].astype(BF16), w_proj_b[l].astype(BF16),
                   w_out[l].astype(BF16), g2, w_ff1[l].astype(BF16), w_ff2[l].astype(BF16))

        kgain_t = jnp.stack([lane_const(kn_a[l]), lane_const(kn_b[l])])
        qa, kat, va, qb, kbt, vb4, kat_tail, vat_tail = _proj_prompt(xp, b, s, g1, w_in_l, gains, cos_p, sin_p, pool,
                                                                     kgain_t, cos_pt, sin_pt)
        as3 = lambda a: a.reshape(b, s, WIDTH)
        ya = _band_prompt(as3(qa), kat, as3(va), bias_f)
        yb = _diff_prompt(as3(qb), kbt, vb4.reshape(b, s * N_BLOCKS, LANES), lam_vecs, subln, lam_init)
        xp = _merge_mlp(xp, ya.reshape(b * s, WIDTH), yb.reshape(b * s, WIDTH), ROW_TILE, *merge_w)
        token_major_a = lambda a: a.reshape(b, N_HEADS_A, HEAD_DIM, keep).transpose(0, 3, 1, 2)
        outs[0].append(token_major_a(kat_tail))
        outs[1].append(token_major_a(vat_tail))
        outs[2].append(kbt.reshape(b, N_HEADS_B, 2, HEAD_DIM, s).transpose(0, 4, 1, 2, 3))
        outs[3].append(vb4.reshape(b, s, N_HEADS_B, 2 * HEAD_DIM))

        new = _proj_sample(xs, g1, w_in_l, gains, cos_s, sin_s, pool)
        caches = (cache_a_k[l].transpose(0, 2, 3, 1).reshape(db, WIDTH, lc),
                  cache_a_v[l].transpose(0, 2, 3, 1).reshape(db, WIDTH, lc),
                  cache_b_k[l].transpose(0, 2, 3, 4, 1).reshape(db, WIDTH, past_len),
                  cache_b_v[l].reshape(db, past_len * N_BLOCKS, LANES))
        ya_s, yb_s = _sample_attn(new, caches, bias_f, lam_vecs, subln, lam_init, t)
        xs = _merge_mlp(xs, ya_s, yb_s, db * t, *merge_w)
        outs[4].append(new[1].reshape(db, t, N_HEADS_A, HEAD_DIM))
        outs[5].append(new[2].reshape(db, t, N_HEADS_A, HEAD_DIM))
        outs[6].append(new[4].reshape(db, t, N_HEADS_B, 2, HEAD_DIM))
        outs[7].append(new[5].reshape(db, t, N_HEADS_B, 2 * HEAD_DIM))

    return (xp.reshape(b, s, d), xs.reshape(db, t, d)) + tuple(jnp.stack(o) for o in outs)
```

```python
import functools
import math

import jax
import jax.numpy as jnp
import numpy as np
from jax import lax
from jax.experimental import pallas as pl
from jax.experimental.pallas import tpu as pltpu

D_MODEL = 1024
CHUNK = 64
HEAD_DIM = 64
N_HEADS_A = 8
N_HEADS_B = 4
BAND_PAST_CHUNKS = 8
BAND_PAST = BAND_PAST_CHUNKS * CHUNK
REL_FUT = CHUNK - 1
REL_PAST = 256
N_REL = REL_FUT + REL_PAST + 1
WIDTH = 512
N_GROUPS = 6
D_FF = 4 * D_MODEL
ROPE_THETA = 10000.0
EPS = 1e-6
NEG_INF = -1e30
LOG2E = math.log2(math.e)
Q_SCALE = HEAD_DIM ** -0.5 * LOG2E

LANES = 128
N_BLOCKS = WIDTH // LANES
MXU_DIM = 256
GROUP_CHUNKS = 2
GROUP_Q = GROUP_CHUNKS * CHUNK
GROUP_KEYS = BAND_PAST + GROUP_Q
BIAS_SPAN = GROUP_KEYS + GROUP_Q
Q_BLOCK_B = 256
ROW_TILE = 512
VMEM_LIMIT = 60 * 1024 * 1024

BF16 = jnp.bfloat16
F32 = jnp.float32


def _dot(a, b):
    return jnp.dot(a, b, preferred_element_type=F32)


def _dot_nt(a, b):
    return lax.dot_general(a, b, (((1,), (1,)), ((), ())), preferred_element_type=F32)


def _rms(x, g):
    return x * lax.rsqrt(jnp.mean(x * x, axis=-1, keepdims=True) + EPS) * g


def _const_spec(shape):
    nd = len(shape)
    return pl.BlockSpec(shape, lambda *_: (0,) * nd, pipeline_mode=pl.Buffered(1))


def _params(n_grid_axes):
    return pltpu.CompilerParams(dimension_semantics=("arbitrary",) * n_grid_axes, vmem_limit_bytes=VMEM_LIMIT)


def _lane_halves(q):
    lane = lax.broadcasted_iota(jnp.int32, q.shape, 1)
    zero = jnp.zeros_like(q)
    return jnp.concatenate([jnp.where(lane < HEAD_DIM, q, zero), jnp.where(lane >= HEAD_DIM, q, zero)], axis=0)


def _pick_halves(o):
    m = o.shape[0] // 2
    lane = lax.broadcasted_iota(jnp.int32, (m, LANES), 1)
    return jnp.where(lane < HEAD_DIM, o[:m], o[m:])


def _softmax_parts(parts):
    m = functools.reduce(jnp.maximum, [jnp.max(s, axis=-1, keepdims=True) for s in parts])
    ps = [jnp.exp2(s - m) for s in parts]
    l = functools.reduce(lambda a, b: a + b, [jnp.sum(p, axis=-1, keepdims=True) for p in ps])
    return ps, l


def _diff_combine(ps, l, lam, t):
    c = lam * l[:t] / l[t:]
    return [(p[:t] - p[t:] * c).astype(BF16) for p in ps], 1.0 / l[:t]


def _pair_bias(f_ref, rows):
    def one(hd):
        x = jnp.broadcast_to(f_ref[hd:hd + 1, :], (rows, BIAS_SPAN))
        return pltpu.roll(x, BIAS_SPAN - GROUP_Q + 1, 1, stride=1, stride_axis=0)

    return jnp.concatenate([one(0), one(1)], axis=0) * LOG2E


def _proj_kernel(x_ref, g1_ref, w_ref, gains_ref, cos_ref, sin_ref, pool_ref, *refs, tiles_per_seq):
    h = _rms(x_ref[...], g1_ref[...]).astype(BF16)
    cos = jnp.concatenate([cos_ref[...]] * N_BLOCKS, axis=1)
    sin = jnp.concatenate([sin_ref[...]] * N_BLOCKS, axis=1)
    pool = pool_ref[...]

    def group(i):
        return _dot(h, w_ref[:, i * WIDTH:(i + 1) * WIDTH])

    def head_norm(z, i):
        zz = (z * z).astype(BF16)
        ms = jnp.concatenate([_dot(zz[:, k * MXU_DIM:(k + 1) * MXU_DIM], pool) for k in range(WIDTH // MXU_DIM)],
                             axis=1)
        return z * lax.rsqrt(ms + EPS) * gains_ref[i:i + 1, :]

    def rope(z):
        lane = lax.broadcasted_iota(jnp.int32, z.shape, 1)
        half = HEAD_DIM // 2
        partner = jnp.where(lane % HEAD_DIM < half, pltpu.roll(z, WIDTH - half, 1), pltpu.roll(z, half, 1))
        return z * cos + partner * sin

    if tiles_per_seq == 0:
        qa_ref, ka_ref, va_ref, qb_ref, kb_ref, vb_ref = refs
        qa_ref[...] = (head_norm(group(0), 0) * Q_SCALE).astype(BF16)
        ka_ref[...] = head_norm(group(1), 1)
        va_ref[...] = group(2)
        qb_ref[...] = (rope(head_norm(group(3), 2)) * Q_SCALE).astype(BF16)
        kb_ref[...] = rope(head_norm(group(4), 3))
        vb_ref[...] = group(5)
        return

    kgain_ref, cos_t_ref, sin_t_ref = refs[:3]
    qa_ref, kat_ref, va_ref, qb_ref, kbt_ref, vb4_ref, kat_tail_ref, vat_tail_ref = refs[3:]
    tm = x_ref.shape[0]
    n_heads = WIDTH // HEAD_DIM

    def head_norm_t(zt, i):
        z3 = zt.reshape(n_heads, HEAD_DIM, tm)
        ms = jnp.mean(z3 * z3, axis=1, keepdims=True)
        gain = jnp.concatenate([kgain_ref[i]] * (tm // LANES), axis=1)
        return z3 * lax.rsqrt(ms + EPS) * gain[None]

    def rope_t(z3):
        half = HEAD_DIM // 2
        cos_t = cos_t_ref[...][None]
        sin_t = sin_t_ref[...][None]
        x1, x2 = z3[:, :half], z3[:, half:]
        return jnp.concatenate([x1 * cos_t - x2 * sin_t, x2 * cos_t + x1 * sin_t], axis=1)

    kbt_ref[0] = rope_t(head_norm_t(group(4).T, 1)).reshape(WIDTH, tm)
    kat = head_norm_t(group(1).T, 0).reshape(WIDTH, tm)
    kat_ref[0] = kat.astype(BF16)
    va = group(2)
    va_ref[...] = va.astype(BF16)
    qb_ref[...] = (rope(head_norm(group(3), 2)) * Q_SCALE).astype(BF16)
    qa_ref[...] = (head_norm(group(0), 0) * Q_SCALE).astype(BF16)
    vb = group(5)
    for hd in range(N_BLOCKS):
        vb4_ref[pl.ds(hd, tm, stride=N_BLOCKS), :] = vb[:, hd * LANES:(hd + 1) * LANES]

    @pl.when(pl.program_id(0) % tiles_per_seq == tiles_per_seq - 1)
    def _():
        keep = kat_tail_ref.shape[2]
        kat_tail_ref[0] = kat[:, tm - keep:]
        vat_tail_ref[0] = va[tm - keep:].T


def _proj_specs(tm, n_pos_blocks, consts):
    row = lambda w: pl.BlockSpec((tm, w), lambda i: (i, 0))
    pos = pl.BlockSpec((tm, LANES), lambda i: (i % n_pos_blocks, 0))
    g1, w_in, gains, pool = consts
    return row, [row(D_MODEL), _const_spec(g1.shape), _const_spec(w_in.shape), _const_spec(gains.shape),
                 pos, pos, _const_spec(pool.shape)]


def _proj_sample(x2d, g1, w_in, gains, cos, sin, pool):
    n = x2d.shape[0]
    row, in_specs = _proj_specs(n, 1, (g1, w_in, gains, pool))
    out_dtypes = (BF16, F32, F32, BF16, F32, F32)
    return pl.pallas_call(
        functools.partial(_proj_kernel, tiles_per_seq=0),
        grid=(1,),
        in_specs=in_specs,
        out_specs=[row(WIDTH)] * N_GROUPS,
        out_shape=[jax.ShapeDtypeStruct((n, WIDTH), dt) for dt in out_dtypes],
        compiler_params=_params(1),
        name="proj_sample",
    )(x2d, g1, w_in, gains, cos, sin, pool)


def _proj_prompt(x2d, b, s, keep, g1, w_in, gains, cos, sin, pool, kgain_t, cos_t, sin_t):
    n = x2d.shape[0]
    tm = 2 * ROW_TILE
    tps = s // tm
    row, in_specs = _proj_specs(tm, tps, (g1, w_in, gains, pool))
    pos_t = pl.BlockSpec((HEAD_DIM // 2, tm), lambda i: (0, i % tps))
    in_specs = in_specs + [_const_spec(kgain_t.shape), pos_t, pos_t]
    feat = pl.BlockSpec((1, WIDTH, tm), lambda i: (i // tps, 0, i % tps))
    tail = pl.BlockSpec((1, WIDTH, keep), lambda i: (i // tps, 0, 0))
    out_specs = [row(WIDTH), feat, row(WIDTH), row(WIDTH), feat,
                 pl.BlockSpec((tm * N_BLOCKS, LANES), lambda i: (i, 0)), tail, tail]
    out_shape = [jax.ShapeDtypeStruct((n, WIDTH), BF16),
                 jax.ShapeDtypeStruct((b, WIDTH, s), BF16),
                 jax.ShapeDtypeStruct((n, WIDTH), BF16),
                 jax.ShapeDtypeStruct((n, WIDTH), BF16),
                 jax.ShapeDtypeStruct((b, WIDTH, s), F32),
                 jax.ShapeDtypeStruct((n * N_BLOCKS, LANES), F32),
                 jax.ShapeDtypeStruct((b, WIDTH, keep), F32),
                 jax.ShapeDtypeStruct((b, WIDTH, keep), F32)]
    return pl.pallas_call(
        functools.partial(_proj_kernel, tiles_per_seq=tps),
        grid=(n // tm,),
        in_specs=in_specs,
        out_specs=out_specs,
        out_shape=out_shape,
        compiler_params=_params(1),
        name="proj_prompt",
    )(x2d, g1, w_in, gains, cos, sin, pool, kgain_t, cos_t, sin_t)


def _group_bias(f_ref):
    bias = _pair_bias(f_ref, GROUP_Q)[:, :GROUP_KEYS]
    qc = (lax.broadcasted_iota(jnp.int32, bias.shape, 0) % GROUP_Q) // CHUNK
    kc = lax.broadcasted_iota(jnp.int32, bias.shape, 1) // CHUNK
    return jnp.where((kc >= qc) & (kc <= qc + BAND_PAST_CHUNKS), bias, NEG_INF)


def _band_prompt_kernel(q_ref, kt_ref, v_ref, f_ref, o_ref, bias_ref):
    n_groups = q_ref.shape[1] // GROUP_Q
    bias_ref[...] = _group_bias(f_ref.at[0])

    def window(g):
        return max(0, g * GROUP_Q - BAND_PAST), (g + 1) * GROUP_Q

    def scores(g):
        lo, hi = window(g)
        qm = _lane_halves(q_ref[0, g * GROUP_Q:(g + 1) * GROUP_Q, :])
        return _dot(qm, kt_ref[0, :, lo:hi]) + bias_ref[:, GROUP_KEYS - (hi - lo):]

    nxt = scores(0)
    for g in range(n_groups):
        s = nxt
        if g + 1 < n_groups:
            nxt = scores(g + 1)
        lo, hi = window(g)
        (p,), l = _softmax_parts([s])
        o = _dot(p.astype(BF16), v_ref[0, lo:hi, :]) * (1.0 / l)
        o_ref[0, g * GROUP_Q:(g + 1) * GROUP_Q, :] = _pick_halves(o).astype(o_ref.dtype)


def _band_prompt(qa, kat, va, f):
    b, s, _ = qa.shape
    blk = pl.BlockSpec((1, s, LANES), lambda i, j: (i, 0, j))
    blk_t = pl.BlockSpec((1, LANES, s), lambda i, j: (i, j, 0))
    return pl.pallas_call(
        _band_prompt_kernel,
        grid=(b, N_BLOCKS),
        in_specs=[blk, blk_t, blk, pl.BlockSpec((1, 2, BIAS_SPAN), lambda i, j: (j, 0, 0))],
        out_specs=blk,
        out_shape=jax.ShapeDtypeStruct((b, s, WIDTH), BF16),
        scratch_shapes=[pltpu.VMEM((2 * GROUP_Q, GROUP_KEYS), F32)],
        compiler_params=_params(2),
        name="band_prompt",
    )(qa, kat, va, f)


def _lambda(lam_ref, lam_init):
    e1 = jnp.exp(jnp.sum(lam_ref[0:1, :] * lam_ref[1:2, :], axis=-1, keepdims=True))
    e2 = jnp.exp(jnp.sum(lam_ref[2:3, :] * lam_ref[3:4, :], axis=-1, keepdims=True))
    return e1 - e2 + lam_init


def _diff_out(o, subln, lam_init):
    return _rms(o, subln) * (1.0 - lam_init)


def _diff_prompt_kernel(q_ref, kt_ref, v4_ref, lam_ref, subln_ref, o_ref, kbf_ref, vbf_ref, *, lam_init):
    s_len = q_ref.shape[1]
    tq = Q_BLOCK_B
    kbf_ref[...] = kt_ref[0].astype(BF16)
    vbf_ref[...] = v4_ref[0, pl.ds(pl.program_id(1), s_len, stride=N_BLOCKS), :].astype(BF16)
    lam = _lambda(lam_ref, lam_init)
    row = lax.broadcasted_iota(jnp.int32, (2 * tq, tq), 0) % tq
    col = lax.broadcasted_iota(jnp.int32, (2 * tq, tq), 1)
    diag_mask = jnp.where(col // CHUNK <= row // CHUNK, 0.0, NEG_INF).astype(F32)

    def scores(i):
        r0 = i * tq
        qs = _lane_halves(q_ref[0, r0:r0 + tq, :])
        parts = [_dot(qs, kbf_ref[:, r0:r0 + tq]) + diag_mask]
        if i > 0:
            parts.append(_dot(qs, kbf_ref[:, 0:r0]))
        return parts

    n_blocks = s_len // tq
    nxt = scores(0)
    for i in range(n_blocks):
        r0 = i * tq
        parts = nxt
        if i + 1 < n_blocks:
            nxt = scores(i + 1)
        a, inv = _diff_combine(*_softmax_parts(parts), lam, tq)
        o = _dot(a[0], vbf_ref[r0:r0 + tq, :])
        if i > 0:
            o = o + _dot(a[1], vbf_ref[0:r0, :])
        o_ref[0, r0:r0 + tq, :] = _diff_out(o * inv, subln_ref[...], lam_init).astype(o_ref.dtype)


def _diff_prompt(qb, kbt, vb4, lam_vecs, subln, lam_init):
    b, s, _ = qb.shape
    blk = pl.BlockSpec((1, s, LANES), lambda i, j: (i, 0, j))
    return pl.pallas_call(
        functools.partial(_diff_prompt_kernel, lam_init=lam_init),
        grid=(b, N_BLOCKS),
        in_specs=[blk, pl.BlockSpec((1, LANES, s), lambda i, j: (i, j, 0)),
                  pl.BlockSpec((1, s * N_BLOCKS, LANES), lambda i, j: (i, 0, 0)),
                  _const_spec(lam_vecs.shape), _const_spec(subln.shape)],
        out_specs=blk,
        out_shape=jax.ShapeDtypeStruct((b, s, WIDTH), BF16),
        scratch_shapes=[pltpu.VMEM((LANES, s), BF16), pltpu.VMEM((s, LANES), BF16)],
        compiler_params=_params(2),
        name="diff_prompt",
    )(qb, kbt, vb4, lam_vecs, subln)


def _sample_attn_kernel(qa_ref, ka_ref, va_ref, qb_ref, kb_ref, vb_ref, cakt_ref, cavt_ref, cbkt_ref, cbv4_ref,
                        f_ref, lam_ref, subln_ref, ya_ref, yb_ref, *, lam_init):
    t = qa_ref.shape[0]
    lc = cakt_ref.shape[2]
    past = cbkt_ref.shape[2]
    lam = _lambda(lam_ref, lam_init)
    blocks = [slice(j * LANES, (j + 1) * LANES) for j in range(N_BLOCKS)]
    sa = []
    for j, cols in enumerate(blocks):
        bias = _pair_bias(f_ref.at[j], t)
        qm = _lane_halves(qa_ref[:, cols])
        sa.append([_dot(qm, cakt_ref[0, cols, :].astype(BF16)) + bias[:, :lc],
                   _dot_nt(qm, ka_ref[:, cols].astype(BF16)) + bias[:, lc:lc + t]])
    sb = []
    for cols in blocks:
        qs = _lane_halves(qb_ref[:, cols])
        sb.append([_dot(qs, cbkt_ref[0, cols, :].astype(BF16)), _dot_nt(qs, kb_ref[:, cols].astype(BF16))])
    pa = [_softmax_parts(s) for s in sa]
    pb = [_diff_combine(*_softmax_parts(s), lam, t) for s in sb]
    for cols, ((pc, pn), l) in zip(blocks, pa):
        o = _dot_nt(pc.astype(BF16), cavt_ref[0, cols, :].astype(BF16)) + _dot(pn.astype(BF16),
                                                                             va_ref[:, cols].astype(BF16))
        ya_ref[:, cols] = _pick_halves(o * (1.0 / l)).astype(ya_ref.dtype)
    for j, (cols, ((ac, an), inv)) in enumerate(zip(blocks, pb)):
        vc = cbv4_ref[0, pl.ds(j, past, stride=N_BLOCKS), :].astype(BF16)
        o = (_dot(ac, vc) + _dot(an, vb_ref[:, cols].astype(BF16))) * inv
        yb_ref[:, cols] = _diff_out(o, subln_ref[...], lam_init).astype(yb_ref.dtype)


def _sample_attn(new, caches, f, lam_vecs, subln, lam_init, t):
    n = new[0].shape[0]
    b = n // t
    row = pl.BlockSpec((t, WIDTH), lambda i: (i, 0))
    cache = lambda c: pl.BlockSpec((1,) + c.shape[1:], lambda i: (i, 0, 0))
    return pl.pallas_call(
        functools.partial(_sample_attn_kernel, lam_init=lam_init),
        grid=(b,),
        in_specs=[row] * 6 + [cache(c) for c in caches]
        + [_const_spec(f.shape), _const_spec(lam_vecs.shape), _const_spec(subln.shape)],
        out_specs=[row, row],
        out_shape=[jax.ShapeDtypeStruct((n, WIDTH), BF16)] * 2,
        compiler_params=_params(1),
        name="sample_attn",
    )(*new, *caches, f, lam_vecs, subln)


def _merge_mlp_kernel(x_ref, ya_ref, yb_ref, g1_ref, wg_ref, bg_ref, wpa_ref, wpb_ref, wo_ref,
                      g2_ref, w1_ref, w2_ref, o_ref):
    x = x_ref[...]
    h = _rms(x, g1_ref[...]).astype(BF16)
    m = jax.nn.sigmoid(_dot(h, wg_ref[:, :D_MODEL]) + bg_ref[:, :D_MODEL]) * _dot(ya_ref[...], wpa_ref[...])
    m = m + jax.nn.sigmoid(_dot(h, wg_ref[:, D_MODEL:]) + bg_ref[:, D_MODEL:]) * _dot(yb_ref[...], wpb_ref[...])
    x1 = x + _dot(m.astype(BF16), wo_ref[...])
    hn = _rms(x1, g2_ref[...]).astype(BF16)
    acc = x1
    for c in range(D_FF // D_MODEL):
        cols = slice(c * D_MODEL, (c + 1) * D_MODEL)
        u = jnp.maximum(_dot(hn, w1_ref[:, cols]), 0.0)
        acc = acc + _dot((u * u).astype(BF16), w2_ref[cols, :])
    o_ref[...] = acc


def _merge_mlp(x2d, ya, yb, tm, g1, wg, bg, wpa, wpb, wo, g2, w1, w2):
    n = x2d.shape[0]
    row = lambda w: pl.BlockSpec((tm, w), lambda i: (i, 0))
    consts = (g1, wg, bg, wpa, wpb, wo, g2, w1, w2)
    return pl.pallas_call(
        _merge_mlp_kernel,
        grid=(n // tm,),
        in_specs=[row(D_MODEL), row(WIDTH), row(WIDTH)] + [_const_spec(c.shape) for c in consts],
        out_specs=row(D_MODEL),
        out_shape=jax.ShapeDtypeStruct((n, D_MODEL), F32),
        compiler_params=_params(1),
        name="merge_mlp",
    )(x2d, ya, yb, *consts)


def _rope_tables(pos):
    half = HEAD_DIM // 2
    inv_freq = ROPE_THETA ** (-jnp.arange(half, dtype=F32) / half)
    ang = pos.astype(F32)[:, None] * inv_freq[None, :]
    cos = jnp.cos(ang)
    sin = jnp.sin(ang)
    reps = LANES // HEAD_DIM
    return (jnp.tile(jnp.concatenate([cos, cos], axis=1), (1, reps)),
            jnp.tile(jnp.concatenate([-sin, sin], axis=1), (1, reps)), cos.T, sin.T)


def _band_offset_table(table):
    n_far = BAND_PAST + GROUP_Q - 1 - REL_PAST + 1
    n_fut = BIAS_SPAN - n_far - (N_REL - 1)
    h = table.shape[0]
    f = jnp.concatenate([jnp.broadcast_to(table[:, N_REL - 1:], (h, n_far)), table[:, N_REL - 2::-1],
                         jnp.broadcast_to(table[:, :1], (h, n_fut))], axis=1).astype(F32)
    return f.reshape(h // 2, 2, BIAS_SPAN)


def _pool_matrix():
    i = np.arange(MXU_DIM)
    return jnp.asarray((i[:, None] // HEAD_DIM == i[None, :] // HEAD_DIM) / HEAD_DIM, dtype=BF16)


def kernel(x_prompt, x_sample, cache_a_k, cache_a_v, cache_b_k, cache_b_v, ln1_g, w_in, qn_a, kn_a, rel_bias, qn_b, kn_b, lam_q1, lam_k1, lam_q2, lam_k2, subln_g, w_gate, b_gate, w_proj_a, w_proj_b, w_out, ln2_g, w_ff1, w_ff2):
    depth = w_in.shape[0]
    b, s, d = x_prompt.shape
    db, t, _ = x_sample.shape
    past_len = cache_b_k.shape[2]
    lc = cache_a_k.shape[2]
    keep = min(BAND_PAST, s)
    assert d == D_MODEL and s % (2 * ROW_TILE) == 0 and keep <= 2 * ROW_TILE and keep % LANES == 0
    assert lc == BAND_PAST and t <= GROUP_Q

    cos_p, sin_p, cos_pt, sin_pt = _rope_tables(jnp.arange(s))
    cos_s, sin_s = (jnp.tile(a, (db, 1)) for a in _rope_tables(past_len + jnp.arange(t))[:2])
    lane_const = lambda g: jnp.broadcast_to(g[:, None], (HEAD_DIM, LANES))
    pool = _pool_matrix()
    tile_heads = lambda g: jnp.tile(g, WIDTH // HEAD_DIM)

    xp = x_prompt.reshape(b * s, d)
    xs = x_sample.reshape(db * t, d)
    outs = [[] for _ in range(8)]
    for l in range(depth):
        lam_init = 0.8 - 0.6 * math.exp(-0.3 * l)
        g1 = ln1_g[l][None]
        g2 = ln2_g[l][None]
        gains = jnp.stack([tile_heads(qn_a[l]), tile_heads(kn_a[l]), tile_heads(qn_b[l]), tile_heads(kn_b[l])])
        lam_vecs = jnp.stack([lam_q1[l], lam_k1[l], lam_q2[l], lam_k2[l]])
        subln = subln_g[l][None]
        bias_f = _band_offset_table(rel_bias[l])
        w_in_l = w_in[l].astype(BF16)
        merge_w = (g1, w_gate[l].astype(BF16), b_gate[l][None], w_proj_a[l].astype(BF16), w_proj_b[l].astype(BF16),
                   w_out[l].astype(BF16), g2, w_ff1[l].astype(BF16), w_ff2[l].astype(BF16))

        kgain_t = jnp.stack([lane_const(kn_a[l]), lane_const(kn_b[l])])
        qa, kat, va, qb, kbt, vb4, kat_tail, vat_tail = _proj_prompt(xp, b, s, keep, g1, w_in_l, gains, cos_p, sin_p,
                                                                     pool, kgain_t, cos_pt, sin_pt)
        as3 = lambda a: a.reshape(b, s, WIDTH)
        ya = _band_prompt(as3(qa), kat, as3(va), bias_f)
        yb = _diff_prompt(as3(qb), kbt, vb4.reshape(b, s * N_BLOCKS, LANES), lam_vecs, subln, lam_init)
        xp = _merge_mlp(xp, ya.reshape(b * s, WIDTH), yb.reshape(b * s, WIDTH), 2 * ROW_TILE, *merge_w)
        token_major_a = lambda a: a.reshape(b, N_HEADS_A, HEAD_DIM, keep).transpose(0, 3, 1, 2)
        outs[0].append(token_major_a(kat_tail))
        outs[1].append(token_major_a(vat_tail))
        outs[2].append(kbt.reshape(b, N_HEADS_B, 2, HEAD_DIM, s).transpose(0, 4, 1, 2, 3))
        outs[3].append(vb4.reshape(b, s, N_HEADS_B, 2 * HEAD_DIM))

        new = _proj_sample(xs, g1, w_in_l, gains, cos_s, sin_s, pool)
        caches = (cache_a_k[l].transpose(0, 2, 3, 1).reshape(db, WIDTH, lc),
                  cache_a_v[l].transpose(0, 2, 3, 1).reshape(db, WIDTH, lc),
                  cache_b_k[l].transpose(0, 2, 3, 4, 1).reshape(db, WIDTH, past_len),
                  cache_b_v[l].reshape(db, past_len * N_BLOCKS, LANES))
        ya_s, yb_s = _sample_attn(new, caches, bias_f, lam_vecs, subln, lam_init, t)
        xs = _merge_mlp(xs, ya_s, yb_s, db * t, *merge_w)
        outs[4].append(new[1].reshape(db, t, N_HEADS_A, HEAD_DIM))
        outs[5].append(new[2].reshape(db, t, N_HEADS_A, HEAD_DIM))
        outs[6].append(new[4].reshape(db, t, N_HEADS_B, 2, HEAD_DIM))
        outs[7].append(new[5].reshape(db, t, N_HEADS_B, 2 * HEAD_DIM))

    return (xp.reshape(b, s, d), xs.reshape(db, t, d)) + tuple(jnp.stack(o) for o in outs)
```

```python
import functools
import math

import jax
import jax.numpy as jnp
import numpy as np
from jax import lax
from jax.experimental import pallas as pl
from jax.experimental.pallas import tpu as pltpu

D_MODEL = 1024
CHUNK = 64
HEAD_DIM = 64
N_HEADS_A = 8
N_HEADS_B = 4
BAND_PAST_CHUNKS = 8
BAND_PAST = BAND_PAST_CHUNKS * CHUNK
REL_FUT = CHUNK - 1
REL_PAST = 256
N_REL = REL_FUT + REL_PAST + 1
WIDTH = 512
N_GROUPS = 6
D_FF = 4 * D_MODEL
ROPE_THETA = 10000.0
EPS = 1e-6
NEG_INF = -1e30
LOG2E = math.log2(math.e)
Q_SCALE = HEAD_DIM ** -0.5 * LOG2E

LANES = 128
N_BLOCKS = WIDTH // LANES
MXU_DIM = 256
GROUP_CHUNKS = 2
GROUP_Q = GROUP_CHUNKS * CHUNK
GROUP_KEYS = BAND_PAST + GROUP_Q
BIAS_SPAN = GROUP_KEYS + GROUP_Q
Q_BLOCK_B = 256
ROW_TILE = 512
VMEM_LIMIT = 60 * 1024 * 1024

BF16 = jnp.bfloat16
F32 = jnp.float32


def _dot(a, b):
    return jnp.dot(a, b, preferred_element_type=F32)


def _dot_nt(a, b):
    return lax.dot_general(a, b, (((1,), (1,)), ((), ())), preferred_element_type=F32)


def _rms(x, g):
    return x * lax.rsqrt(jnp.mean(x * x, axis=-1, keepdims=True) + EPS) * g


def _const_spec(shape):
    nd = len(shape)
    return pl.BlockSpec(shape, lambda *_: (0,) * nd, pipeline_mode=pl.Buffered(1))


def _params(n_grid_axes):
    return pltpu.CompilerParams(dimension_semantics=("arbitrary",) * n_grid_axes, vmem_limit_bytes=VMEM_LIMIT)


def _lane_halves(q):
    lane = lax.broadcasted_iota(jnp.int32, q.shape, 1)
    zero = jnp.zeros_like(q)
    return jnp.concatenate([jnp.where(lane < HEAD_DIM, q, zero), jnp.where(lane >= HEAD_DIM, q, zero)], axis=0)


def _pick_halves(o):
    m = o.shape[0] // 2
    lane = lax.broadcasted_iota(jnp.int32, (m, LANES), 1)
    return jnp.where(lane < HEAD_DIM, o[:m], o[m:])


def _softmax_parts(parts):
    m = functools.reduce(jnp.maximum, [jnp.max(s, axis=-1, keepdims=True) for s in parts])
    ps = [jnp.exp2(s - m) for s in parts]
    l = functools.reduce(lambda a, b: a + b, [jnp.sum(p, axis=-1, keepdims=True) for p in ps])
    return ps, l


def _diff_combine(ps, l, lam, t):
    c = lam * l[:t] / l[t:]
    return [(p[:t] - p[t:] * c).astype(BF16) for p in ps], 1.0 / l[:t]


def _pair_bias(f_ref, rows):
    def one(hd):
        x = jnp.broadcast_to(f_ref[hd:hd + 1, :], (rows, BIAS_SPAN))
        return pltpu.roll(x, BIAS_SPAN - GROUP_Q + 1, 1, stride=1, stride_axis=0)

    return jnp.concatenate([one(0), one(1)], axis=0) * LOG2E


def _proj_kernel(x_ref, g1_ref, w_ref, gains_ref, cos_ref, sin_ref, pool_ref, *refs, tiles_per_seq):
    h = _rms(x_ref[...], g1_ref[...]).astype(BF16)
    cos = jnp.concatenate([cos_ref[...]] * N_BLOCKS, axis=1)
    sin = jnp.concatenate([sin_ref[...]] * N_BLOCKS, axis=1)
    pool = pool_ref[...]

    def group(i):
        return _dot(h, w_ref[:, i * WIDTH:(i + 1) * WIDTH])

    def head_norm(z, i):
        zz = (z * z).astype(BF16)
        ms = jnp.concatenate([_dot(zz[:, k * MXU_DIM:(k + 1) * MXU_DIM], pool) for k in range(WIDTH // MXU_DIM)],
                             axis=1)
        return z * lax.rsqrt(ms + EPS) * gains_ref[i:i + 1, :]

    def rope(z):
        lane = lax.broadcasted_iota(jnp.int32, z.shape, 1)
        half = HEAD_DIM // 2
        partner = jnp.where(lane % HEAD_DIM < half, pltpu.roll(z, WIDTH - half, 1), pltpu.roll(z, half, 1))
        return z * cos + partner * sin

    if tiles_per_seq == 0:
        qa_ref, ka_ref, va_ref, qb_ref, kb_ref, vb_ref = refs
        qa_ref[...] = (head_norm(group(0), 0) * Q_SCALE).astype(BF16)
        ka_ref[...] = head_norm(group(1), 1)
        va_ref[...] = group(2)
        qb_ref[...] = (rope(head_norm(group(3), 2)) * Q_SCALE).astype(BF16)
        kb_ref[...] = rope(head_norm(group(4), 3))
        vb_ref[...] = group(5)
        return

    kgain_ref, cos_t_ref, sin_t_ref = refs[:3]
    qa_ref, kat_ref, va_ref, qb_ref, kbt_ref, vb4_ref, kat_tail_ref, vat_tail_ref = refs[3:]
    tm = x_ref.shape[0]
    n_heads = WIDTH // HEAD_DIM

    def head_norm_t(zt, i):
        z3 = zt.reshape(n_heads, HEAD_DIM, tm)
        ms = jnp.mean(z3 * z3, axis=1, keepdims=True)
        gain = jnp.concatenate([kgain_ref[i]] * (tm // LANES), axis=1)
        return z3 * lax.rsqrt(ms + EPS) * gain[None]

    def rope_t(z3):
        half = HEAD_DIM // 2
        cos_t = cos_t_ref[...][None]
        sin_t = sin_t_ref[...][None]
        x1, x2 = z3[:, :half], z3[:, half:]
        return jnp.concatenate([x1 * cos_t - x2 * sin_t, x2 * cos_t + x1 * sin_t], axis=1)

    kbt_ref[0] = rope_t(head_norm_t(group(4).T, 1)).reshape(WIDTH, tm)
    kat = head_norm_t(group(1).T, 0).reshape(WIDTH, tm)
    kat_ref[0] = kat.astype(BF16)
    va = group(2)
    va_ref[...] = va.astype(BF16)
    qb_ref[...] = (rope(head_norm(group(3), 2)) * Q_SCALE).astype(BF16)
    qa_ref[...] = (head_norm(group(0), 0) * Q_SCALE).astype(BF16)
    vb = group(5)
    for hd in range(N_BLOCKS):
        vb4_ref[pl.ds(hd, tm, stride=N_BLOCKS), :] = vb[:, hd * LANES:(hd + 1) * LANES]

    @pl.when(pl.program_id(0) % tiles_per_seq == tiles_per_seq - 1)
    def _():
        keep = kat_tail_ref.shape[2]
        kat_tail_ref[0] = kat[:, tm - keep:]
        vat_tail_ref[0] = va[tm - keep:].T


def _proj_specs(tm, n_pos_blocks, consts):
    row = lambda w: pl.BlockSpec((tm, w), lambda i: (i, 0))
    pos = pl.BlockSpec((tm, LANES), lambda i: (i % n_pos_blocks, 0))
    g1, w_in, gains, pool = consts
    return row, [row(D_MODEL), _const_spec(g1.shape), _const_spec(w_in.shape), _const_spec(gains.shape),
                 pos, pos, _const_spec(pool.shape)]


def _proj_sample(x2d, g1, w_in, gains, cos, sin, pool):
    n = x2d.shape[0]
    row, in_specs = _proj_specs(n, 1, (g1, w_in, gains, pool))
    out_dtypes = (BF16, F32, F32, BF16, F32, F32)
    return pl.pallas_call(
        functools.partial(_proj_kernel, tiles_per_seq=0),
        grid=(1,),
        in_specs=in_specs,
        out_specs=[row(WIDTH)] * N_GROUPS,
        out_shape=[jax.ShapeDtypeStruct((n, WIDTH), dt) for dt in out_dtypes],
        compiler_params=_params(1),
        name="proj_sample",
    )(x2d, g1, w_in, gains, cos, sin, pool)


def _proj_prompt(x2d, b, s, keep, g1, w_in, gains, cos, sin, pool, kgain_t, cos_t, sin_t):
    n = x2d.shape[0]
    tm = 2 * ROW_TILE
    tps = s // tm
    row, in_specs = _proj_specs(tm, tps, (g1, w_in, gains, pool))
    pos_t = pl.BlockSpec((HEAD_DIM // 2, tm), lambda i: (0, i % tps))
    in_specs = in_specs + [_const_spec(kgain_t.shape), pos_t, pos_t]
    feat = pl.BlockSpec((1, WIDTH, tm), lambda i: (i // tps, 0, i % tps))
    tail = pl.BlockSpec((1, WIDTH, keep), lambda i: (i // tps, 0, 0))
    out_specs = [row(WIDTH), feat, row(WIDTH), row(WIDTH), feat,
                 pl.BlockSpec((tm * N_BLOCKS, LANES), lambda i: (i, 0)), tail, tail]
    out_shape = [jax.ShapeDtypeStruct((n, WIDTH), BF16),
                 jax.ShapeDtypeStruct((b, WIDTH, s), BF16),
                 jax.ShapeDtypeStruct((n, WIDTH), BF16),
                 jax.ShapeDtypeStruct((n, WIDTH), BF16),
                 jax.ShapeDtypeStruct((b, WIDTH, s), F32),
                 jax.ShapeDtypeStruct((n * N_BLOCKS, LANES), F32),
                 jax.ShapeDtypeStruct((b, WIDTH, keep), F32),
                 jax.ShapeDtypeStruct((b, WIDTH, keep), F32)]
    return pl.pallas_call(
        functools.partial(_proj_kernel, tiles_per_seq=tps),
        grid=(n // tm,),
        in_specs=in_specs,
        out_specs=out_specs,
        out_shape=out_shape,
        compiler_params=_params(1),
        name="proj_prompt",
    )(x2d, g1, w_in, gains, cos, sin, pool, kgain_t, cos_t, sin_t)


def _group_bias(f_ref):
    bias = _pair_bias(f_ref, GROUP_Q)[:, :GROUP_KEYS]
    qc = (lax.broadcasted_iota(jnp.int32, bias.shape, 0) % GROUP_Q) // CHUNK
    kc = lax.broadcasted_iota(jnp.int32, bias.shape, 1) // CHUNK
    return jnp.where((kc >= qc) & (kc <= qc + BAND_PAST_CHUNKS), bias, NEG_INF)


def _band_prompt_kernel(q_ref, kt_ref, v_ref, f_ref, o_ref, bias_ref):
    n_groups = q_ref.shape[1] // GROUP_Q

    def window(g):
        return max(0, g * GROUP_Q - BAND_PAST), (g + 1) * GROUP_Q

    def head_pair(j, carry):
        cols = pl.ds(pl.multiple_of(j * LANES, LANES), LANES)
        bias_ref[...] = _group_bias(f_ref.at[j])

        def scores(g):
            lo, hi = window(g)
            qm = _lane_halves(q_ref[0, g * GROUP_Q:(g + 1) * GROUP_Q, cols])
            return _dot(qm, kt_ref[0, cols, lo:hi]) + bias_ref[:, GROUP_KEYS - (hi - lo):]

        nxt = scores(0)
        for g in range(n_groups):
            s = nxt
            if g + 1 < n_groups:
                nxt = scores(g + 1)
            lo, hi = window(g)
            (p,), l = _softmax_parts([s])
            o = _dot(p.astype(BF16), v_ref[0, lo:hi, cols]) * (1.0 / l)
            o_ref[0, g * GROUP_Q:(g + 1) * GROUP_Q, cols] = _pick_halves(o).astype(o_ref.dtype)
        return carry

    lax.fori_loop(0, N_BLOCKS, head_pair, 0)


def _band_prompt(qa, kat, va, f):
    b, s, _ = qa.shape
    blk = pl.BlockSpec((1, s, WIDTH), lambda i: (i, 0, 0))
    return pl.pallas_call(
        _band_prompt_kernel,
        grid=(b,),
        in_specs=[blk, pl.BlockSpec((1, WIDTH, s), lambda i: (i, 0, 0)), blk, _const_spec(f.shape)],
        out_specs=blk,
        out_shape=jax.ShapeDtypeStruct((b, s, WIDTH), BF16),
        scratch_shapes=[pltpu.VMEM((2 * GROUP_Q, GROUP_KEYS), F32)],
        compiler_params=_params(1),
        name="band_prompt",
    )(qa, kat, va, f)


def _lambda(lam_ref, lam_init):
    e1 = jnp.exp(jnp.sum(lam_ref[0:1, :] * lam_ref[1:2, :], axis=-1, keepdims=True))
    e2 = jnp.exp(jnp.sum(lam_ref[2:3, :] * lam_ref[3:4, :], axis=-1, keepdims=True))
    return e1 - e2 + lam_init


def _diff_out(o, subln, lam_init):
    return _rms(o, subln) * (1.0 - lam_init)


def _diff_prompt_kernel(q_ref, kt_ref, v4_ref, lam_ref, subln_ref, o_ref, kbf_ref, vbf_ref, *, lam_init):
    s_len = q_ref.shape[1]
    tq = Q_BLOCK_B
    n_blocks = s_len // tq
    lam = _lambda(lam_ref, lam_init)

    def head(j, carry):
        cols = pl.ds(pl.multiple_of(j * LANES, LANES), LANES)
        kbf_ref[...] = kt_ref[0, cols, :].astype(BF16)
        vbf_ref[...] = v4_ref[0, pl.ds(j, s_len, stride=N_BLOCKS), :].astype(BF16)
        row = lax.broadcasted_iota(jnp.int32, (2 * tq, tq), 0) % tq
        col = lax.broadcasted_iota(jnp.int32, (2 * tq, tq), 1)
        diag_mask = jnp.where(col // CHUNK <= row // CHUNK, 0.0, NEG_INF).astype(F32)

        def scores(i):
            r0 = i * tq
            qs = _lane_halves(q_ref[0, r0:r0 + tq, cols])
            parts = [_dot(qs, kbf_ref[:, r0:r0 + tq]) + diag_mask]
            if i > 0:
                parts.append(_dot(qs, kbf_ref[:, 0:r0]))
            return parts

        nxt = scores(0)
        for i in range(n_blocks):
            r0 = i * tq
            parts = nxt
            if i + 1 < n_blocks:
                nxt = scores(i + 1)
            a, inv = _diff_combine(*_softmax_parts(parts), lam, tq)
            o = _dot(a[0], vbf_ref[r0:r0 + tq, :])
            if i > 0:
                o = o + _dot(a[1], vbf_ref[0:r0, :])
            o_ref[0, r0:r0 + tq, cols] = _diff_out(o * inv, subln_ref[...], lam_init).astype(o_ref.dtype)
        return carry

    lax.fori_loop(0, N_BLOCKS, head, 0)


def _diff_prompt(qb, kbt, vb4, lam_vecs, subln, lam_init):
    b, s, _ = qb.shape
    blk = pl.BlockSpec((1, s, WIDTH), lambda i: (i, 0, 0))
    return pl.pallas_call(
        functools.partial(_diff_prompt_kernel, lam_init=lam_init),
        grid=(b,),
        in_specs=[blk, pl.BlockSpec((1, WIDTH, s), lambda i: (i, 0, 0)),
                  pl.BlockSpec((1, s * N_BLOCKS, LANES), lambda i: (i, 0, 0)),
                  _const_spec(lam_vecs.shape), _const_spec(subln.shape)],
        out_specs=blk,
        out_shape=jax.ShapeDtypeStruct((b, s, WIDTH), BF16),
        scratch_shapes=[pltpu.VMEM((LANES, s), BF16), pltpu.VMEM((s, LANES), BF16)],
        compiler_params=_params(1),
        name="diff_prompt",
    )(qb, kbt, vb4, lam_vecs, subln)


def _sample_attn_kernel(qa_ref, ka_ref, va_ref, qb_ref, kb_ref, vb_ref, cakt_ref, cavt_ref, cbkt_ref, cbv4_ref,
                        f_ref, lam_ref, subln_ref, ya_ref, yb_ref, *, lam_init):
    t = qa_ref.shape[0]
    lc = cakt_ref.shape[2]
    past = cbkt_ref.shape[2]
    lam = _lambda(lam_ref, lam_init)
    blocks = [slice(j * LANES, (j + 1) * LANES) for j in range(N_BLOCKS)]
    sa = []
    for j, cols in enumerate(blocks):
        bias = _pair_bias(f_ref.at[j], t)
        qm = _lane_halves(qa_ref[:, cols])
        sa.append([_dot(qm, cakt_ref[0, cols, :].astype(BF16)) + bias[:, :lc],
                   _dot_nt(qm, ka_ref[:, cols].astype(BF16)) + bias[:, lc:lc + t]])
    sb = []
    for cols in blocks:
        qs = _lane_halves(qb_ref[:, cols])
        sb.append([_dot(qs, cbkt_ref[0, cols, :].astype(BF16)), _dot_nt(qs, kb_ref[:, cols].astype(BF16))])
    pa = [_softmax_parts(s) for s in sa]
    pb = [_diff_combine(*_softmax_parts(s), lam, t) for s in sb]
    for cols, ((pc, pn), l) in zip(blocks, pa):
        o = _dot_nt(pc.astype(BF16), cavt_ref[0, cols, :].astype(BF16)) + _dot(pn.astype(BF16),
                                                                             va_ref[:, cols].astype(BF16))
        ya_ref[:, cols] = _pick_halves(o * (1.0 / l)).astype(ya_ref.dtype)
    for j, (cols, ((ac, an), inv)) in enumerate(zip(blocks, pb)):
        vc = cbv4_ref[0, pl.ds(j, past, stride=N_BLOCKS), :].astype(BF16)
        o = (_dot(ac, vc) + _dot(an, vb_ref[:, cols].astype(BF16))) * inv
        yb_ref[:, cols] = _diff_out(o, subln_ref[...], lam_init).astype(yb_ref.dtype)


def _sample_attn(new, caches, f, lam_vecs, subln, lam_init, t):
    n = new[0].shape[0]
    b = n // t
    row = pl.BlockSpec((t, WIDTH), lambda i: (i, 0))
    cache = lambda c: pl.BlockSpec((1,) + c.shape[1:], lambda i: (i, 0, 0))
    return pl.pallas_call(
        functools.partial(_sample_attn_kernel, lam_init=lam_init),
        grid=(b,),
        in_specs=[row] * 6 + [cache(c) for c in caches]
        + [_const_spec(f.shape), _const_spec(lam_vecs.shape), _const_spec(subln.shape)],
        out_specs=[row, row],
        out_shape=[jax.ShapeDtypeStruct((n, WIDTH), BF16)] * 2,
        compiler_params=_params(1),
        name="sample_attn",
    )(*new, *caches, f, lam_vecs, subln)


def _merge_mlp_kernel(x_ref, ya_ref, yb_ref, g1_ref, wg_ref, bg_ref, wpa_ref, wpb_ref, wo_ref,
                      g2_ref, w1_ref, w2_ref, o_ref):
    x = x_ref[...]
    h = _rms(x, g1_ref[...]).astype(BF16)
    m = jax.nn.sigmoid(_dot(h, wg_ref[:, :D_MODEL]) + bg_ref[:, :D_MODEL]) * _dot(ya_ref[...], wpa_ref[...])
    m = m + jax.nn.sigmoid(_dot(h, wg_ref[:, D_MODEL:]) + bg_ref[:, D_MODEL:]) * _dot(yb_ref[...], wpb_ref[...])
    x1 = x + _dot(m.astype(BF16), wo_ref[...])
    hn = _rms(x1, g2_ref[...]).astype(BF16)
    acc = x1
    for c in range(D_FF // D_MODEL):
        cols = slice(c * D_MODEL, (c + 1) * D_MODEL)
        u = jnp.maximum(_dot(hn, w1_ref[:, cols]), 0.0)
        acc = acc + _dot((u * u).astype(BF16), w2_ref[cols, :])
    o_ref[...] = acc


def _merge_mlp(x2d, ya, yb, tm, g1, wg, bg, wpa, wpb, wo, g2, w1, w2):
    n = x2d.shape[0]
    row = lambda w: pl.BlockSpec((tm, w), lambda i: (i, 0))
    consts = (g1, wg, bg, wpa, wpb, wo, g2, w1, w2)
    return pl.pallas_call(
        _merge_mlp_kernel,
        grid=(n // tm,),
        in_specs=[row(D_MODEL), row(WIDTH), row(WIDTH)] + [_const_spec(c.shape) for c in consts],
        out_specs=row(D_MODEL),
        out_shape=jax.ShapeDtypeStruct((n, D_MODEL), F32),
        compiler_params=_params(1),
        name="merge_mlp",
    )(x2d, ya, yb, *consts)


def _rope_tables(pos):
    half = HEAD_DIM // 2
    inv_freq = ROPE_THETA ** (-jnp.arange(half, dtype=F32) / half)
    ang = pos.astype(F32)[:, None] * inv_freq[None, :]
    cos = jnp.cos(ang)
    sin = jnp.sin(ang)
    reps = LANES // HEAD_DIM
    return (jnp.tile(jnp.concatenate([cos, cos], axis=1), (1, reps)),
            jnp.tile(jnp.concatenate([-sin, sin], axis=1), (1, reps)), cos.T, sin.T)


def _band_offset_table(table):
    n_far = BAND_PAST + GROUP_Q - 1 - REL_PAST + 1
    n_fut = BIAS_SPAN - n_far - (N_REL - 1)
    h = table.shape[0]
    f = jnp.concatenate([jnp.broadcast_to(table[:, N_REL - 1:], (h, n_far)), table[:, N_REL - 2::-1],
                         jnp.broadcast_to(table[:, :1], (h, n_fut))], axis=1).astype(F32)
    return f.reshape(h // 2, 2, BIAS_SPAN)


def _pool_matrix():
    i = np.arange(MXU_DIM)
    return jnp.asarray((i[:, None] // HEAD_DIM == i[None, :] // HEAD_DIM) / HEAD_DIM, dtype=BF16)


def kernel(x_prompt, x_sample, cache_a_k, cache_a_v, cache_b_k, cache_b_v, ln1_g, w_in, qn_a, kn_a, rel_bias, qn_b, kn_b, lam_q1, lam_k1, lam_q2, lam_k2, subln_g, w_gate, b_gate, w_proj_a, w_proj_b, w_out, ln2_g, w_ff1, w_ff2):
    depth = w_in.shape[0]
    b, s, d = x_prompt.shape
    db, t, _ = x_sample.shape
    past_len = cache_b_k.shape[2]
    lc = cache_a_k.shape[2]
    keep = min(BAND_PAST, s)
    assert d == D_MODEL and s % (2 * ROW_TILE) == 0 and keep <= 2 * ROW_TILE and keep % LANES == 0
    assert lc == BAND_PAST and t <= GROUP_Q

    cos_p, sin_p, cos_pt, sin_pt = _rope_tables(jnp.arange(s))
    cos_s, sin_s = (jnp.tile(a, (db, 1)) for a in _rope_tables(past_len + jnp.arange(t))[:2])
    lane_const = lambda g: jnp.broadcast_to(g[:, None], (HEAD_DIM, LANES))
    pool = _pool_matrix()
    tile_heads = lambda g: jnp.tile(g, WIDTH // HEAD_DIM)

    xp = x_prompt.reshape(b * s, d)
    xs = x_sample.reshape(db * t, d)
    outs = [[] for _ in range(8)]
    for l in range(depth):
        lam_init = 0.8 - 0.6 * math.exp(-0.3 * l)
        g1 = ln1_g[l][None]
        g2 = ln2_g[l][None]
        gains = jnp.stack([tile_heads(qn_a[l]), tile_heads(kn_a[l]), tile_heads(qn_b[l]), tile_heads(kn_b[l])])
        lam_vecs = jnp.stack([lam_q1[l], lam_k1[l], lam_q2[l], lam_k2[l]])
        subln = subln_g[l][None]
        bias_f = _band_offset_table(rel_bias[l])
        w_in_l = w_in[l].astype(BF16)
        merge_w = (g1, w_gate[l].astype(BF16), b_gate[l][None], w_proj_a[l].astype(BF16), w_proj_b[l].astype(BF16),
                   w_out[l].astype(BF16), g2, w_ff1[l].astype(BF16), w_ff2[l].astype(BF16))

        kgain_t = jnp.stack([lane_const(kn_a[l]), lane_const(kn_b[l])])
        qa, kat, va, qb, kbt, vb4, kat_tail, vat_tail = _proj_prompt(xp, b, s, keep, g1, w_in_l, gains, cos_p, sin_p,
                                                                     pool, kgain_t, cos_pt, sin_pt)
        as3 = lambda a: a.reshape(b, s, WIDTH)
        ya = _band_prompt(as3(qa), kat, as3(va), bias_f)
        yb = _diff_prompt(as3(qb), kbt, vb4.reshape(b, s * N_BLOCKS, LANES), lam_vecs, subln, lam_init)
        xp = _merge_mlp(xp, ya.reshape(b * s, WIDTH), yb.reshape(b * s, WIDTH), 2 * ROW_TILE, *merge_w)
        token_major_a = lambda a: a.reshape(b, N_HEADS_A, HEAD_DIM, keep).transpose(0, 3, 1, 2)
        outs[0].append(token_major_a(kat_tail))
        outs[1].append(token_major_a(vat_tail))
        outs[2].append(kbt.reshape(b, N_HEADS_B, 2, HEAD_DIM, s).transpose(0, 4, 1, 2, 3))
        outs[3].append(vb4.reshape(b, s, N_HEADS_B, 2 * HEAD_DIM))

        new = _proj_sample(xs, g1, w_in_l, gains, cos_s, sin_s, pool)
        caches = (cache_a_k[l].transpose(0, 2, 3, 1).reshape(db, WIDTH, lc),
                  cache_a_v[l].transpose(0, 2, 3, 1).reshape(db, WIDTH, lc),
                  cache_b_k[l].transpose(0, 2, 3, 4, 1).reshape(db, WIDTH, past_len),
                  cache_b_v[l].reshape(db, past_len * N_BLOCKS, LANES))
        ya_s, yb_s = _sample_attn(new, caches, bias_f, lam_vecs, subln, lam_init, t)
        xs = _merge_mlp(xs, ya_s, yb_s, db * t, *merge_w)
        outs[4].append(new[1].reshape(db, t, N_HEADS_A, HEAD_DIM))
        outs[5].append(new[2].reshape(db, t, N_HEADS_A, HEAD_DIM))
        outs[6].append(new[4].reshape(db, t, N_HEADS_B, 2, HEAD_DIM))
        outs[7].append(new[5].reshape(db, t, N_HEADS_B, 2 * HEAD_DIM))

    return (xp.reshape(b, s, d), xs.reshape(db, t, d)) + tuple(jnp.stack(o) for o in outs)
```

```python
import functools
import math

import jax
import jax.numpy as jnp
import numpy as np
from jax import lax
from jax.experimental import pallas as pl
from jax.experimental.pallas import tpu as pltpu

D_MODEL = 1024
CHUNK = 64
HEAD_DIM = 64
N_HEADS_A = 8
N_HEADS_B = 4
BAND_PAST_CHUNKS = 8
BAND_PAST = BAND_PAST_CHUNKS * CHUNK
REL_FUT = CHUNK - 1
REL_PAST = 256
N_REL = REL_FUT + REL_PAST + 1
WIDTH = 512
N_GROUPS = 6
D_FF = 4 * D_MODEL
ROPE_THETA = 10000.0
EPS = 1e-6
NEG_INF = -1e30
LOG2E = math.log2(math.e)
Q_SCALE = HEAD_DIM ** -0.5 * LOG2E

LANES = 128
N_BLOCKS = WIDTH // LANES
MXU_DIM = 256
GROUP_CHUNKS = 2
GROUP_Q = GROUP_CHUNKS * CHUNK
GROUP_KEYS = BAND_PAST + GROUP_Q
BIAS_SPAN = GROUP_KEYS + GROUP_Q
Q_BLOCK_B = 256
ROW_TILE = 512
VMEM_LIMIT = 60 * 1024 * 1024

BF16 = jnp.bfloat16
F32 = jnp.float32


def _dot(a, b):
    return jnp.dot(a, b, preferred_element_type=F32)


def _dot_nt(a, b):
    return lax.dot_general(a, b, (((1,), (1,)), ((), ())), preferred_element_type=F32)


def _rms(x, g):
    return x * lax.rsqrt(jnp.mean(x * x, axis=-1, keepdims=True) + EPS) * g


def _const_spec(shape):
    nd = len(shape)
    return pl.BlockSpec(shape, lambda *_: (0,) * nd, pipeline_mode=pl.Buffered(1))


def _params(n_grid_axes):
    return pltpu.CompilerParams(dimension_semantics=("arbitrary",) * n_grid_axes, vmem_limit_bytes=VMEM_LIMIT)


def _lane_halves(q):
    lane = lax.broadcasted_iota(jnp.int32, q.shape, 1)
    zero = jnp.zeros_like(q)
    return jnp.concatenate([jnp.where(lane < HEAD_DIM, q, zero), jnp.where(lane >= HEAD_DIM, q, zero)], axis=0)


def _pick_halves(o):
    m = o.shape[0] // 2
    lane = lax.broadcasted_iota(jnp.int32, (m, LANES), 1)
    return jnp.where(lane < HEAD_DIM, o[:m], o[m:])


def _softmax_parts(parts):
    m = functools.reduce(jnp.maximum, [jnp.max(s, axis=-1, keepdims=True) for s in parts])
    ps = [jnp.exp2(s - m) for s in parts]
    l = functools.reduce(lambda a, b: a + b, [jnp.sum(p, axis=-1, keepdims=True) for p in ps])
    return ps, l


def _diff_combine(ps, l, lam, t):
    c = lam * l[:t] / l[t:]
    return [(p[:t] - p[t:] * c).astype(BF16) for p in ps], 1.0 / l[:t]


def _pair_bias(f_ref, rows):
    def one(hd):
        x = jnp.broadcast_to(f_ref[hd:hd + 1, :], (rows, BIAS_SPAN))
        return pltpu.roll(x, BIAS_SPAN - GROUP_Q + 1, 1, stride=1, stride_axis=0)

    return jnp.concatenate([one(0), one(1)], axis=0) * LOG2E


def _proj_kernel(x_ref, g1_ref, w_ref, gains_ref, cos_ref, sin_ref, pool_ref, *refs, tiles_per_seq):
    h = _rms(x_ref[...], g1_ref[...]).astype(BF16)
    cos = jnp.concatenate([cos_ref[...]] * N_BLOCKS, axis=1)
    sin = jnp.concatenate([sin_ref[...]] * N_BLOCKS, axis=1)
    pool = pool_ref[...]

    def group(i):
        return _dot(h, w_ref[:, i * WIDTH:(i + 1) * WIDTH])

    def head_norm(z, i):
        zz = (z * z).astype(BF16)
        ms = jnp.concatenate([_dot(zz[:, k * MXU_DIM:(k + 1) * MXU_DIM], pool) for k in range(WIDTH // MXU_DIM)],
                             axis=1)
        return z * lax.rsqrt(ms + EPS) * gains_ref[i:i + 1, :]

    def rope(z):
        lane = lax.broadcasted_iota(jnp.int32, z.shape, 1)
        half = HEAD_DIM // 2
        partner = jnp.where(lane % HEAD_DIM < half, pltpu.roll(z, WIDTH - half, 1), pltpu.roll(z, half, 1))
        return z * cos + partner * sin

    if tiles_per_seq == 0:
        qa_ref, ka_ref, va_ref, qb_ref, kb_ref, vb_ref = refs
        qa_ref[...] = (head_norm(group(0), 0) * Q_SCALE).astype(BF16)
        ka_ref[...] = head_norm(group(1), 1)
        va_ref[...] = group(2)
        qb_ref[...] = (rope(head_norm(group(3), 2)) * Q_SCALE).astype(BF16)
        kb_ref[...] = rope(head_norm(group(4), 3))
        vb_ref[...] = group(5)
        return

    kgain_ref, cos_t_ref, sin_t_ref = refs[:3]
    qa_ref, kat_ref, va_ref, qb_ref, kbt_ref, vb4_ref, kat_tail_ref, vat_tail_ref = refs[3:]
    tm = x_ref.shape[0]
    n_heads = WIDTH // HEAD_DIM

    def head_norm_t(zt, i):
        z3 = zt.reshape(n_heads, HEAD_DIM, tm)
        ms = jnp.mean(z3 * z3, axis=1, keepdims=True)
        gain = jnp.concatenate([kgain_ref[i]] * (tm // LANES), axis=1)
        return z3 * lax.rsqrt(ms + EPS) * gain[None]

    def rope_t(z3):
        half = HEAD_DIM // 2
        cos_t = cos_t_ref[...][None]
        sin_t = sin_t_ref[...][None]
        x1, x2 = z3[:, :half], z3[:, half:]
        return jnp.concatenate([x1 * cos_t - x2 * sin_t, x2 * cos_t + x1 * sin_t], axis=1)

    kbt_ref[0] = rope_t(head_norm_t(group(4).T, 1)).reshape(WIDTH, tm)
    kat = head_norm_t(group(1).T, 0).reshape(WIDTH, tm)
    kat_ref[0] = kat.astype(BF16)
    va = group(2)
    va_ref[...] = va.astype(BF16)
    qb_ref[...] = (rope(head_norm(group(3), 2)) * Q_SCALE).astype(BF16)
    qa_ref[...] = (head_norm(group(0), 0) * Q_SCALE).astype(BF16)
    vb = group(5)
    for hd in range(N_BLOCKS):
        vb4_ref[pl.ds(hd, tm, stride=N_BLOCKS), :] = vb[:, hd * LANES:(hd + 1) * LANES]

    @pl.when(pl.program_id(0) % tiles_per_seq == tiles_per_seq - 1)
    def _():
        keep = kat_tail_ref.shape[2]
        kat_tail_ref[0] = kat[:, tm - keep:]
        vat_tail_ref[0] = va[tm - keep:].T


def _proj_specs(tm, n_pos_blocks, consts):
    row = lambda w: pl.BlockSpec((tm, w), lambda i: (i, 0))
    pos = pl.BlockSpec((tm, LANES), lambda i: (i % n_pos_blocks, 0))
    g1, w_in, gains, pool = consts
    return row, [row(D_MODEL), _const_spec(g1.shape), _const_spec(w_in.shape), _const_spec(gains.shape),
                 pos, pos, _const_spec(pool.shape)]


def _proj_sample(x2d, g1, w_in, gains, cos, sin, pool):
    n = x2d.shape[0]
    row, in_specs = _proj_specs(n, 1, (g1, w_in, gains, pool))
    out_dtypes = (BF16, F32, F32, BF16, F32, F32)
    return pl.pallas_call(
        functools.partial(_proj_kernel, tiles_per_seq=0),
        grid=(1,),
        in_specs=in_specs,
        out_specs=[row(WIDTH)] * N_GROUPS,
        out_shape=[jax.ShapeDtypeStruct((n, WIDTH), dt) for dt in out_dtypes],
        compiler_params=_params(1),
        name="proj_sample",
    )(x2d, g1, w_in, gains, cos, sin, pool)


def _proj_prompt(x2d, b, s, keep, g1, w_in, gains, cos, sin, pool, kgain_t, cos_t, sin_t):
    n = x2d.shape[0]
    tm = 2 * ROW_TILE
    tps = s // tm
    row, in_specs = _proj_specs(tm, tps, (g1, w_in, gains, pool))
    pos_t = pl.BlockSpec((HEAD_DIM // 2, tm), lambda i: (0, i % tps))
    in_specs = in_specs + [_const_spec(kgain_t.shape), pos_t, pos_t]
    feat = pl.BlockSpec((1, WIDTH, tm), lambda i: (i // tps, 0, i % tps))
    tail = pl.BlockSpec((1, WIDTH, keep), lambda i: (i // tps, 0, 0))
    out_specs = [row(WIDTH), feat, row(WIDTH), row(WIDTH), feat,
                 pl.BlockSpec((tm * N_BLOCKS, LANES), lambda i: (i, 0)), tail, tail]
    out_shape = [jax.ShapeDtypeStruct((n, WIDTH), BF16),
                 jax.ShapeDtypeStruct((b, WIDTH, s), BF16),
                 jax.ShapeDtypeStruct((n, WIDTH), BF16),
                 jax.ShapeDtypeStruct((n, WIDTH), BF16),
                 jax.ShapeDtypeStruct((b, WIDTH, s), F32),
                 jax.ShapeDtypeStruct((n * N_BLOCKS, LANES), F32),
                 jax.ShapeDtypeStruct((b, WIDTH, keep), F32),
                 jax.ShapeDtypeStruct((b, WIDTH, keep), F32)]
    return pl.pallas_call(
        functools.partial(_proj_kernel, tiles_per_seq=tps),
        grid=(n // tm,),
        in_specs=in_specs,
        out_specs=out_specs,
        out_shape=out_shape,
        compiler_params=_params(1),
        name="proj_prompt",
    )(x2d, g1, w_in, gains, cos, sin, pool, kgain_t, cos_t, sin_t)


def _group_bias(f_ref):
    bias = _pair_bias(f_ref, GROUP_Q)[:, :GROUP_KEYS]
    qc = (lax.broadcasted_iota(jnp.int32, bias.shape, 0) % GROUP_Q) // CHUNK
    kc = lax.broadcasted_iota(jnp.int32, bias.shape, 1) // CHUNK
    return jnp.where((kc >= qc) & (kc <= qc + BAND_PAST_CHUNKS), bias, NEG_INF)


def _band_prompt_kernel(q_ref, kt_ref, v_ref, f_ref, o_ref, bias_ref, kt_heads_ref):
    n_groups = q_ref.shape[1] // GROUP_Q
    bias_ref[...] = _group_bias(f_ref.at[0])
    kt = kt_ref[0]
    feat = lax.broadcasted_iota(jnp.int32, kt.shape, 0)
    kt_heads_ref[0] = jnp.where(feat < HEAD_DIM, kt, jnp.zeros_like(kt))
    kt_heads_ref[1] = jnp.where(feat >= HEAD_DIM, kt, jnp.zeros_like(kt))

    def window(g):
        return max(0, g * GROUP_Q - BAND_PAST), (g + 1) * GROUP_Q

    def scores(g):
        lo, hi = window(g)
        q = q_ref[0, g * GROUP_Q:(g + 1) * GROUP_Q, :]
        s = _dot(q, jnp.concatenate([kt_heads_ref[0, :, lo:hi], kt_heads_ref[1, :, lo:hi]], axis=1))
        s = jnp.concatenate([s[:, :hi - lo], s[:, hi - lo:]], axis=0)
        return s + bias_ref[:, GROUP_KEYS - (hi - lo):]

    nxt = scores(0)
    for g in range(n_groups):
        s = nxt
        if g + 1 < n_groups:
            nxt = scores(g + 1)
        lo, hi = window(g)
        (p,), l = _softmax_parts([s])
        o = _dot(p.astype(BF16), v_ref[0, lo:hi, :]) * (1.0 / l)
        o_ref[0, g * GROUP_Q:(g + 1) * GROUP_Q, :] = _pick_halves(o).astype(o_ref.dtype)


def _band_prompt(qa, kat, va, f):
    b, s, _ = qa.shape
    blk = pl.BlockSpec((1, s, LANES), lambda i, j: (i, 0, j))
    blk_t = pl.BlockSpec((1, LANES, s), lambda i, j: (i, j, 0))
    return pl.pallas_call(
        _band_prompt_kernel,
        grid=(b, N_BLOCKS),
        in_specs=[blk, blk_t, blk, pl.BlockSpec((1, 2, BIAS_SPAN), lambda i, j: (j, 0, 0))],
        out_specs=blk,
        out_shape=jax.ShapeDtypeStruct((b, s, WIDTH), BF16),
        scratch_shapes=[pltpu.VMEM((2 * GROUP_Q, GROUP_KEYS), F32), pltpu.VMEM((2, LANES, s), BF16)],
        compiler_params=_params(2),
        name="band_prompt",
    )(qa, kat, va, f)


def _lambda(lam_ref, lam_init):
    e1 = jnp.exp(jnp.sum(lam_ref[0:1, :] * lam_ref[1:2, :], axis=-1, keepdims=True))
    e2 = jnp.exp(jnp.sum(lam_ref[2:3, :] * lam_ref[3:4, :], axis=-1, keepdims=True))
    return e1 - e2 + lam_init


def _diff_out(o, subln, lam_init):
    return _rms(o, subln) * (1.0 - lam_init)


def _diff_prompt_kernel(q_ref, kt_ref, v4_ref, lam_ref, subln_ref, o_ref, kbf_ref, vbf_ref, *, lam_init):
    s_len = q_ref.shape[1]
    tq = Q_BLOCK_B
    kt = kt_ref[0].astype(BF16)
    feat = lax.broadcasted_iota(jnp.int32, kt.shape, 0)
    kbf_ref[0] = jnp.where(feat < HEAD_DIM, kt, jnp.zeros_like(kt))
    kbf_ref[1] = jnp.where(feat >= HEAD_DIM, kt, jnp.zeros_like(kt))
    vbf_ref[...] = v4_ref[0, pl.ds(pl.program_id(1), s_len, stride=N_BLOCKS), :].astype(BF16)
    lam = _lambda(lam_ref, lam_init)
    row = lax.broadcasted_iota(jnp.int32, (2 * tq, tq), 0) % tq
    col = lax.broadcasted_iota(jnp.int32, (2 * tq, tq), 1)
    diag_mask = jnp.where(col // CHUNK <= row // CHUNK, 0.0, NEG_INF).astype(F32)

    def both(q, lo, hi):
        s = _dot(q, jnp.concatenate([kbf_ref[0, :, lo:hi], kbf_ref[1, :, lo:hi]], axis=1))
        return jnp.concatenate([s[:, :hi - lo], s[:, hi - lo:]], axis=0)

    def scores(i):
        r0 = i * tq
        q = q_ref[0, r0:r0 + tq, :]
        parts = [both(q, r0, r0 + tq) + diag_mask]
        if i > 0:
            parts.append(both(q, 0, r0))
        return parts

    n_blocks = s_len // tq
    nxt = scores(0)
    for i in range(n_blocks):
        r0 = i * tq
        parts = nxt
        if i + 1 < n_blocks:
            nxt = scores(i + 1)
        a, inv = _diff_combine(*_softmax_parts(parts), lam, tq)
        o = _dot(a[0], vbf_ref[r0:r0 + tq, :])
        if i > 0:
            o = o + _dot(a[1], vbf_ref[0:r0, :])
        o_ref[0, r0:r0 + tq, :] = _diff_out(o * inv, subln_ref[...], lam_init).astype(o_ref.dtype)


def _diff_prompt(qb, kbt, vb4, lam_vecs, subln, lam_init):
    b, s, _ = qb.shape
    blk = pl.BlockSpec((1, s, LANES), lambda i, j: (i, 0, j))
    return pl.pallas_call(
        functools.partial(_diff_prompt_kernel, lam_init=lam_init),
        grid=(b, N_BLOCKS),
        in_specs=[blk, pl.BlockSpec((1, LANES, s), lambda i, j: (i, j, 0)),
                  pl.BlockSpec((1, s * N_BLOCKS, LANES), lambda i, j: (i, 0, 0)),
                  _const_spec(lam_vecs.shape), _const_spec(subln.shape)],
        out_specs=blk,
        out_shape=jax.ShapeDtypeStruct((b, s, WIDTH), BF16),
        scratch_shapes=[pltpu.VMEM((2, LANES, s), BF16), pltpu.VMEM((s, LANES), BF16)],
        compiler_params=_params(2),
        name="diff_prompt",
    )(qb, kbt, vb4, lam_vecs, subln)


def _sample_attn_kernel(qa_ref, ka_ref, va_ref, qb_ref, kb_ref, vb_ref, cakt_ref, cavt_ref, cbkt_ref, cbv4_ref,
                        f_ref, lam_ref, subln_ref, ya_ref, yb_ref, *, lam_init):
    t = qa_ref.shape[0]
    lc = cakt_ref.shape[2]
    past = cbkt_ref.shape[2]
    lam = _lambda(lam_ref, lam_init)
    blocks = [slice(j * LANES, (j + 1) * LANES) for j in range(N_BLOCKS)]
    sa = []
    for j, cols in enumerate(blocks):
        bias = _pair_bias(f_ref.at[j], t)
        qm = _lane_halves(qa_ref[:, cols])
        sa.append([_dot(qm, cakt_ref[0, cols, :].astype(BF16)) + bias[:, :lc],
                   _dot_nt(qm, ka_ref[:, cols].astype(BF16)) + bias[:, lc:lc + t]])
    sb = []
    for cols in blocks:
        qs = _lane_halves(qb_ref[:, cols])
        sb.append([_dot(qs, cbkt_ref[0, cols, :].astype(BF16)), _dot_nt(qs, kb_ref[:, cols].astype(BF16))])
    pa = [_softmax_parts(s) for s in sa]
    pb = [_diff_combine(*_softmax_parts(s), lam, t) for s in sb]
    for cols, ((pc, pn), l) in zip(blocks, pa):
        o = _dot_nt(pc.astype(BF16), cavt_ref[0, cols, :].astype(BF16)) + _dot(pn.astype(BF16),
                                                                             va_ref[:, cols].astype(BF16))
        ya_ref[:, cols] = _pick_halves(o * (1.0 / l)).astype(ya_ref.dtype)
    for j, (cols, ((ac, an), inv)) in enumerate(zip(blocks, pb)):
        vc = cbv4_ref[0, pl.ds(j, past, stride=N_BLOCKS), :].astype(BF16)
        o = (_dot(ac, vc) + _dot(an, vb_ref[:, cols].astype(BF16))) * inv
        yb_ref[:, cols] = _diff_out(o, subln_ref[...], lam_init).astype(yb_ref.dtype)


def _sample_attn(new, caches, f, lam_vecs, subln, lam_init, t):
    n = new[0].shape[0]
    b = n // t
    row = pl.BlockSpec((t, WIDTH), lambda i: (i, 0))
    cache = lambda c: pl.BlockSpec((1,) + c.shape[1:], lambda i: (i, 0, 0))
    return pl.pallas_call(
        functools.partial(_sample_attn_kernel, lam_init=lam_init),
        grid=(b,),
        in_specs=[row] * 6 + [cache(c) for c in caches]
        + [_const_spec(f.shape), _const_spec(lam_vecs.shape), _const_spec(subln.shape)],
        out_specs=[row, row],
        out_shape=[jax.ShapeDtypeStruct((n, WIDTH), BF16)] * 2,
        compiler_params=_params(1),
        name="sample_attn",
    )(*new, *caches, f, lam_vecs, subln)


def _merge_mlp_kernel(x_ref, ya_ref, yb_ref, g1_ref, wg_ref, bg_ref, wpa_ref, wpb_ref, wo_ref,
                      g2_ref, w1_ref, w2_ref, o_ref):
    x = x_ref[...]
    h = _rms(x, g1_ref[...]).astype(BF16)
    m = jax.nn.sigmoid(_dot(h, wg_ref[:, :D_MODEL]) + bg_ref[:, :D_MODEL]) * _dot(ya_ref[...], wpa_ref[...])
    m = m + jax.nn.sigmoid(_dot(h, wg_ref[:, D_MODEL:]) + bg_ref[:, D_MODEL:]) * _dot(yb_ref[...], wpb_ref[...])
    x1 = x + _dot(m.astype(BF16), wo_ref[...])
    hn = _rms(x1, g2_ref[...]).astype(BF16)
    acc = x1
    for c in range(D_FF // D_MODEL):
        cols = slice(c * D_MODEL, (c + 1) * D_MODEL)
        u = jnp.maximum(_dot(hn, w1_ref[:, cols]), 0.0)
        acc = acc + _dot((u * u).astype(BF16), w2_ref[cols, :])
    o_ref[...] = acc


def _merge_mlp(x2d, ya, yb, tm, g1, wg, bg, wpa, wpb, wo, g2, w1, w2):
    n = x2d.shape[0]
    row = lambda w: pl.BlockSpec((tm, w), lambda i: (i, 0))
    consts = (g1, wg, bg, wpa, wpb, wo, g2, w1, w2)
    return pl.pallas_call(
        _merge_mlp_kernel,
        grid=(n // tm,),
        in_specs=[row(D_MODEL), row(WIDTH), row(WIDTH)] + [_const_spec(c.shape) for c in consts],
        out_specs=row(D_MODEL),
        out_shape=jax.ShapeDtypeStruct((n, D_MODEL), F32),
        compiler_params=_params(1),
        name="merge_mlp",
    )(x2d, ya, yb, *consts)


def _rope_tables(pos):
    half = HEAD_DIM // 2
    inv_freq = ROPE_THETA ** (-jnp.arange(half, dtype=F32) / half)
    ang = pos.astype(F32)[:, None] * inv_freq[None, :]
    cos = jnp.cos(ang)
    sin = jnp.sin(ang)
    reps = LANES // HEAD_DIM
    return (jnp.tile(jnp.concatenate([cos, cos], axis=1), (1, reps)),
            jnp.tile(jnp.concatenate([-sin, sin], axis=1), (1, reps)), cos.T, sin.T)


def _band_offset_table(table):
    n_far = BAND_PAST + GROUP_Q - 1 - REL_PAST + 1
    n_fut = BIAS_SPAN - n_far - (N_REL - 1)
    h = table.shape[0]
    f = jnp.concatenate([jnp.broadcast_to(table[:, N_REL - 1:], (h, n_far)), table[:, N_REL - 2::-1],
                         jnp.broadcast_to(table[:, :1], (h, n_fut))], axis=1).astype(F32)
    return f.reshape(h // 2, 2, BIAS_SPAN)


def _pool_matrix():
    i = np.arange(MXU_DIM)
    return jnp.asarray((i[:, None] // HEAD_DIM == i[None, :] // HEAD_DIM) / HEAD_DIM, dtype=BF16)


def kernel(x_prompt, x_sample, cache_a_k, cache_a_v, cache_b_k, cache_b_v, ln1_g, w_in, qn_a, kn_a, rel_bias, qn_b, kn_b, lam_q1, lam_k1, lam_q2, lam_k2, subln_g, w_gate, b_gate, w_proj_a, w_proj_b, w_out, ln2_g, w_ff1, w_ff2):
    depth = w_in.shape[0]
    b, s, d = x_prompt.shape
    db, t, _ = x_sample.shape
    past_len = cache_b_k.shape[2]
    lc = cache_a_k.shape[2]
    keep = min(BAND_PAST, s)
    assert d == D_MODEL and s % (2 * ROW_TILE) == 0 and keep <= 2 * ROW_TILE and keep % LANES == 0
    assert lc == BAND_PAST and t <= GROUP_Q

    cos_p, sin_p, cos_pt, sin_pt = _rope_tables(jnp.arange(s))
    cos_s, sin_s = (jnp.tile(a, (db, 1)) for a in _rope_tables(past_len + jnp.arange(t))[:2])
    lane_const = lambda g: jnp.broadcast_to(g[:, None], (HEAD_DIM, LANES))
    pool = _pool_matrix()
    tile_heads = lambda g: jnp.tile(g, WIDTH // HEAD_DIM)

    xp = x_prompt.reshape(b * s, d)
    xs = x_sample.reshape(db * t, d)
    outs = [[] for _ in range(8)]
    for l in range(depth):
        lam_init = 0.8 - 0.6 * math.exp(-0.3 * l)
        g1 = ln1_g[l][None]
        g2 = ln2_g[l][None]
        gains = jnp.stack([tile_heads(qn_a[l]), tile_heads(kn_a[l]), tile_heads(qn_b[l]), tile_heads(kn_b[l])])
        lam_vecs = jnp.stack([lam_q1[l], lam_k1[l], lam_q2[l], lam_k2[l]])
        subln = subln_g[l][None]
        bias_f = _band_offset_table(rel_bias[l])
        w_in_l = w_in[l].astype(BF16)
        merge_w = (g1, w_gate[l].astype(BF16), b_gate[l][None], w_proj_a[l].astype(BF16), w_proj_b[l].astype(BF16),
                   w_out[l].astype(BF16), g2, w_ff1[l].astype(BF16), w_ff2[l].astype(BF16))

        kgain_t = jnp.stack([lane_const(kn_a[l]), lane_const(kn_b[l])])
        qa, kat, va, qb, kbt, vb4, kat_tail, vat_tail = _proj_prompt(xp, b, s, keep, g1, w_in_l, gains, cos_p, sin_p,
                                                                     pool, kgain_t, cos_pt, sin_pt)
        as3 = lambda a: a.reshape(b, s, WIDTH)
        ya = _band_prompt(as3(qa), kat, as3(va), bias_f)
        yb = _diff_prompt(as3(qb), kbt, vb4.reshape(b, s * N_BLOCKS, LANES), lam_vecs, subln, lam_init)
        xp = _merge_mlp(xp, ya.reshape(b * s, WIDTH), yb.reshape(b * s, WIDTH), 2 * ROW_TILE, *merge_w)
        token_major_a = lambda a: a.reshape(b, N_HEADS_A, HEAD_DIM, keep).transpose(0, 3, 1, 2)
        outs[0].append(token_major_a(kat_tail))
        outs[1].append(token_major_a(vat_tail))
        outs[2].append(kbt.reshape(b, N_HEADS_B, 2, HEAD_DIM, s).transpose(0, 4, 1, 2, 3))
        outs[3].append(vb4.reshape(b, s, N_HEADS_B, 2 * HEAD_DIM))

        new = _proj_sample(xs, g1, w_in_l, gains, cos_s, sin_s, pool)
        caches = (cache_a_k[l].transpose(0, 2, 3, 1).reshape(db, WIDTH, lc),
                  cache_a_v[l].transpose(0, 2, 3, 1).reshape(db, WIDTH, lc),
                  cache_b_k[l].transpose(0, 2, 3, 4, 1).reshape(db, WIDTH, past_len),
                  cache_b_v[l].reshape(db, past_len * N_BLOCKS, LANES))
        ya_s, yb_s = _sample_attn(new, caches, bias_f, lam_vecs, subln, lam_init, t)
        xs = _merge_mlp(xs, ya_s, yb_s, db * t, *merge_w)
        outs[4].append(new[1].reshape(db, t, N_HEADS_A, HEAD_DIM))
        outs[5].append(new[2].reshape(db, t, N_HEADS_A, HEAD_DIM))
        outs[6].append(new[4].reshape(db, t, N_HEADS_B, 2, HEAD_DIM))
        outs[7].append(new[5].reshape(db, t, N_HEADS_B, 2 * HEAD_DIM))

    return (xp.reshape(b, s, d), xs.reshape(db, t, d)) + tuple(jnp.stack(o) for o in outs)
```

```python
import functools
import math

import jax
import jax.numpy as jnp
import numpy as np
from jax import lax
from jax.experimental import pallas as pl
from jax.experimental.pallas import tpu as pltpu

D_MODEL = 1024
CHUNK = 64
HEAD_DIM = 64
N_HEADS_A = 8
N_HEADS_B = 4
BAND_PAST_CHUNKS = 8
BAND_PAST = BAND_PAST_CHUNKS * CHUNK
REL_FUT = CHUNK - 1
REL_PAST = 256
N_REL = REL_FUT + REL_PAST + 1
WIDTH = 512
N_GROUPS = 6
D_FF = 4 * D_MODEL
ROPE_THETA = 10000.0
EPS = 1e-6
NEG_INF = -1e30
LOG2E = math.log2(math.e)
Q_SCALE = HEAD_DIM ** -0.5 * LOG2E

LANES = 128
N_BLOCKS = WIDTH // LANES
MXU_DIM = 256
GROUP_CHUNKS = 2
GROUP_Q = GROUP_CHUNKS * CHUNK
GROUP_KEYS = BAND_PAST + GROUP_Q
BIAS_SPAN = GROUP_KEYS + GROUP_Q
Q_BLOCK_B = 256
ROW_TILE = 512
VMEM_LIMIT = 60 * 1024 * 1024

BF16 = jnp.bfloat16
F32 = jnp.float32


def _dot(a, b):
    return jnp.dot(a, b, preferred_element_type=F32)


def _dot_nt(a, b):
    return lax.dot_general(a, b, (((1,), (1,)), ((), ())), preferred_element_type=F32)


def _rms(x, g):
    return x * lax.rsqrt(jnp.mean(x * x, axis=-1, keepdims=True) + EPS) * g


def _const_spec(shape):
    nd = len(shape)
    return pl.BlockSpec(shape, lambda *_: (0,) * nd, pipeline_mode=pl.Buffered(1))


def _params(n_grid_axes):
    return pltpu.CompilerParams(dimension_semantics=("arbitrary",) * n_grid_axes, vmem_limit_bytes=VMEM_LIMIT)


def _lane_halves(q):
    lane = lax.broadcasted_iota(jnp.int32, q.shape, 1)
    zero = jnp.zeros_like(q)
    return jnp.concatenate([jnp.where(lane < HEAD_DIM, q, zero), jnp.where(lane >= HEAD_DIM, q, zero)], axis=0)


def _pick_halves(o):
    m = o.shape[0] // 2
    lane = lax.broadcasted_iota(jnp.int32, (m, LANES), 1)
    return jnp.where(lane < HEAD_DIM, o[:m], o[m:])


def _softmax_parts(parts):
    m = functools.reduce(jnp.maximum, [jnp.max(s, axis=-1, keepdims=True) for s in parts])
    ps = [jnp.exp2(s - m) for s in parts]
    l = functools.reduce(lambda a, b: a + b, [jnp.sum(p, axis=-1, keepdims=True) for p in ps])
    return ps, l


def _diff_combine(ps, l, lam, t):
    c = lam * l[:t] / l[t:]
    return [(p[:t] - p[t:] * c).astype(BF16) for p in ps], 1.0 / l[:t]


def _pair_bias(f_ref, rows):
    def one(hd):
        x = jnp.broadcast_to(f_ref[hd:hd + 1, :], (rows, BIAS_SPAN))
        return pltpu.roll(x, BIAS_SPAN - GROUP_Q + 1, 1, stride=1, stride_axis=0)

    return jnp.concatenate([one(0), one(1)], axis=0) * LOG2E


def _proj_kernel(x_ref, g1_ref, w_ref, gains_ref, cos_ref, sin_ref, pool_ref, *refs, tiles_per_seq):
    h = _rms(x_ref[...], g1_ref[...]).astype(BF16)
    cos = jnp.concatenate([cos_ref[...]] * N_BLOCKS, axis=1)
    sin = jnp.concatenate([sin_ref[...]] * N_BLOCKS, axis=1)
    pool = pool_ref[...]

    def group(i):
        return _dot(h, w_ref[:, i * WIDTH:(i + 1) * WIDTH])

    def head_norm(z, i):
        zz = (z * z).astype(BF16)
        ms = jnp.concatenate([_dot(zz[:, k * MXU_DIM:(k + 1) * MXU_DIM], pool) for k in range(WIDTH // MXU_DIM)],
                             axis=1)
        return z * lax.rsqrt(ms + EPS) * gains_ref[i:i + 1, :]

    def rope(z):
        lane = lax.broadcasted_iota(jnp.int32, z.shape, 1)
        half = HEAD_DIM // 2
        partner = jnp.where(lane % HEAD_DIM < half, pltpu.roll(z, WIDTH - half, 1), pltpu.roll(z, half, 1))
        return z * cos + partner * sin

    if tiles_per_seq == 0:
        qa_ref, ka_ref, va_ref, qb_ref, kb_ref, vb_ref = refs
        qa_ref[...] = (head_norm(group(0), 0) * Q_SCALE).astype(BF16)
        ka_ref[...] = head_norm(group(1), 1)
        va_ref[...] = group(2)
        qb_ref[...] = (rope(head_norm(group(3), 2)) * Q_SCALE).astype(BF16)
        kb_ref[...] = rope(head_norm(group(4), 3))
        vb_ref[...] = group(5)
        return

    kgain_ref, cos_t_ref, sin_t_ref = refs[:3]
    qa_ref, kat_ref, va_ref, qb_ref, kbt_ref, vb4_ref, kat_tail_ref, vat_tail_ref = refs[3:]
    tm = x_ref.shape[0]
    n_heads = WIDTH // HEAD_DIM

    def head_norm_t(zt, i):
        z3 = zt.reshape(n_heads, HEAD_DIM, tm)
        ms = jnp.mean(z3 * z3, axis=1, keepdims=True)
        gain = jnp.concatenate([kgain_ref[i]] * (tm // LANES), axis=1)
        return z3 * lax.rsqrt(ms + EPS) * gain[None]

    def rope_t(z3):
        half = HEAD_DIM // 2
        cos_t = cos_t_ref[...][None]
        sin_t = sin_t_ref[...][None]
        x1, x2 = z3[:, :half], z3[:, half:]
        return jnp.concatenate([x1 * cos_t - x2 * sin_t, x2 * cos_t + x1 * sin_t], axis=1)

    kbt_ref[0] = rope_t(head_norm_t(group(4).T, 1)).reshape(WIDTH, tm)
    kat = head_norm_t(group(1).T, 0).reshape(WIDTH, tm)
    kat_ref[0] = kat.astype(BF16)
    va = group(2)
    va_ref[...] = va.astype(BF16)
    qb_ref[...] = (rope(head_norm(group(3), 2)) * Q_SCALE).astype(BF16)
    qa_ref[...] = (head_norm(group(0), 0) * Q_SCALE).astype(BF16)
    vb = group(5)
    for hd in range(N_BLOCKS):
        vb4_ref[pl.ds(hd, tm, stride=N_BLOCKS), :] = vb[:, hd * LANES:(hd + 1) * LANES]

    @pl.when(pl.program_id(0) % tiles_per_seq == tiles_per_seq - 1)
    def _():
        keep = kat_tail_ref.shape[2]
        kat_tail_ref[0] = kat[:, tm - keep:]
        vat_tail_ref[0] = va[tm - keep:].T


def _proj_specs(tm, n_pos_blocks, consts):
    row = lambda w: pl.BlockSpec((tm, w), lambda i: (i, 0))
    pos = pl.BlockSpec((tm, LANES), lambda i: (i % n_pos_blocks, 0))
    g1, w_in, gains, pool = consts
    return row, [row(D_MODEL), _const_spec(g1.shape), _const_spec(w_in.shape), _const_spec(gains.shape),
                 pos, pos, _const_spec(pool.shape)]


def _proj_sample(x2d, g1, w_in, gains, cos, sin, pool):
    n = x2d.shape[0]
    row, in_specs = _proj_specs(n, 1, (g1, w_in, gains, pool))
    out_dtypes = (BF16, F32, F32, BF16, F32, F32)
    return pl.pallas_call(
        functools.partial(_proj_kernel, tiles_per_seq=0),
        grid=(1,),
        in_specs=in_specs,
        out_specs=[row(WIDTH)] * N_GROUPS,
        out_shape=[jax.ShapeDtypeStruct((n, WIDTH), dt) for dt in out_dtypes],
        compiler_params=_params(1),
        name="proj_sample",
    )(x2d, g1, w_in, gains, cos, sin, pool)


def _proj_prompt(x2d, b, s, keep, g1, w_in, gains, cos, sin, pool, kgain_t, cos_t, sin_t):
    n = x2d.shape[0]
    tm = 2 * ROW_TILE
    tps = s // tm
    row, in_specs = _proj_specs(tm, tps, (g1, w_in, gains, pool))
    pos_t = pl.BlockSpec((HEAD_DIM // 2, tm), lambda i: (0, i % tps))
    in_specs = in_specs + [_const_spec(kgain_t.shape), pos_t, pos_t]
    feat = pl.BlockSpec((1, WIDTH, tm), lambda i: (i // tps, 0, i % tps))
    tail = pl.BlockSpec((1, WIDTH, keep), lambda i: (i // tps, 0, 0))
    out_specs = [row(WIDTH), feat, row(WIDTH), row(WIDTH), feat,
                 pl.BlockSpec((tm * N_BLOCKS, LANES), lambda i: (i, 0)), tail, tail]
    out_shape = [jax.ShapeDtypeStruct((n, WIDTH), BF16),
                 jax.ShapeDtypeStruct((b, WIDTH, s), BF16),
                 jax.ShapeDtypeStruct((n, WIDTH), BF16),
                 jax.ShapeDtypeStruct((n, WIDTH), BF16),
                 jax.ShapeDtypeStruct((b, WIDTH, s), F32),
                 jax.ShapeDtypeStruct((n * N_BLOCKS, LANES), F32),
                 jax.ShapeDtypeStruct((b, WIDTH, keep), F32),
                 jax.ShapeDtypeStruct((b, WIDTH, keep), F32)]
    return pl.pallas_call(
        functools.partial(_proj_kernel, tiles_per_seq=tps),
        grid=(n // tm,),
        in_specs=in_specs,
        out_specs=out_specs,
        out_shape=out_shape,
        compiler_params=_params(1),
        name="proj_prompt",
    )(x2d, g1, w_in, gains, cos, sin, pool, kgain_t, cos_t, sin_t)


def _group_bias(f_ref):
    bias = _pair_bias(f_ref, GROUP_Q)[:, :GROUP_KEYS]
    qc = (lax.broadcasted_iota(jnp.int32, bias.shape, 0) % GROUP_Q) // CHUNK
    kc = lax.broadcasted_iota(jnp.int32, bias.shape, 1) // CHUNK
    return jnp.where((kc >= qc) & (kc <= qc + BAND_PAST_CHUNKS), bias, NEG_INF)


def _band_prompt_kernel(q_ref, kt_ref, v_ref, f_ref, o_ref, bias_ref, kt_heads_ref):
    n_groups = q_ref.shape[1] // GROUP_Q
    bias_ref[...] = _group_bias(f_ref.at[0])
    kt = kt_ref[0]
    feat = lax.broadcasted_iota(jnp.int32, kt.shape, 0)
    kt_heads_ref[0] = jnp.where(feat < HEAD_DIM, kt, jnp.zeros_like(kt))
    kt_heads_ref[1] = jnp.where(feat >= HEAD_DIM, kt, jnp.zeros_like(kt))

    def window(g):
        return max(0, g * GROUP_Q - BAND_PAST), (g + 1) * GROUP_Q

    def scores(g):
        lo, hi = window(g)
        q = q_ref[0, g * GROUP_Q:(g + 1) * GROUP_Q, :]
        s = _dot(q, jnp.concatenate([kt_heads_ref[0, :, lo:hi], kt_heads_ref[1, :, lo:hi]], axis=1))
        s = jnp.concatenate([s[:, :hi - lo], s[:, hi - lo:]], axis=0)
        return s + bias_ref[:, GROUP_KEYS - (hi - lo):]

    nxt = scores(0)
    for g in range(n_groups):
        s = nxt
        if g + 1 < n_groups:
            nxt = scores(g + 1)
        lo, hi = window(g)
        (p,), l = _softmax_parts([s])
        o = _dot(p.astype(BF16), v_ref[0, lo:hi, :]) * (1.0 / l)
        o_ref[0, g * GROUP_Q:(g + 1) * GROUP_Q, :] = _pick_halves(o).astype(o_ref.dtype)


def _band_prompt(qa, kat, va, f):
    b, s, _ = qa.shape
    blk = pl.BlockSpec((1, s, LANES), lambda i, j: (i, 0, j))
    blk_t = pl.BlockSpec((1, LANES, s), lambda i, j: (i, j, 0))
    return pl.pallas_call(
        _band_prompt_kernel,
        grid=(b, N_BLOCKS),
        in_specs=[blk, blk_t, blk, pl.BlockSpec((1, 2, BIAS_SPAN), lambda i, j: (j, 0, 0))],
        out_specs=blk,
        out_shape=jax.ShapeDtypeStruct((b, s, WIDTH), BF16),
        scratch_shapes=[pltpu.VMEM((2 * GROUP_Q, GROUP_KEYS), F32), pltpu.VMEM((2, LANES, s), BF16)],
        compiler_params=_params(2),
        name="band_prompt",
    )(qa, kat, va, f)


def _lambda(lam_ref, lam_init):
    e1 = jnp.exp(jnp.sum(lam_ref[0:1, :] * lam_ref[1:2, :], axis=-1, keepdims=True))
    e2 = jnp.exp(jnp.sum(lam_ref[2:3, :] * lam_ref[3:4, :], axis=-1, keepdims=True))
    return e1 - e2 + lam_init


def _diff_out(o, subln, lam_init):
    return _rms(o, subln) * (1.0 - lam_init)


def _diff_prompt_kernel(q_ref, kt_ref, v4_ref, lam_ref, subln_ref, o_ref, kbf_ref, vbf_ref, *, lam_init):
    s_len = q_ref.shape[1]
    tq = Q_BLOCK_B
    kbf_ref[...] = kt_ref[0].astype(BF16)
    vbf_ref[...] = v4_ref[0, pl.ds(pl.program_id(1), s_len, stride=N_BLOCKS), :].astype(BF16)
    lam = _lambda(lam_ref, lam_init)
    row = lax.broadcasted_iota(jnp.int32, (2 * tq, tq), 0) % tq
    col = lax.broadcasted_iota(jnp.int32, (2 * tq, tq), 1)
    diag_mask = jnp.where(col // CHUNK <= row // CHUNK, 0.0, NEG_INF).astype(F32)

    def scores(i):
        r0 = i * tq
        qs = _lane_halves(q_ref[0, r0:r0 + tq, :])
        parts = [_dot(qs, kbf_ref[:, r0:r0 + tq]) + diag_mask]
        if i > 0:
            parts.append(_dot(qs, kbf_ref[:, 0:r0]))
        return parts

    n_blocks = s_len // tq
    nxt = scores(0)
    for i in range(n_blocks):
        r0 = i * tq
        parts = nxt
        if i + 1 < n_blocks:
            nxt = scores(i + 1)
        a, inv = _diff_combine(*_softmax_parts(parts), lam, tq)
        o = _dot(a[0], vbf_ref[r0:r0 + tq, :])
        if i > 0:
            o = o + _dot(a[1], vbf_ref[0:r0, :])
        o_ref[0, r0:r0 + tq, :] = _diff_out(o * inv, subln_ref[...], lam_init).astype(o_ref.dtype)


def _diff_prompt(qb, kbt, vb4, lam_vecs, subln, lam_init):
    b, s, _ = qb.shape
    blk = pl.BlockSpec((1, s, LANES), lambda i, j: (i, 0, j))
    return pl.pallas_call(
        functools.partial(_diff_prompt_kernel, lam_init=lam_init),
        grid=(b, N_BLOCKS),
        in_specs=[blk, pl.BlockSpec((1, LANES, s), lambda i, j: (i, j, 0)),
                  pl.BlockSpec((1, s * N_BLOCKS, LANES), lambda i, j: (i, 0, 0)),
                  _const_spec(lam_vecs.shape), _const_spec(subln.shape)],
        out_specs=blk,
        out_shape=jax.ShapeDtypeStruct((b, s, WIDTH), BF16),
        scratch_shapes=[pltpu.VMEM((LANES, s), BF16), pltpu.VMEM((s, LANES), BF16)],
        compiler_params=_params(2),
        name="diff_prompt",
    )(qb, kbt, vb4, lam_vecs, subln)


def _sample_attn_kernel(qa_ref, ka_ref, va_ref, qb_ref, kb_ref, vb_ref, cakt_ref, cavt_ref, cbkt_ref, cbv4_ref,
                        f_ref, lam_ref, subln_ref, ya_ref, yb_ref, *, lam_init):
    t = qa_ref.shape[0]
    lc = cakt_ref.shape[2]
    past = cbkt_ref.shape[2]
    lam = _lambda(lam_ref, lam_init)
    blocks = [slice(j * LANES, (j + 1) * LANES) for j in range(N_BLOCKS)]
    sa = []
    for j, cols in enumerate(blocks):
        bias = _pair_bias(f_ref.at[j], t)
        qm = _lane_halves(qa_ref[:, cols])
        sa.append([_dot(qm, cakt_ref[0, cols, :].astype(BF16)) + bias[:, :lc],
                   _dot_nt(qm, ka_ref[:, cols].astype(BF16)) + bias[:, lc:lc + t]])
    sb = []
    for cols in blocks:
        qs = _lane_halves(qb_ref[:, cols])
        sb.append([_dot(qs, cbkt_ref[0, cols, :].astype(BF16)), _dot_nt(qs, kb_ref[:, cols].astype(BF16))])
    pa = [_softmax_parts(s) for s in sa]
    pb = [_diff_combine(*_softmax_parts(s), lam, t) for s in sb]
    for cols, ((pc, pn), l) in zip(blocks, pa):
        o = _dot_nt(pc.astype(BF16), cavt_ref[0, cols, :].astype(BF16)) + _dot(pn.astype(BF16),
                                                                             va_ref[:, cols].astype(BF16))
        ya_ref[:, cols] = _pick_halves(o * (1.0 / l)).astype(ya_ref.dtype)
    for j, (cols, ((ac, an), inv)) in enumerate(zip(blocks, pb)):
        vc = cbv4_ref[0, pl.ds(j, past, stride=N_BLOCKS), :].astype(BF16)
        o = (_dot(ac, vc) + _dot(an, vb_ref[:, cols].astype(BF16))) * inv
        yb_ref[:, cols] = _diff_out(o, subln_ref[...], lam_init).astype(yb_ref.dtype)


def _sample_attn(new, caches, f, lam_vecs, subln, lam_init, t):
    n = new[0].shape[0]
    b = n // t
    row = pl.BlockSpec((t, WIDTH), lambda i: (i, 0))
    cache = lambda c: pl.BlockSpec((1,) + c.shape[1:], lambda i: (i, 0, 0))
    return pl.pallas_call(
        functools.partial(_sample_attn_kernel, lam_init=lam_init),
        grid=(b,),
        in_specs=[row] * 6 + [cache(c) for c in caches]
        + [_const_spec(f.shape), _const_spec(lam_vecs.shape), _const_spec(subln.shape)],
        out_specs=[row, row],
        out_shape=[jax.ShapeDtypeStruct((n, WIDTH), BF16)] * 2,
        compiler_params=_params(1),
        name="sample_attn",
    )(*new, *caches, f, lam_vecs, subln)


def _merge_mlp_kernel(x_ref, ya_ref, yb_ref, g1_ref, wg_ref, bg_ref, wpa_ref, wpb_ref, wo_ref,
                      g2_ref, w1_ref, w2_ref, o_ref):
    x = x_ref[...]
    h = _rms(x, g1_ref[...]).astype(BF16)
    m = jax.nn.sigmoid(_dot(h, wg_ref[:, :D_MODEL]) + bg_ref[:, :D_MODEL]) * _dot(ya_ref[...], wpa_ref[...])
    m = m + jax.nn.sigmoid(_dot(h, wg_ref[:, D_MODEL:]) + bg_ref[:, D_MODEL:]) * _dot(yb_ref[...], wpb_ref[...])
    x1 = x + _dot(m.astype(BF16), wo_ref[...])
    hn = _rms(x1, g2_ref[...]).astype(BF16)
    acc = x1
    for c in range(D_FF // D_MODEL):
        cols = slice(c * D_MODEL, (c + 1) * D_MODEL)
        u = jnp.maximum(_dot(hn, w1_ref[:, cols]), 0.0)
        acc = acc + _dot((u * u).astype(BF16), w2_ref[cols, :])
    o_ref[...] = acc


def _merge_mlp(x2d, ya, yb, tm, g1, wg, bg, wpa, wpb, wo, g2, w1, w2):
    n = x2d.shape[0]
    row = lambda w: pl.BlockSpec((tm, w), lambda i: (i, 0))
    consts = (g1, wg, bg, wpa, wpb, wo, g2, w1, w2)
    return pl.pallas_call(
        _merge_mlp_kernel,
        grid=(n // tm,),
        in_specs=[row(D_MODEL), row(WIDTH), row(WIDTH)] + [_const_spec(c.shape) for c in consts],
        out_specs=row(D_MODEL),
        out_shape=jax.ShapeDtypeStruct((n, D_MODEL), F32),
        compiler_params=_params(1),
        name="merge_mlp",
    )(x2d, ya, yb, *consts)


def _rope_tables(pos):
    half = HEAD_DIM // 2
    inv_freq = ROPE_THETA ** (-jnp.arange(half, dtype=F32) / half)
    ang = pos.astype(F32)[:, None] * inv_freq[None, :]
    cos = jnp.cos(ang)
    sin = jnp.sin(ang)
    reps = LANES // HEAD_DIM
    return (jnp.tile(jnp.concatenate([cos, cos], axis=1), (1, reps)),
            jnp.tile(jnp.concatenate([-sin, sin], axis=1), (1, reps)), cos.T, sin.T)


def _band_offset_table(table):
    n_far = BAND_PAST + GROUP_Q - 1 - REL_PAST + 1
    n_fut = BIAS_SPAN - n_far - (N_REL - 1)
    h = table.shape[0]
    f = jnp.concatenate([jnp.broadcast_to(table[:, N_REL - 1:], (h, n_far)), table[:, N_REL - 2::-1],
                         jnp.broadcast_to(table[:, :1], (h, n_fut))], axis=1).astype(F32)
    return f.reshape(h // 2, 2, BIAS_SPAN)


def _pool_matrix():
    i = np.arange(MXU_DIM)
    return jnp.asarray((i[:, None] // HEAD_DIM == i[None, :] // HEAD_DIM) / HEAD_DIM, dtype=BF16)


def kernel(x_prompt, x_sample, cache_a_k, cache_a_v, cache_b_k, cache_b_v, ln1_g, w_in, qn_a, kn_a, rel_bias, qn_b, kn_b, lam_q1, lam_k1, lam_q2, lam_k2, subln_g, w_gate, b_gate, w_proj_a, w_proj_b, w_out, ln2_g, w_ff1, w_ff2):
    depth = w_in.shape[0]
    b, s, d = x_prompt.shape
    db, t, _ = x_sample.shape
    past_len = cache_b_k.shape[2]
    lc = cache_a_k.shape[2]
    keep = min(BAND_PAST, s)
    assert d == D_MODEL and s % (2 * ROW_TILE) == 0 and keep <= 2 * ROW_TILE and keep % LANES == 0
    assert lc == BAND_PAST and t <= GROUP_Q

    cos_p, sin_p, cos_pt, sin_pt = _rope_tables(jnp.arange(s))
    cos_s, sin_s = (jnp.tile(a, (db, 1)) for a in _rope_tables(past_len + jnp.arange(t))[:2])
    lane_const = lambda g: jnp.broadcast_to(g[:, None], (HEAD_DIM, LANES))
    pool = _pool_matrix()
    tile_heads = lambda g: jnp.tile(g, WIDTH // HEAD_DIM)

    xp = x_prompt.reshape(b * s, d)
    xs = x_sample.reshape(db * t, d)
    outs = [[] for _ in range(8)]
    for l in range(depth):
        lam_init = 0.8 - 0.6 * math.exp(-0.3 * l)
        g1 = ln1_g[l][None]
        g2 = ln2_g[l][None]
        gains = jnp.stack([tile_heads(qn_a[l]), tile_heads(kn_a[l]), tile_heads(qn_b[l]), tile_heads(kn_b[l])])
        lam_vecs = jnp.stack([lam_q1[l], lam_k1[l], lam_q2[l], lam_k2[l]])
        subln = subln_g[l][None]
        bias_f = _band_offset_table(rel_bias[l])
        w_in_l = w_in[l].astype(BF16)
        merge_w = (g1, w_gate[l].astype(BF16), b_gate[l][None], w_proj_a[l].astype(BF16), w_proj_b[l].astype(BF16),
                   w_out[l].astype(BF16), g2, w_ff1[l].astype(BF16), w_ff2[l].astype(BF16))

        kgain_t = jnp.stack([lane_const(kn_a[l]), lane_const(kn_b[l])])
        qa, kat, va, qb, kbt, vb4, kat_tail, vat_tail = _proj_prompt(xp, b, s, keep, g1, w_in_l, gains, cos_p, sin_p,
                                                                     pool, kgain_t, cos_pt, sin_pt)
        as3 = lambda a: a.reshape(b, s, WIDTH)
        ya = _band_prompt(as3(qa), kat, as3(va), bias_f)
        yb = _diff_prompt(as3(qb), kbt, vb4.reshape(b, s * N_BLOCKS, LANES), lam_vecs, subln, lam_init)
        xp = _merge_mlp(xp, ya.reshape(b * s, WIDTH), yb.reshape(b * s, WIDTH), 2 * ROW_TILE, *merge_w)
        token_major_a = lambda a: a.reshape(b, N_HEADS_A, HEAD_DIM, keep).transpose(0, 3, 1, 2)
        outs[0].append(token_major_a(kat_tail))
        outs[1].append(token_major_a(vat_tail))
        outs[2].append(kbt.reshape(b, N_HEADS_B, 2, HEAD_DIM, s).transpose(0, 4, 1, 2, 3))
        outs[3].append(vb4.reshape(b, s, N_HEADS_B, 2 * HEAD_DIM))

        new = _proj_sample(xs, g1, w_in_l, gains, cos_s, sin_s, pool)
        caches = (cache_a_k[l].transpose(0, 2, 3, 1).reshape(db, WIDTH, lc),
                  cache_a_v[l].transpose(0, 2, 3, 1).reshape(db, WIDTH, lc),
                  cache_b_k[l].transpose(0, 2, 3, 4, 1).reshape(db, WIDTH, past_len),
                  cache_b_v[l].reshape(db, past_len * N_BLOCKS, LANES))
        ya_s, yb_s = _sample_attn(new, caches, bias_f, lam_vecs, subln, lam_init, t)
        xs = _merge_mlp(xs, ya_s, yb_s, db * t, *merge_w)
        outs[4].append(new[1].reshape(db, t, N_HEADS_A, HEAD_DIM))
        outs[5].append(new[2].reshape(db, t, N_HEADS_A, HEAD_DIM))
        outs[6].append(new[4].reshape(db, t, N_HEADS_B, 2, HEAD_DIM))
        outs[7].append(new[5].reshape(db, t, N_HEADS_B, 2 * HEAD_DIM))

    return (xp.reshape(b, s, d), xs.reshape(db, t, d)) + tuple(jnp.stack(o) for o in outs)
```

```python
import functools
import math

import jax
import jax.numpy as jnp
import numpy as np
from jax import lax
from jax.experimental import pallas as pl
from jax.experimental.pallas import tpu as pltpu

D_MODEL = 1024
CHUNK = 64
HEAD_DIM = 64
N_HEADS_A = 8
N_HEADS_B = 4
BAND_PAST_CHUNKS = 8
BAND_PAST = BAND_PAST_CHUNKS * CHUNK
REL_FUT = CHUNK - 1
REL_PAST = 256
N_REL = REL_FUT + REL_PAST + 1
WIDTH = 512
N_GROUPS = 6
D_FF = 4 * D_MODEL
ROPE_THETA = 10000.0
EPS = 1e-6
NEG_INF = -1e30
LOG2E = math.log2(math.e)
Q_SCALE = HEAD_DIM ** -0.5 * LOG2E

LANES = 128
N_BLOCKS = WIDTH // LANES
MXU_DIM = 256
GROUP_CHUNKS = 2
GROUP_Q = GROUP_CHUNKS * CHUNK
GROUP_KEYS = BAND_PAST + GROUP_Q
BIAS_SPAN = GROUP_KEYS + GROUP_Q
BAND_PAIRS_PER_STEP = 2
Q_BLOCK_B = 256
DIFF_HEADS_PER_STEP = 2
ROW_TILE = 512
VMEM_LIMIT = 60 * 1024 * 1024

BF16 = jnp.bfloat16
F32 = jnp.float32


def _dot(a, b):
    return jnp.dot(a, b, preferred_element_type=F32)


def _dot_nt(a, b):
    return lax.dot_general(a, b, (((1,), (1,)), ((), ())), preferred_element_type=F32)


def _rms(x, g):
    return x * lax.rsqrt(jnp.mean(x * x, axis=-1, keepdims=True) + EPS) * g


def _const_spec(shape):
    nd = len(shape)
    return pl.BlockSpec(shape, lambda *_: (0,) * nd, pipeline_mode=pl.Buffered(1))


def _params(n_grid_axes):
    return pltpu.CompilerParams(dimension_semantics=("arbitrary",) * n_grid_axes, vmem_limit_bytes=VMEM_LIMIT)


def _lane_halves(q):
    lane = lax.broadcasted_iota(jnp.int32, q.shape, 1)
    zero = jnp.zeros_like(q)
    return jnp.concatenate([jnp.where(lane < HEAD_DIM, q, zero), jnp.where(lane >= HEAD_DIM, q, zero)], axis=0)


def _pick_halves(o):
    m = o.shape[0] // 2
    lane = lax.broadcasted_iota(jnp.int32, (m, LANES), 1)
    return jnp.where(lane < HEAD_DIM, o[:m], o[m:])


def _softmax_parts(parts):
    m = functools.reduce(jnp.maximum, [jnp.max(s, axis=-1, keepdims=True) for s in parts])
    ps = [jnp.exp2(s - m) for s in parts]
    l = functools.reduce(lambda a, b: a + b, [jnp.sum(p, axis=-1, keepdims=True) for p in ps])
    return ps, l


def _diff_combine(ps, l, lam, t):
    c = lam * l[:t] / l[t:]
    return [(p[:t] - p[t:] * c).astype(BF16) for p in ps], 1.0 / l[:t]


def _pair_bias(f_ref, rows):
    def one(hd):
        x = jnp.broadcast_to(f_ref[hd:hd + 1, :], (rows, BIAS_SPAN))
        return pltpu.roll(x, BIAS_SPAN - GROUP_Q + 1, 1, stride=1, stride_axis=0)

    return jnp.concatenate([one(0), one(1)], axis=0) * LOG2E


def _proj_kernel(x_ref, g1_ref, w_ref, gains_ref, cos_ref, sin_ref, pool_ref, *refs, tiles_per_seq):
    h = _rms(x_ref[...], g1_ref[...]).astype(BF16)
    cos = jnp.concatenate([cos_ref[...]] * N_BLOCKS, axis=1)
    sin = jnp.concatenate([sin_ref[...]] * N_BLOCKS, axis=1)
    pool = pool_ref[...]

    def group(i):
        return _dot(h, w_ref[:, i * WIDTH:(i + 1) * WIDTH])

    def head_norm(z, i):
        zz = (z * z).astype(BF16)
        ms = jnp.concatenate([_dot(zz[:, k * MXU_DIM:(k + 1) * MXU_DIM], pool) for k in range(WIDTH // MXU_DIM)],
                             axis=1)
        return z * lax.rsqrt(ms + EPS) * gains_ref[i:i + 1, :]

    def rope(z):
        lane = lax.broadcasted_iota(jnp.int32, z.shape, 1)
        half = HEAD_DIM // 2
        partner = jnp.where(lane % HEAD_DIM < half, pltpu.roll(z, WIDTH - half, 1), pltpu.roll(z, half, 1))
        return z * cos + partner * sin

    if tiles_per_seq == 0:
        qa_ref, ka_ref, va_ref, qb_ref, kb_ref, vb_ref = refs
        qa_ref[...] = (head_norm(group(0), 0) * Q_SCALE).astype(BF16)
        ka_ref[...] = head_norm(group(1), 1)
        va_ref[...] = group(2)
        qb_ref[...] = (rope(head_norm(group(3), 2)) * Q_SCALE).astype(BF16)
        kb_ref[...] = rope(head_norm(group(4), 3))
        vb_ref[...] = group(5)
        return

    kgain_ref, cos_t_ref, sin_t_ref = refs[:3]
    qa_ref, kat_ref, va_ref, qb_ref, kbt_ref, vb4_ref, kat_tail_ref, vat_tail_ref = refs[3:]
    tm = x_ref.shape[0]
    n_heads = WIDTH // HEAD_DIM

    def head_norm_t(zt, i):
        z3 = zt.reshape(n_heads, HEAD_DIM, tm)
        ms = jnp.mean(z3 * z3, axis=1, keepdims=True)
        gain = jnp.concatenate([kgain_ref[i]] * (tm // LANES), axis=1)
        return z3 * lax.rsqrt(ms + EPS) * gain[None]

    def rope_t(z3):
        half = HEAD_DIM // 2
        cos_t = cos_t_ref[...][None]
        sin_t = sin_t_ref[...][None]
        x1, x2 = z3[:, :half], z3[:, half:]
        return jnp.concatenate([x1 * cos_t - x2 * sin_t, x2 * cos_t + x1 * sin_t], axis=1)

    kbt_ref[0] = rope_t(head_norm_t(group(4).T, 1)).reshape(WIDTH, tm)
    kat = head_norm_t(group(1).T, 0).reshape(WIDTH, tm)
    kat_ref[0] = kat.astype(BF16)
    va = group(2)
    va_ref[...] = va.astype(BF16)
    qb_ref[...] = (rope(head_norm(group(3), 2)) * Q_SCALE).astype(BF16)
    qa_ref[...] = (head_norm(group(0), 0) * Q_SCALE).astype(BF16)
    vb = group(5)
    for hd in range(N_BLOCKS):
        vb4_ref[pl.ds(hd, tm, stride=N_BLOCKS), :] = vb[:, hd * LANES:(hd + 1) * LANES]

    @pl.when(pl.program_id(0) % tiles_per_seq == tiles_per_seq - 1)
    def _():
        keep = kat_tail_ref.shape[2]
        kat_tail_ref[0] = kat[:, tm - keep:]
        vat_tail_ref[0] = va[tm - keep:].T


def _proj_specs(tm, n_pos_blocks, consts):
    row = lambda w: pl.BlockSpec((tm, w), lambda i: (i, 0))
    pos = pl.BlockSpec((tm, LANES), lambda i: (i % n_pos_blocks, 0))
    g1, w_in, gains, pool = consts
    return row, [row(D_MODEL), _const_spec(g1.shape), _const_spec(w_in.shape), _const_spec(gains.shape),
                 pos, pos, _const_spec(pool.shape)]


def _proj_sample(x2d, g1, w_in, gains, cos, sin, pool):
    n = x2d.shape[0]
    row, in_specs = _proj_specs(n, 1, (g1, w_in, gains, pool))
    out_dtypes = (BF16, F32, F32, BF16, F32, F32)
    return pl.pallas_call(
        functools.partial(_proj_kernel, tiles_per_seq=0),
        grid=(1,),
        in_specs=in_specs,
        out_specs=[row(WIDTH)] * N_GROUPS,
        out_shape=[jax.ShapeDtypeStruct((n, WIDTH), dt) for dt in out_dtypes],
        compiler_params=_params(1),
        name="proj_sample",
    )(x2d, g1, w_in, gains, cos, sin, pool)


def _proj_prompt(x2d, b, s, keep, g1, w_in, gains, cos, sin, pool, kgain_t, cos_t, sin_t):
    n = x2d.shape[0]
    tm = 2 * ROW_TILE
    tps = s // tm
    row, in_specs = _proj_specs(tm, tps, (g1, w_in, gains, pool))
    pos_t = pl.BlockSpec((HEAD_DIM // 2, tm), lambda i: (0, i % tps))
    in_specs = in_specs + [_const_spec(kgain_t.shape), pos_t, pos_t]
    feat = pl.BlockSpec((1, WIDTH, tm), lambda i: (i // tps, 0, i % tps))
    tail = pl.BlockSpec((1, WIDTH, keep), lambda i: (i // tps, 0, 0))
    out_specs = [row(WIDTH), feat, row(WIDTH), row(WIDTH), feat,
                 pl.BlockSpec((tm * N_BLOCKS, LANES), lambda i: (i, 0)), tail, tail]
    out_shape = [jax.ShapeDtypeStruct((n, WIDTH), BF16),
                 jax.ShapeDtypeStruct((b, WIDTH, s), BF16),
                 jax.ShapeDtypeStruct((n, WIDTH), BF16),
                 jax.ShapeDtypeStruct((n, WIDTH), BF16),
                 jax.ShapeDtypeStruct((b, WIDTH, s), F32),
                 jax.ShapeDtypeStruct((n * N_BLOCKS, LANES), F32),
                 jax.ShapeDtypeStruct((b, WIDTH, keep), F32),
                 jax.ShapeDtypeStruct((b, WIDTH, keep), F32)]
    return pl.pallas_call(
        functools.partial(_proj_kernel, tiles_per_seq=tps),
        grid=(n // tm,),
        in_specs=in_specs,
        out_specs=out_specs,
        out_shape=out_shape,
        compiler_params=_params(1),
        name="proj_prompt",
    )(x2d, g1, w_in, gains, cos, sin, pool, kgain_t, cos_t, sin_t)


def _group_bias(f_ref):
    bias = _pair_bias(f_ref, GROUP_Q)[:, :GROUP_KEYS]
    qc = (lax.broadcasted_iota(jnp.int32, bias.shape, 0) % GROUP_Q) // CHUNK
    kc = lax.broadcasted_iota(jnp.int32, bias.shape, 1) // CHUNK
    return jnp.where((kc >= qc) & (kc <= qc + BAND_PAST_CHUNKS), bias, NEG_INF)


def _band_prompt_kernel(q_ref, kt_ref, v_ref, f_ref, o_ref, bias_ref, kt_heads_ref):
    n_groups = q_ref.shape[1] // GROUP_Q

    def window(g):
        return max(0, g * GROUP_Q - BAND_PAST), (g + 1) * GROUP_Q

    for jj in range(BAND_PAIRS_PER_STEP):
        cols = slice(jj * LANES, (jj + 1) * LANES)
        bias_ref[jj] = _group_bias(f_ref.at[jj])
        kt = kt_ref[0, cols, :]
        feat = lax.broadcasted_iota(jnp.int32, kt.shape, 0)
        kt_heads_ref[jj, 0] = jnp.where(feat < HEAD_DIM, kt, jnp.zeros_like(kt))
        kt_heads_ref[jj, 1] = jnp.where(feat >= HEAD_DIM, kt, jnp.zeros_like(kt))

        def scores(g):
            lo, hi = window(g)
            q = q_ref[0, g * GROUP_Q:(g + 1) * GROUP_Q, cols]
            s = _dot(q, jnp.concatenate([kt_heads_ref[jj, 0, :, lo:hi], kt_heads_ref[jj, 1, :, lo:hi]], axis=1))
            s = jnp.concatenate([s[:, :hi - lo], s[:, hi - lo:]], axis=0)
            return s + bias_ref[jj, :, GROUP_KEYS - (hi - lo):]

        nxt = scores(0)
        for g in range(n_groups):
            s = nxt
            if g + 1 < n_groups:
                nxt = scores(g + 1)
            lo, hi = window(g)
            (p,), l = _softmax_parts([s])
            o = _dot(p.astype(BF16), v_ref[0, lo:hi, cols]) * (1.0 / l)
            o_ref[0, g * GROUP_Q:(g + 1) * GROUP_Q, cols] = _pick_halves(o).astype(o_ref.dtype)


def _band_prompt(qa, kat, va, f):
    b, s, _ = qa.shape
    pairs = BAND_PAIRS_PER_STEP
    blk = pl.BlockSpec((1, s, pairs * LANES), lambda i, j: (i, 0, j))
    blk_t = pl.BlockSpec((1, pairs * LANES, s), lambda i, j: (i, j, 0))
    return pl.pallas_call(
        _band_prompt_kernel,
        grid=(b, N_BLOCKS // pairs),
        in_specs=[blk, blk_t, blk, pl.BlockSpec((pairs, 2, BIAS_SPAN), lambda i, j: (j, 0, 0))],
        out_specs=blk,
        out_shape=jax.ShapeDtypeStruct((b, s, WIDTH), BF16),
        scratch_shapes=[pltpu.VMEM((pairs, 2 * GROUP_Q, GROUP_KEYS), F32), pltpu.VMEM((pairs, 2, LANES, s), BF16)],
        compiler_params=_params(2),
        name="band_prompt",
    )(qa, kat, va, f)


def _lambda(lam_ref, lam_init):
    e1 = jnp.exp(jnp.sum(lam_ref[0:1, :] * lam_ref[1:2, :], axis=-1, keepdims=True))
    e2 = jnp.exp(jnp.sum(lam_ref[2:3, :] * lam_ref[3:4, :], axis=-1, keepdims=True))
    return e1 - e2 + lam_init


def _diff_out(o, subln, lam_init):
    return _rms(o, subln) * (1.0 - lam_init)


def _diff_prompt_kernel(q_ref, kt_ref, v4_ref, lam_ref, subln_ref, o_ref, kbf_ref, vbf_ref, *, lam_init):
    s_len = q_ref.shape[1]
    tq = Q_BLOCK_B
    n_blocks = s_len // tq
    lam = _lambda(lam_ref, lam_init)
    row = lax.broadcasted_iota(jnp.int32, (2 * tq, tq), 0) % tq
    col = lax.broadcasted_iota(jnp.int32, (2 * tq, tq), 1)
    diag_mask = jnp.where(col // CHUNK <= row // CHUNK, 0.0, NEG_INF).astype(F32)

    for jj in range(DIFF_HEADS_PER_STEP):
        cols = slice(jj * LANES, (jj + 1) * LANES)
        head = pl.program_id(1) * DIFF_HEADS_PER_STEP + jj
        kbf_ref[jj] = kt_ref[0, cols, :].astype(BF16)
        vbf_ref[jj] = v4_ref[0, pl.ds(head, s_len, stride=N_BLOCKS), :].astype(BF16)

        def scores(i):
            r0 = i * tq
            qs = _lane_halves(q_ref[0, r0:r0 + tq, cols])
            parts = [_dot(qs, kbf_ref[jj, :, r0:r0 + tq]) + diag_mask]
            if i > 0:
                parts.append(_dot(qs, kbf_ref[jj, :, 0:r0]))
            return parts

        nxt = scores(0)
        for i in range(n_blocks):
            r0 = i * tq
            parts = nxt
            if i + 1 < n_blocks:
                nxt = scores(i + 1)
            a, inv = _diff_combine(*_softmax_parts(parts), lam, tq)
            o = _dot(a[0], vbf_ref[jj, r0:r0 + tq, :])
            if i > 0:
                o = o + _dot(a[1], vbf_ref[jj, 0:r0, :])
            o_ref[0, r0:r0 + tq, cols] = _diff_out(o * inv, subln_ref[...], lam_init).astype(o_ref.dtype)


def _diff_prompt(qb, kbt, vb4, lam_vecs, subln, lam_init):
    b, s, _ = qb.shape
    heads = DIFF_HEADS_PER_STEP
    blk = pl.BlockSpec((1, s, heads * LANES), lambda i, j: (i, 0, j))
    return pl.pallas_call(
        functools.partial(_diff_prompt_kernel, lam_init=lam_init),
        grid=(b, N_BLOCKS // heads),
        in_specs=[blk, pl.BlockSpec((1, heads * LANES, s), lambda i, j: (i, j, 0)),
                  pl.BlockSpec((1, s * N_BLOCKS, LANES), lambda i, j: (i, 0, 0)),
                  _const_spec(lam_vecs.shape), _const_spec(subln.shape)],
        out_specs=blk,
        out_shape=jax.ShapeDtypeStruct((b, s, WIDTH), BF16),
        scratch_shapes=[pltpu.VMEM((heads, LANES, s), BF16), pltpu.VMEM((heads, s, LANES), BF16)],
        compiler_params=_params(2),
        name="diff_prompt",
    )(qb, kbt, vb4, lam_vecs, subln)


def _sample_attn_kernel(qa_ref, ka_ref, va_ref, qb_ref, kb_ref, vb_ref, cakt_ref, cavt_ref, cbkt_ref, cbv4_ref,
                        f_ref, lam_ref, subln_ref, ya_ref, yb_ref, *, lam_init):
    t = qa_ref.shape[0]
    lc = cakt_ref.shape[2]
    past = cbkt_ref.shape[2]
    lam = _lambda(lam_ref, lam_init)
    blocks = [slice(j * LANES, (j + 1) * LANES) for j in range(N_BLOCKS)]
    sa = []
    for j, cols in enumerate(blocks):
        bias = _pair_bias(f_ref.at[j], t)
        qm = _lane_halves(qa_ref[:, cols])
        sa.append([_dot(qm, cakt_ref[0, cols, :].astype(BF16)) + bias[:, :lc],
                   _dot_nt(qm, ka_ref[:, cols].astype(BF16)) + bias[:, lc:lc + t]])
    sb = []
    for cols in blocks:
        qs = _lane_halves(qb_ref[:, cols])
        sb.append([_dot(qs, cbkt_ref[0, cols, :].astype(BF16)), _dot_nt(qs, kb_ref[:, cols].astype(BF16))])
    pa = [_softmax_parts(s) for s in sa]
    pb = [_diff_combine(*_softmax_parts(s), lam, t) for s in sb]
    for cols, ((pc, pn), l) in zip(blocks, pa):
        o = _dot_nt(pc.astype(BF16), cavt_ref[0, cols, :].astype(BF16)) + _dot(pn.astype(BF16),
                                                                             va_ref[:, cols].astype(BF16))
        ya_ref[:, cols] = _pick_halves(o * (1.0 / l)).astype(ya_ref.dtype)
    for j, (cols, ((ac, an), inv)) in enumerate(zip(blocks, pb)):
        vc = cbv4_ref[0, pl.ds(j, past, stride=N_BLOCKS), :].astype(BF16)
        o = (_dot(ac, vc) + _dot(an, vb_ref[:, cols].astype(BF16))) * inv
        yb_ref[:, cols] = _diff_out(o, subln_ref[...], lam_init).astype(yb_ref.dtype)


def _sample_attn(new, caches, f, lam_vecs, subln, lam_init, t):
    n = new[0].shape[0]
    b = n // t
    row = pl.BlockSpec((t, WIDTH), lambda i: (i, 0))
    cache = lambda c: pl.BlockSpec((1,) + c.shape[1:], lambda i: (i, 0, 0))
    return pl.pallas_call(
        functools.partial(_sample_attn_kernel, lam_init=lam_init),
        grid=(b,),
        in_specs=[row] * 6 + [cache(c) for c in caches]
        + [_const_spec(f.shape), _const_spec(lam_vecs.shape), _const_spec(subln.shape)],
        out_specs=[row, row],
        out_shape=[jax.ShapeDtypeStruct((n, WIDTH), BF16)] * 2,
        compiler_params=_params(1),
        name="sample_attn",
    )(*new, *caches, f, lam_vecs, subln)


def _merge_mlp_kernel(x_ref, ya_ref, yb_ref, g1_ref, wg_ref, bg_ref, wpa_ref, wpb_ref, wo_ref,
                      g2_ref, w1_ref, w2_ref, o_ref):
    x = x_ref[...]
    h = _rms(x, g1_ref[...]).astype(BF16)
    m = jax.nn.sigmoid(_dot(h, wg_ref[:, :D_MODEL]) + bg_ref[:, :D_MODEL]) * _dot(ya_ref[...], wpa_ref[...])
    m = m + jax.nn.sigmoid(_dot(h, wg_ref[:, D_MODEL:]) + bg_ref[:, D_MODEL:]) * _dot(yb_ref[...], wpb_ref[...])
    x1 = x + _dot(m.astype(BF16), wo_ref[...])
    hn = _rms(x1, g2_ref[...]).astype(BF16)
    acc = x1
    for c in range(D_FF // D_MODEL):
        cols = slice(c * D_MODEL, (c + 1) * D_MODEL)
        u = jnp.maximum(_dot(hn, w1_ref[:, cols]), 0.0)
        acc = acc + _dot((u * u).astype(BF16), w2_ref[cols, :])
    o_ref[...] = acc


def _merge_mlp(x2d, ya, yb, tm, g1, wg, bg, wpa, wpb, wo, g2, w1, w2):
    n = x2d.shape[0]
    row = lambda w: pl.BlockSpec((tm, w), lambda i: (i, 0))
    consts = (g1, wg, bg, wpa, wpb, wo, g2, w1, w2)
    return pl.pallas_call(
        _merge_mlp_kernel,
        grid=(n // tm,),
        in_specs=[row(D_MODEL), row(WIDTH), row(WIDTH)] + [_const_spec(c.shape) for c in consts],
        out_specs=row(D_MODEL),
        out_shape=jax.ShapeDtypeStruct((n, D_MODEL), F32),
        compiler_params=_params(1),
        name="merge_mlp",
    )(x2d, ya, yb, *consts)


def _rope_tables(pos):
    half = HEAD_DIM // 2
    inv_freq = ROPE_THETA ** (-jnp.arange(half, dtype=F32) / half)
    ang = pos.astype(F32)[:, None] * inv_freq[None, :]
    cos = jnp.cos(ang)
    sin = jnp.sin(ang)
    reps = LANES // HEAD_DIM
    return (jnp.tile(jnp.concatenate([cos, cos], axis=1), (1, reps)),
            jnp.tile(jnp.concatenate([-sin, sin], axis=1), (1, reps)), cos.T, sin.T)


def _band_offset_table(table):
    n_far = BAND_PAST + GROUP_Q - 1 - REL_PAST + 1
    n_fut = BIAS_SPAN - n_far - (N_REL - 1)
    h = table.shape[0]
    f = jnp.concatenate([jnp.broadcast_to(table[:, N_REL - 1:], (h, n_far)), table[:, N_REL - 2::-1],
                         jnp.broadcast_to(table[:, :1], (h, n_fut))], axis=1).astype(F32)
    return f.reshape(h // 2, 2, BIAS_SPAN)


def _pool_matrix():
    i = np.arange(MXU_DIM)
    return jnp.asarray((i[:, None] // HEAD_DIM == i[None, :] // HEAD_DIM) / HEAD_DIM, dtype=BF16)


def kernel(x_prompt, x_sample, cache_a_k, cache_a_v, cache_b_k, cache_b_v, ln1_g, w_in, qn_a, kn_a, rel_bias, qn_b, kn_b, lam_q1, lam_k1, lam_q2, lam_k2, subln_g, w_gate, b_gate, w_proj_a, w_proj_b, w_out, ln2_g, w_ff1, w_ff2):
    depth = w_in.shape[0]
    b, s, d = x_prompt.shape
    db, t, _ = x_sample.shape
    past_len = cache_b_k.shape[2]
    lc = cache_a_k.shape[2]
    keep = min(BAND_PAST, s)
    assert d == D_MODEL and s % (2 * ROW_TILE) == 0 and keep <= 2 * ROW_TILE and keep % LANES == 0
    assert lc == BAND_PAST and t <= GROUP_Q

    cos_p, sin_p, cos_pt, sin_pt = _rope_tables(jnp.arange(s))
    cos_s, sin_s = (jnp.tile(a, (db, 1)) for a in _rope_tables(past_len + jnp.arange(t))[:2])
    lane_const = lambda g: jnp.broadcast_to(g[:, None], (HEAD_DIM, LANES))
    pool = _pool_matrix()
    tile_heads = lambda g: jnp.tile(g, WIDTH // HEAD_DIM)

    xp = x_prompt.reshape(b * s, d)
    xs = x_sample.reshape(db * t, d)
    outs = [[] for _ in range(8)]
    for l in range(depth):
        lam_init = 0.8 - 0.6 * math.exp(-0.3 * l)
        g1 = ln1_g[l][None]
        g2 = ln2_g[l][None]
        gains = jnp.stack([tile_heads(qn_a[l]), tile_heads(kn_a[l]), tile_heads(qn_b[l]), tile_heads(kn_b[l])])
        lam_vecs = jnp.stack([lam_q1[l], lam_k1[l], lam_q2[l], lam_k2[l]])
        subln = subln_g[l][None]
        bias_f = _band_offset_table(rel_bias[l])
        w_in_l = w_in[l].astype(BF16)
        merge_w = (g1, w_gate[l].astype(BF16), b_gate[l][None], w_proj_a[l].astype(BF16), w_proj_b[l].astype(BF16),
                   w_out[l].astype(BF16), g2, w_ff1[l].astype(BF16), w_ff2[l].astype(BF16))

        kgain_t = jnp.stack([lane_const(kn_a[l]), lane_const(kn_b[l])])
        qa, kat, va, qb, kbt, vb4, kat_tail, vat_tail = _proj_prompt(xp, b, s, keep, g1, w_in_l, gains, cos_p, sin_p,
                                                                     pool, kgain_t, cos_pt, sin_pt)
        as3 = lambda a: a.reshape(b, s, WIDTH)
        ya = _band_prompt(as3(qa), kat, as3(va), bias_f)
        yb = _diff_prompt(as3(qb), kbt, vb4.reshape(b, s * N_BLOCKS, LANES), lam_vecs, subln, lam_init)
        xp = _merge_mlp(xp, ya.reshape(b * s, WIDTH), yb.reshape(b * s, WIDTH), 2 * ROW_TILE, *merge_w)
        token_major_a = lambda a: a.reshape(b, N_HEADS_A, HEAD_DIM, keep).transpose(0, 3, 1, 2)
        outs[0].append(token_major_a(kat_tail))
        outs[1].append(token_major_a(vat_tail))
        outs[2].append(kbt.reshape(b, N_HEADS_B, 2, HEAD_DIM, s).transpose(0, 4, 1, 2, 3))
        outs[3].append(vb4.reshape(b, s, N_HEADS_B, 2 * HEAD_DIM))

        new = _proj_sample(xs, g1, w_in_l, gains, cos_s, sin_s, pool)
        caches = (cache_a_k[l].transpose(0, 2, 3, 1).reshape(db, WIDTH, lc),
                  cache_a_v[l].transpose(0, 2, 3, 1).reshape(db, WIDTH, lc),
                  cache_b_k[l].transpose(0, 2, 3, 4, 1).reshape(db, WIDTH, past_len),
                  cache_b_v[l].reshape(db, past_len * N_BLOCKS, LANES))
        ya_s, yb_s = _sample_attn(new, caches, bias_f, lam_vecs, subln, lam_init, t)
        xs = _merge_mlp(xs, ya_s, yb_s, db * t, *merge_w)
        outs[4].append(new[1].reshape(db, t, N_HEADS_A, HEAD_DIM))
        outs[5].append(new[2].reshape(db, t, N_HEADS_A, HEAD_DIM))
        outs[6].append(new[4].reshape(db, t, N_HEADS_B, 2, HEAD_DIM))
        outs[7].append(new[5].reshape(db, t, N_HEADS_B, 2 * HEAD_DIM))

    return (xp.reshape(b, s, d), xs.reshape(db, t, d)) + tuple(jnp.stack(o) for o in outs)
```

```python
import functools
import math

import jax
import jax.numpy as jnp
import numpy as np
from jax import lax
from jax.experimental import pallas as pl
from jax.experimental.pallas import tpu as pltpu

D_MODEL = 1024
CHUNK = 64
HEAD_DIM = 64
N_HEADS_A = 8
N_HEADS_B = 4
BAND_PAST_CHUNKS = 8
BAND_PAST = BAND_PAST_CHUNKS * CHUNK
REL_FUT = CHUNK - 1
REL_PAST = 256
N_REL = REL_FUT + REL_PAST + 1
WIDTH = 512
N_GROUPS = 6
D_FF = 4 * D_MODEL
ROPE_THETA = 10000.0
EPS = 1e-6
NEG_INF = -1e30
LOG2E = math.log2(math.e)
Q_SCALE = HEAD_DIM ** -0.5 * LOG2E

LANES = 128
BF16_ROWS = 16
N_BLOCKS = WIDTH // LANES
MXU_DIM = 256
GROUP_CHUNKS = 2
GROUP_Q = GROUP_CHUNKS * CHUNK
GROUP_KEYS = BAND_PAST + GROUP_Q
BIAS_SPAN = GROUP_KEYS + GROUP_Q
BAND_PAIRS_PER_STEP = 2
Q_BLOCK_B = 256
DIFF_HEADS_PER_STEP = 2
ROW_TILE = 512
VMEM_LIMIT = 60 * 1024 * 1024

BF16 = jnp.bfloat16
F32 = jnp.float32


def _dot(a, b):
    return jnp.dot(a, b, preferred_element_type=F32)


def _dot_nt(a, b):
    return lax.dot_general(a, b, (((1,), (1,)), ((), ())), preferred_element_type=F32)


def _rms(x, g):
    return x * lax.rsqrt(jnp.mean(x * x, axis=-1, keepdims=True) + EPS) * g


def _const_spec(shape):
    nd = len(shape)
    return pl.BlockSpec(shape, lambda *_: (0,) * nd, pipeline_mode=pl.Buffered(1))


def _params(n_grid_axes):
    return pltpu.CompilerParams(dimension_semantics=("arbitrary",) * n_grid_axes, vmem_limit_bytes=VMEM_LIMIT)


def _lane_halves(q):
    lane = lax.broadcasted_iota(jnp.int32, q.shape, 1)
    zero = jnp.zeros_like(q)
    return jnp.concatenate([jnp.where(lane < HEAD_DIM, q, zero), jnp.where(lane >= HEAD_DIM, q, zero)], axis=0)


def _pick_halves(o):
    m = o.shape[0] // 2
    lane = lax.broadcasted_iota(jnp.int32, (m, LANES), 1)
    return jnp.where(lane < HEAD_DIM, o[:m], o[m:])


def _softmax_parts(parts):
    m = functools.reduce(jnp.maximum, [jnp.max(s, axis=-1, keepdims=True) for s in parts])
    ps = [jnp.exp2(s - m) for s in parts]
    l = functools.reduce(lambda a, b: a + b, [jnp.sum(p, axis=-1, keepdims=True) for p in ps])
    return ps, l


def _diff_combine(ps, l, lam, t):
    c = lam * l[:t] / l[t:]
    return [(p[:t] - p[t:] * c).astype(BF16) for p in ps], 1.0 / l[:t]


def _pair_bias(f_ref, rows):
    def one(hd):
        x = jnp.broadcast_to(f_ref[hd:hd + 1, :], (rows, BIAS_SPAN))
        return pltpu.roll(x, BIAS_SPAN - GROUP_Q + 1, 1, stride=1, stride_axis=0)

    return jnp.concatenate([one(0), one(1)], axis=0) * LOG2E


def _proj_kernel(x_ref, g1_ref, w_ref, gains_ref, cos_ref, sin_ref, pool_ref, *refs, tiles_per_seq):
    h = _rms(x_ref[...], g1_ref[...]).astype(BF16)
    cos = jnp.concatenate([cos_ref[...]] * N_BLOCKS, axis=1)
    sin = jnp.concatenate([sin_ref[...]] * N_BLOCKS, axis=1)
    pool = pool_ref[...]

    def group(i):
        return _dot(h, w_ref[:, i * WIDTH:(i + 1) * WIDTH])

    def head_norm(z, i):
        zz = (z * z).astype(BF16)
        ms = jnp.concatenate([_dot(zz[:, k * MXU_DIM:(k + 1) * MXU_DIM], pool) for k in range(WIDTH // MXU_DIM)],
                             axis=1)
        return z * lax.rsqrt(ms + EPS) * gains_ref[i:i + 1, :]

    def rope(z):
        lane = lax.broadcasted_iota(jnp.int32, z.shape, 1)
        half = HEAD_DIM // 2
        partner = jnp.where(lane % HEAD_DIM < half, pltpu.roll(z, WIDTH - half, 1), pltpu.roll(z, half, 1))
        return z * cos + partner * sin

    if tiles_per_seq == 0:
        qa_ref, ka_ref, va_ref, qb_ref, kb_ref, vb_ref = refs
        qa_ref[...] = (head_norm(group(0), 0) * Q_SCALE).astype(BF16)
        ka_ref[...] = head_norm(group(1), 1)
        va_ref[...] = group(2)
        qb_ref[...] = (rope(head_norm(group(3), 2)) * Q_SCALE).astype(BF16)
        kb_ref[...] = rope(head_norm(group(4), 3))
        vb_ref[...] = group(5)
        return

    kgain_ref, cos_t_ref, sin_t_ref = refs[:3]
    n_cast = (len(refs) - 3 - 8) // 2
    qa_ref, kat_ref, va_ref, qb_ref, kbt_ref, vb4_ref, kat_tail_ref, vat_tail_ref = refs[3 + n_cast:11 + n_cast]
    for src_ref, dst_ref in zip(refs[3:3 + n_cast], refs[11 + n_cast:]):
        dst_ref[...] = src_ref[...].astype(BF16)
    tm = x_ref.shape[0]
    n_heads = WIDTH // HEAD_DIM

    def head_norm_t(zt, i):
        z3 = zt.reshape(n_heads, HEAD_DIM, tm)
        ms = jnp.mean(z3 * z3, axis=1, keepdims=True)
        gain = jnp.concatenate([kgain_ref[i]] * (tm // LANES), axis=1)
        return z3 * lax.rsqrt(ms + EPS) * gain[None]

    def rope_t(z3):
        half = HEAD_DIM // 2
        cos_t = cos_t_ref[...][None]
        sin_t = sin_t_ref[...][None]
        x1, x2 = z3[:, :half], z3[:, half:]
        return jnp.concatenate([x1 * cos_t - x2 * sin_t, x2 * cos_t + x1 * sin_t], axis=1)

    kbt_ref[0] = rope_t(head_norm_t(group(4).T, 1)).reshape(WIDTH, tm)
    kat = head_norm_t(group(1).T, 0).reshape(WIDTH, tm)
    kat_ref[0] = kat.astype(BF16)
    va = group(2)
    va_ref[...] = va.astype(BF16)
    qb_ref[...] = (rope(head_norm(group(3), 2)) * Q_SCALE).astype(BF16)
    qa_ref[...] = (head_norm(group(0), 0) * Q_SCALE).astype(BF16)
    vb = group(5)
    for hd in range(N_BLOCKS):
        vb4_ref[pl.ds(hd, tm, stride=N_BLOCKS), :] = vb[:, hd * LANES:(hd + 1) * LANES]

    @pl.when(pl.program_id(0) % tiles_per_seq == tiles_per_seq - 1)
    def _():
        keep = kat_tail_ref.shape[2]
        kat_tail_ref[0] = kat[:, tm - keep:]
        vat_tail_ref[0] = va[tm - keep:].T


def _proj_specs(tm, n_pos_blocks, consts):
    row = lambda w: pl.BlockSpec((tm, w), lambda i: (i, 0))
    pos = pl.BlockSpec((tm, LANES), lambda i: (i % n_pos_blocks, 0))
    g1, w_in, gains, pool = consts
    return row, [row(D_MODEL), _const_spec(g1.shape), _const_spec(w_in.shape), _const_spec(gains.shape),
                 pos, pos, _const_spec(pool.shape)]


def _proj_sample(x2d, g1, w_in, gains, cos, sin, pool):
    n = x2d.shape[0]
    row, in_specs = _proj_specs(n, 1, (g1, w_in, gains, pool))
    out_dtypes = (BF16, F32, F32, BF16, F32, F32)
    return pl.pallas_call(
        functools.partial(_proj_kernel, tiles_per_seq=0),
        grid=(1,),
        in_specs=in_specs,
        out_specs=[row(WIDTH)] * N_GROUPS,
        out_shape=[jax.ShapeDtypeStruct((n, WIDTH), dt) for dt in out_dtypes],
        compiler_params=_params(1),
        name="proj_sample",
    )(x2d, g1, w_in, gains, cos, sin, pool)


def _proj_prompt(x2d, b, s, keep, g1, w_in, gains, cos, sin, pool, kgain_t, cos_t, sin_t, to_cast):
    n = x2d.shape[0]
    tm = 2 * ROW_TILE
    tps = s // tm
    steps = n // tm
    row, in_specs = _proj_specs(tm, tps, (g1, w_in, gains, pool))
    pos_t = pl.BlockSpec((HEAD_DIM // 2, tm), lambda i: (0, i % tps))
    assert all(w.shape[0] % (BF16_ROWS * steps) == 0 for w in to_cast)
    cast_specs = [pl.BlockSpec((w.shape[0] // steps, w.shape[1]), lambda i: (i, 0)) for w in to_cast]
    in_specs = in_specs + [_const_spec(kgain_t.shape), pos_t, pos_t] + cast_specs
    feat = pl.BlockSpec((1, WIDTH, tm), lambda i: (i // tps, 0, i % tps))
    tail = pl.BlockSpec((1, WIDTH, keep), lambda i: (i // tps, 0, 0))
    out_specs = [row(WIDTH), feat, row(WIDTH), row(WIDTH), feat,
                 pl.BlockSpec((tm * N_BLOCKS, LANES), lambda i: (i, 0)), tail, tail] + cast_specs
    out_shape = [jax.ShapeDtypeStruct((n, WIDTH), BF16),
                 jax.ShapeDtypeStruct((b, WIDTH, s), BF16),
                 jax.ShapeDtypeStruct((n, WIDTH), BF16),
                 jax.ShapeDtypeStruct((n, WIDTH), BF16),
                 jax.ShapeDtypeStruct((b, WIDTH, s), F32),
                 jax.ShapeDtypeStruct((n * N_BLOCKS, LANES), F32),
                 jax.ShapeDtypeStruct((b, WIDTH, keep), F32),
                 jax.ShapeDtypeStruct((b, WIDTH, keep), F32)]
    out_shape += [jax.ShapeDtypeStruct(w.shape, BF16) for w in to_cast]
    return pl.pallas_call(
        functools.partial(_proj_kernel, tiles_per_seq=tps),
        grid=(steps,),
        in_specs=in_specs,
        out_specs=out_specs,
        out_shape=out_shape,
        compiler_params=_params(1),
        name="proj_prompt",
    )(x2d, g1, w_in, gains, cos, sin, pool, kgain_t, cos_t, sin_t, *to_cast)


def _group_bias(f_ref):
    bias = _pair_bias(f_ref, GROUP_Q)[:, :GROUP_KEYS]
    qc = (lax.broadcasted_iota(jnp.int32, bias.shape, 0) % GROUP_Q) // CHUNK
    kc = lax.broadcasted_iota(jnp.int32, bias.shape, 1) // CHUNK
    return jnp.where((kc >= qc) & (kc <= qc + BAND_PAST_CHUNKS), bias, NEG_INF)


def _band_prompt_kernel(q_ref, kt_ref, v_ref, f_ref, o_ref, bias_ref, kt_heads_ref):
    n_groups = q_ref.shape[1] // GROUP_Q

    def window(g):
        return max(0, g * GROUP_Q - BAND_PAST), (g + 1) * GROUP_Q

    for jj in range(BAND_PAIRS_PER_STEP):
        cols = slice(jj * LANES, (jj + 1) * LANES)
        bias_ref[jj] = _group_bias(f_ref.at[jj])
        kt = kt_ref[0, cols, :]
        feat = lax.broadcasted_iota(jnp.int32, kt.shape, 0)
        kt_heads_ref[jj, 0] = jnp.where(feat < HEAD_DIM, kt, jnp.zeros_like(kt))
        kt_heads_ref[jj, 1] = jnp.where(feat >= HEAD_DIM, kt, jnp.zeros_like(kt))

        def scores(g):
            lo, hi = window(g)
            q = q_ref[0, g * GROUP_Q:(g + 1) * GROUP_Q, cols]
            s = _dot(q, jnp.concatenate([kt_heads_ref[jj, 0, :, lo:hi], kt_heads_ref[jj, 1, :, lo:hi]], axis=1))
            s = jnp.concatenate([s[:, :hi - lo], s[:, hi - lo:]], axis=0)
            return s + bias_ref[jj, :, GROUP_KEYS - (hi - lo):]

        nxt = scores(0)
        for g in range(n_groups):
            s = nxt
            if g + 1 < n_groups:
                nxt = scores(g + 1)
            lo, hi = window(g)
            (p,), l = _softmax_parts([s])
            o = _dot(p.astype(BF16), v_ref[0, lo:hi, cols]) * (1.0 / l)
            o_ref[0, g * GROUP_Q:(g + 1) * GROUP_Q, cols] = _pick_halves(o).astype(o_ref.dtype)


def _band_prompt(qa, kat, va, f):
    b, s, _ = qa.shape
    pairs = BAND_PAIRS_PER_STEP
    blk = pl.BlockSpec((1, s, pairs * LANES), lambda i, j: (i, 0, j))
    blk_t = pl.BlockSpec((1, pairs * LANES, s), lambda i, j: (i, j, 0))
    return pl.pallas_call(
        _band_prompt_kernel,
        grid=(b, N_BLOCKS // pairs),
        in_specs=[blk, blk_t, blk, pl.BlockSpec((pairs, 2, BIAS_SPAN), lambda i, j: (j, 0, 0))],
        out_specs=blk,
        out_shape=jax.ShapeDtypeStruct((b, s, WIDTH), BF16),
        scratch_shapes=[pltpu.VMEM((pairs, 2 * GROUP_Q, GROUP_KEYS), F32), pltpu.VMEM((pairs, 2, LANES, s), BF16)],
        compiler_params=_params(2),
        name="band_prompt",
    )(qa, kat, va, f)


def _lambda(lam_ref, lam_init):
    e1 = jnp.exp(jnp.sum(lam_ref[0:1, :] * lam_ref[1:2, :], axis=-1, keepdims=True))
    e2 = jnp.exp(jnp.sum(lam_ref[2:3, :] * lam_ref[3:4, :], axis=-1, keepdims=True))
    return e1 - e2 + lam_init


def _diff_out(o, subln, lam_init):
    return _rms(o, subln) * (1.0 - lam_init)


def _diff_prompt_kernel(q_ref, kt_ref, v4_ref, lam_ref, subln_ref, o_ref, kbf_ref, vbf_ref, *, lam_init):
    s_len = q_ref.shape[1]
    tq = Q_BLOCK_B
    n_blocks = s_len // tq
    lam = _lambda(lam_ref, lam_init)
    row = lax.broadcasted_iota(jnp.int32, (2 * tq, tq), 0) % tq
    col = lax.broadcasted_iota(jnp.int32, (2 * tq, tq), 1)
    diag_mask = jnp.where(col // CHUNK <= row // CHUNK, 0.0, NEG_INF).astype(F32)

    for jj in range(DIFF_HEADS_PER_STEP):
        cols = slice(jj * LANES, (jj + 1) * LANES)
        head = pl.program_id(1) * DIFF_HEADS_PER_STEP + jj
        kbf_ref[jj] = kt_ref[0, cols, :].astype(BF16)
        vbf_ref[jj] = v4_ref[0, pl.ds(head, s_len, stride=N_BLOCKS), :].astype(BF16)

        def scores(i):
            r0 = i * tq
            qs = _lane_halves(q_ref[0, r0:r0 + tq, cols])
            parts = [_dot(qs, kbf_ref[jj, :, r0:r0 + tq]) + diag_mask]
            if i > 0:
                parts.append(_dot(qs, kbf_ref[jj, :, 0:r0]))
            return parts

        nxt = scores(0)
        for i in range(n_blocks):
            r0 = i * tq
            parts = nxt
            if i + 1 < n_blocks:
                nxt = scores(i + 1)
            a, inv = _diff_combine(*_softmax_parts(parts), lam, tq)
            o = _dot(a[0], vbf_ref[jj, r0:r0 + tq, :])
            if i > 0:
                o = o + _dot(a[1], vbf_ref[jj, 0:r0, :])
            o_ref[0, r0:r0 + tq, cols] = _diff_out(o * inv, subln_ref[...], lam_init).astype(o_ref.dtype)


def _diff_prompt(qb, kbt, vb4, lam_vecs, subln, lam_init):
    b, s, _ = qb.shape
    heads = DIFF_HEADS_PER_STEP
    blk = pl.BlockSpec((1, s, heads * LANES), lambda i, j: (i, 0, j))
    return pl.pallas_call(
        functools.partial(_diff_prompt_kernel, lam_init=lam_init),
        grid=(b, N_BLOCKS // heads),
        in_specs=[blk, pl.BlockSpec((1, heads * LANES, s), lambda i, j: (i, j, 0)),
                  pl.BlockSpec((1, s * N_BLOCKS, LANES), lambda i, j: (i, 0, 0)),
                  _const_spec(lam_vecs.shape), _const_spec(subln.shape)],
        out_specs=blk,
        out_shape=jax.ShapeDtypeStruct((b, s, WIDTH), BF16),
        scratch_shapes=[pltpu.VMEM((heads, LANES, s), BF16), pltpu.VMEM((heads, s, LANES), BF16)],
        compiler_params=_params(2),
        name="diff_prompt",
    )(qb, kbt, vb4, lam_vecs, subln)


def _sample_attn_kernel(qa_ref, ka_ref, va_ref, qb_ref, kb_ref, vb_ref, cakt_ref, cavt_ref, cbkt_ref, cbv4_ref,
                        f_ref, lam_ref, subln_ref, ya_ref, yb_ref, *, lam_init):
    t = qa_ref.shape[0]
    lc = cakt_ref.shape[2]
    past = cbkt_ref.shape[2]
    lam = _lambda(lam_ref, lam_init)
    blocks = [slice(j * LANES, (j + 1) * LANES) for j in range(N_BLOCKS)]
    sa = []
    for j, cols in enumerate(blocks):
        bias = _pair_bias(f_ref.at[j], t)
        qm = _lane_halves(qa_ref[:, cols])
        sa.append([_dot(qm, cakt_ref[0, cols, :].astype(BF16)) + bias[:, :lc],
                   _dot_nt(qm, ka_ref[:, cols].astype(BF16)) + bias[:, lc:lc + t]])
    sb = []
    for cols in blocks:
        qs = _lane_halves(qb_ref[:, cols])
        sb.append([_dot(qs, cbkt_ref[0, cols, :].astype(BF16)), _dot_nt(qs, kb_ref[:, cols].astype(BF16))])
    pa = [_softmax_parts(s) for s in sa]
    pb = [_diff_combine(*_softmax_parts(s), lam, t) for s in sb]
    for cols, ((pc, pn), l) in zip(blocks, pa):
        o = _dot_nt(pc.astype(BF16), cavt_ref[0, cols, :].astype(BF16)) + _dot(pn.astype(BF16),
                                                                             va_ref[:, cols].astype(BF16))
        ya_ref[:, cols] = _pick_halves(o * (1.0 / l)).astype(ya_ref.dtype)
    for j, (cols, ((ac, an), inv)) in enumerate(zip(blocks, pb)):
        vc = cbv4_ref[0, pl.ds(j, past, stride=N_BLOCKS), :].astype(BF16)
        o = (_dot(ac, vc) + _dot(an, vb_ref[:, cols].astype(BF16))) * inv
        yb_ref[:, cols] = _diff_out(o, subln_ref[...], lam_init).astype(yb_ref.dtype)


def _sample_attn(new, caches, f, lam_vecs, subln, lam_init, t):
    n = new[0].shape[0]
    b = n // t
    row = pl.BlockSpec((t, WIDTH), lambda i: (i, 0))
    cache = lambda c: pl.BlockSpec((1,) + c.shape[1:], lambda i: (i, 0, 0))
    return pl.pallas_call(
        functools.partial(_sample_attn_kernel, lam_init=lam_init),
        grid=(b,),
        in_specs=[row] * 6 + [cache(c) for c in caches]
        + [_const_spec(f.shape), _const_spec(lam_vecs.shape), _const_spec(subln.shape)],
        out_specs=[row, row],
        out_shape=[jax.ShapeDtypeStruct((n, WIDTH), BF16)] * 2,
        compiler_params=_params(1),
        name="sample_attn",
    )(*new, *caches, f, lam_vecs, subln)


def _merge_mlp_kernel(x_ref, ya_ref, yb_ref, g1_ref, wg_ref, bg_ref, wpa_ref, wpb_ref, wo_ref,
                      g2_ref, w1_ref, w2_ref, o_ref):
    x = x_ref[...]
    h = _rms(x, g1_ref[...]).astype(BF16)
    m = jax.nn.sigmoid(_dot(h, wg_ref[:, :D_MODEL]) + bg_ref[:, :D_MODEL]) * _dot(ya_ref[...], wpa_ref[...])
    m = m + jax.nn.sigmoid(_dot(h, wg_ref[:, D_MODEL:]) + bg_ref[:, D_MODEL:]) * _dot(yb_ref[...], wpb_ref[...])
    x1 = x + _dot(m.astype(BF16), wo_ref[...])
    hn = _rms(x1, g2_ref[...]).astype(BF16)
    acc = x1
    for c in range(D_FF // D_MODEL):
        cols = slice(c * D_MODEL, (c + 1) * D_MODEL)
        u = jnp.maximum(_dot(hn, w1_ref[:, cols]), 0.0)
        acc = acc + _dot((u * u).astype(BF16), w2_ref[cols, :])
    o_ref[...] = acc


def _merge_mlp(x2d, ya, yb, tm, g1, wg, bg, wpa, wpb, wo, g2, w1, w2):
    n = x2d.shape[0]
    row = lambda w: pl.BlockSpec((tm, w), lambda i: (i, 0))
    consts = (g1, wg, bg, wpa, wpb, wo, g2, w1, w2)
    return pl.pallas_call(
        _merge_mlp_kernel,
        grid=(n // tm,),
        in_specs=[row(D_MODEL), row(WIDTH), row(WIDTH)] + [_const_spec(c.shape) for c in consts],
        out_specs=row(D_MODEL),
        out_shape=jax.ShapeDtypeStruct((n, D_MODEL), F32),
        compiler_params=_params(1),
        name="merge_mlp",
    )(x2d, ya, yb, *consts)


def _rope_tables(pos):
    half = HEAD_DIM // 2
    inv_freq = ROPE_THETA ** (-jnp.arange(half, dtype=F32) / half)
    ang = pos.astype(F32)[:, None] * inv_freq[None, :]
    cos = jnp.cos(ang)
    sin = jnp.sin(ang)
    reps = LANES // HEAD_DIM
    return (jnp.tile(jnp.concatenate([cos, cos], axis=1), (1, reps)),
            jnp.tile(jnp.concatenate([-sin, sin], axis=1), (1, reps)), cos.T, sin.T)


def _band_offset_table(table):
    n_far = BAND_PAST + GROUP_Q - 1 - REL_PAST + 1
    n_fut = BIAS_SPAN - n_far - (N_REL - 1)
    h = table.shape[0]
    f = jnp.concatenate([jnp.broadcast_to(table[:, N_REL - 1:], (h, n_far)), table[:, N_REL - 2::-1],
                         jnp.broadcast_to(table[:, :1], (h, n_fut))], axis=1).astype(F32)
    return f.reshape(h // 2, 2, BIAS_SPAN)


def _pool_matrix():
    i = np.arange(MXU_DIM)
    return jnp.asarray((i[:, None] // HEAD_DIM == i[None, :] // HEAD_DIM) / HEAD_DIM, dtype=BF16)


def kernel(x_prompt, x_sample, cache_a_k, cache_a_v, cache_b_k, cache_b_v, ln1_g, w_in, qn_a, kn_a, rel_bias, qn_b, kn_b, lam_q1, lam_k1, lam_q2, lam_k2, subln_g, w_gate, b_gate, w_proj_a, w_proj_b, w_out, ln2_g, w_ff1, w_ff2):
    depth = w_in.shape[0]
    b, s, d = x_prompt.shape
    db, t, _ = x_sample.shape
    past_len = cache_b_k.shape[2]
    lc = cache_a_k.shape[2]
    keep = min(BAND_PAST, s)
    assert d == D_MODEL and s % (2 * ROW_TILE) == 0 and keep <= 2 * ROW_TILE and keep % LANES == 0
    assert lc == BAND_PAST and t <= GROUP_Q

    cos_p, sin_p, cos_pt, sin_pt = _rope_tables(jnp.arange(s))
    cos_s, sin_s = (jnp.tile(a, (db, 1)) for a in _rope_tables(past_len + jnp.arange(t))[:2])
    lane_const = lambda g: jnp.broadcast_to(g[:, None], (HEAD_DIM, LANES))
    pool = _pool_matrix()
    tile_heads = lambda g: jnp.tile(g, WIDTH // HEAD_DIM)

    xp = x_prompt.reshape(b * s, d)
    xs = x_sample.reshape(db * t, d)
    outs = [[] for _ in range(8)]
    for l in range(depth):
        lam_init = 0.8 - 0.6 * math.exp(-0.3 * l)
        g1 = ln1_g[l][None]
        g2 = ln2_g[l][None]
        gains = jnp.stack([tile_heads(qn_a[l]), tile_heads(kn_a[l]), tile_heads(qn_b[l]), tile_heads(kn_b[l])])
        lam_vecs = jnp.stack([lam_q1[l], lam_k1[l], lam_q2[l], lam_k2[l]])
        subln = subln_g[l][None]
        bias_f = _band_offset_table(rel_bias[l])
        w_in_l = w_in[l].astype(BF16)

        kgain_t = jnp.stack([lane_const(kn_a[l]), lane_const(kn_b[l])])
        later_w = (w_gate[l], w_proj_a[l], w_proj_b[l], w_out[l], w_ff1[l], w_ff2[l])
        proj_out = _proj_prompt(xp, b, s, keep, g1, w_in_l, gains, cos_p, sin_p, pool, kgain_t, cos_pt, sin_pt, later_w)
        qa, kat, va, qb, kbt, vb4, kat_tail, vat_tail = proj_out[:8]
        wg, wpa, wpb, wo, w1, w2 = proj_out[8:]
        merge_w = (g1, wg, b_gate[l][None], wpa, wpb, wo, g2, w1, w2)
        as3 = lambda a: a.reshape(b, s, WIDTH)
        ya = _band_prompt(as3(qa), kat, as3(va), bias_f)
        yb = _diff_prompt(as3(qb), kbt, vb4.reshape(b, s * N_BLOCKS, LANES), lam_vecs, subln, lam_init)
        xp = _merge_mlp(xp, ya.reshape(b * s, WIDTH), yb.reshape(b * s, WIDTH), 2 * ROW_TILE, *merge_w)
        token_major_a = lambda a: a.reshape(b, N_HEADS_A, HEAD_DIM, keep).transpose(0, 3, 1, 2)
        outs[0].append(token_major_a(kat_tail))
        outs[1].append(token_major_a(vat_tail))
        outs[2].append(kbt.reshape(b, N_HEADS_B, 2, HEAD_DIM, s).transpose(0, 4, 1, 2, 3))
        outs[3].append(vb4.reshape(b, s, N_HEADS_B, 2 * HEAD_DIM))

        new = _proj_sample(xs, g1, w_in_l, gains, cos_s, sin_s, pool)
        caches = (cache_a_k[l].transpose(0, 2, 3, 1).reshape(db, WIDTH, lc),
                  cache_a_v[l].transpose(0, 2, 3, 1).reshape(db, WIDTH, lc),
                  cache_b_k[l].transpose(0, 2, 3, 4, 1).reshape(db, WIDTH, past_len),
                  cache_b_v[l].reshape(db, past_len * N_BLOCKS, LANES))
        ya_s, yb_s = _sample_attn(new, caches, bias_f, lam_vecs, subln, lam_init, t)
        xs = _merge_mlp(xs, ya_s, yb_s, db * t, *merge_w)
        outs[4].append(new[1].reshape(db, t, N_HEADS_A, HEAD_DIM))
        outs[5].append(new[2].reshape(db, t, N_HEADS_A, HEAD_DIM))
        outs[6].append(new[4].reshape(db, t, N_HEADS_B, 2, HEAD_DIM))
        outs[7].append(new[5].reshape(db, t, N_HEADS_B, 2 * HEAD_DIM))

    return (xp.reshape(b, s, d), xs.reshape(db, t, d)) + tuple(jnp.stack(o) for o in outs)
```

```python
import functools
import math

import jax
import jax.numpy as jnp
import numpy as np
from jax import lax
from jax.experimental import pallas as pl
from jax.experimental.pallas import tpu as pltpu

D_MODEL = 1024
CHUNK = 64
HEAD_DIM = 64
N_HEADS_A = 8
N_HEADS_B = 4
BAND_PAST_CHUNKS = 8
BAND_PAST = BAND_PAST_CHUNKS * CHUNK
REL_FUT = CHUNK - 1
REL_PAST = 256
N_REL = REL_FUT + REL_PAST + 1
WIDTH = 512
N_GROUPS = 6
D_FF = 4 * D_MODEL
ROPE_THETA = 10000.0
EPS = 1e-6
NEG_INF = -1e30
LOG2E = math.log2(math.e)
Q_SCALE = HEAD_DIM ** -0.5 * LOG2E

LANES = 128
BF16_ROWS = 16
N_BLOCKS = WIDTH // LANES
MXU_DIM = 256
GROUP_CHUNKS = 2
GROUP_Q = GROUP_CHUNKS * CHUNK
GROUP_KEYS = BAND_PAST + GROUP_Q
BIAS_SPAN = GROUP_KEYS + GROUP_Q
BAND_PAIRS_PER_STEP = 2
Q_BLOCK_B = 256
DIFF_HEADS_PER_STEP = 2
ROW_TILE = 512
VMEM_LIMIT = 60 * 1024 * 1024

BF16 = jnp.bfloat16
F32 = jnp.float32


def _dot(a, b):
    return jnp.dot(a, b, preferred_element_type=F32)


def _dot_nt(a, b):
    return lax.dot_general(a, b, (((1,), (1,)), ((), ())), preferred_element_type=F32)


def _rms(x, g):
    return x * lax.rsqrt(jnp.mean(x * x, axis=-1, keepdims=True) + EPS) * g


def _const_spec(shape):
    nd = len(shape)
    return pl.BlockSpec(shape, lambda *_: (0,) * nd, pipeline_mode=pl.Buffered(1))


def _params(n_grid_axes):
    return pltpu.CompilerParams(dimension_semantics=("arbitrary",) * n_grid_axes, vmem_limit_bytes=VMEM_LIMIT)


def _lane_halves(q):
    lane = lax.broadcasted_iota(jnp.int32, q.shape, 1)
    zero = jnp.zeros_like(q)
    return jnp.concatenate([jnp.where(lane < HEAD_DIM, q, zero), jnp.where(lane >= HEAD_DIM, q, zero)], axis=0)


def _pick_halves(o):
    m = o.shape[0] // 2
    lane = lax.broadcasted_iota(jnp.int32, (m, LANES), 1)
    return jnp.where(lane < HEAD_DIM, o[:m], o[m:])


def _softmax_parts(parts):
    m = functools.reduce(jnp.maximum, [jnp.max(s, axis=-1, keepdims=True) for s in parts])
    ps = [jnp.exp2(s - m) for s in parts]
    l = functools.reduce(lambda a, b: a + b, [jnp.sum(p, axis=-1, keepdims=True) for p in ps])
    return ps, l


def _diff_combine(ps, l, lam, t):
    c = lam * l[:t] / l[t:]
    return [(p[:t] - p[t:] * c).astype(BF16) for p in ps], 1.0 / l[:t]


def _pair_bias(f_ref, rows):
    def one(hd):
        x = jnp.broadcast_to(f_ref[hd:hd + 1, :], (rows, BIAS_SPAN))
        return pltpu.roll(x, BIAS_SPAN - GROUP_Q + 1, 1, stride=1, stride_axis=0)

    return jnp.concatenate([one(0), one(1)], axis=0) * LOG2E


def _proj_kernel(x_ref, g1_ref, w_ref, gains_ref, cos_ref, sin_ref, pool_ref, *refs, tiles_per_seq):
    h = _rms(x_ref[...], g1_ref[...]).astype(BF16)
    cos = jnp.concatenate([cos_ref[...]] * N_BLOCKS, axis=1)
    sin = jnp.concatenate([sin_ref[...]] * N_BLOCKS, axis=1)
    pool = pool_ref[...]

    def group(i):
        return _dot(h, w_ref[:, i * WIDTH:(i + 1) * WIDTH])

    def head_norm(z, i):
        zz = (z * z).astype(BF16)
        ms = jnp.concatenate([_dot(zz[:, k * MXU_DIM:(k + 1) * MXU_DIM], pool) for k in range(WIDTH // MXU_DIM)],
                             axis=1)
        return z * lax.rsqrt(ms + EPS) * gains_ref[i:i + 1, :]

    def rope(z):
        lane = lax.broadcasted_iota(jnp.int32, z.shape, 1)
        half = HEAD_DIM // 2
        partner = jnp.where(lane % HEAD_DIM < half, pltpu.roll(z, WIDTH - half, 1), pltpu.roll(z, half, 1))
        return z * cos + partner * sin

    if tiles_per_seq == 0:
        qa_ref, ka_ref, va_ref, qb_ref, kb_ref, vb_ref = refs
        qa_ref[...] = (head_norm(group(0), 0) * Q_SCALE).astype(BF16)
        ka_ref[...] = head_norm(group(1), 1)
        va_ref[...] = group(2)
        qb_ref[...] = (rope(head_norm(group(3), 2)) * Q_SCALE).astype(BF16)
        kb_ref[...] = rope(head_norm(group(4), 3))
        vb_ref[...] = group(5)
        return

    kgain_ref, cos_t_ref, sin_t_ref = refs[:3]
    n_cast = (len(refs) - 3 - 8) // 2
    qa_ref, kat_ref, va_ref, qb_ref, kbt_ref, vb4_ref, kat_tail_ref, vat_tail_ref = refs[3 + n_cast:11 + n_cast]
    for src_ref, dst_ref in zip(refs[3:3 + n_cast], refs[11 + n_cast:]):
        dst_ref[...] = src_ref[...].astype(BF16)
    tm = x_ref.shape[0]
    n_heads = WIDTH // HEAD_DIM

    def head_norm_t(zt, i):
        z3 = zt.reshape(n_heads, HEAD_DIM, tm)
        ms = jnp.mean(z3 * z3, axis=1, keepdims=True)
        gain = jnp.concatenate([kgain_ref[i]] * (tm // LANES), axis=1)
        return z3 * lax.rsqrt(ms + EPS) * gain[None]

    def rope_t(z3):
        half = HEAD_DIM // 2
        cos_t = cos_t_ref[...][None]
        sin_t = sin_t_ref[...][None]
        x1, x2 = z3[:, :half], z3[:, half:]
        return jnp.concatenate([x1 * cos_t - x2 * sin_t, x2 * cos_t + x1 * sin_t], axis=1)

    kbt_ref[0] = rope_t(head_norm_t(group(4).T, 1)).reshape(WIDTH, tm)
    kat = head_norm_t(group(1).T, 0).reshape(WIDTH, tm)
    kat_ref[0] = kat.astype(BF16)
    va = group(2)
    va_ref[...] = va.astype(BF16)
    qb_ref[...] = (rope(head_norm(group(3), 2)) * Q_SCALE).astype(BF16)
    qa_ref[...] = (head_norm(group(0), 0) * Q_SCALE).astype(BF16)
    vb = group(5)
    for hd in range(N_BLOCKS):
        vb4_ref[pl.ds(hd, tm, stride=N_BLOCKS), :] = vb[:, hd * LANES:(hd + 1) * LANES]

    @pl.when(pl.program_id(0) % tiles_per_seq == tiles_per_seq - 1)
    def _():
        keep = kat_tail_ref.shape[2]
        kat_tail_ref[0] = kat[:, tm - keep:]
        vat_tail_ref[0] = va[tm - keep:].T


def _proj_specs(tm, n_pos_blocks, consts):
    row = lambda w: pl.BlockSpec((tm, w), lambda i: (i, 0))
    pos = pl.BlockSpec((tm, LANES), lambda i: (i % n_pos_blocks, 0))
    g1, w_in, gains, pool = consts
    return row, [row(D_MODEL), _const_spec(g1.shape), _const_spec(w_in.shape), _const_spec(gains.shape),
                 pos, pos, _const_spec(pool.shape)]


def _proj_sample(x2d, g1, w_in, gains, cos, sin, pool):
    n = x2d.shape[0]
    row, in_specs = _proj_specs(n, 1, (g1, w_in, gains, pool))
    out_dtypes = (BF16, F32, F32, BF16, F32, F32)
    return pl.pallas_call(
        functools.partial(_proj_kernel, tiles_per_seq=0),
        grid=(1,),
        in_specs=in_specs,
        out_specs=[row(WIDTH)] * N_GROUPS,
        out_shape=[jax.ShapeDtypeStruct((n, WIDTH), dt) for dt in out_dtypes],
        compiler_params=_params(1),
        name="proj_sample",
    )(x2d, g1, w_in, gains, cos, sin, pool)


def _proj_prompt(x2d, b, s, keep, g1, w_in, gains, cos, sin, pool, kgain_t, cos_t, sin_t, to_cast):
    n = x2d.shape[0]
    tm = 2 * ROW_TILE
    tps = s // tm
    steps = n // tm
    row, in_specs = _proj_specs(tm, tps, (g1, w_in, gains, pool))
    pos_t = pl.BlockSpec((HEAD_DIM // 2, tm), lambda i: (0, i % tps))
    assert all(w.shape[0] % (BF16_ROWS * steps) == 0 for w in to_cast)
    cast_specs = [pl.BlockSpec((w.shape[0] // steps, w.shape[1]), lambda i: (i, 0)) for w in to_cast]
    in_specs = in_specs + [_const_spec(kgain_t.shape), pos_t, pos_t] + cast_specs
    feat = pl.BlockSpec((1, WIDTH, tm), lambda i: (i // tps, 0, i % tps))
    tail = pl.BlockSpec((1, WIDTH, keep), lambda i: (i // tps, 0, 0))
    out_specs = [row(WIDTH), feat, row(WIDTH), row(WIDTH), feat,
                 pl.BlockSpec((tm * N_BLOCKS, LANES), lambda i: (i, 0)), tail, tail] + cast_specs
    out_shape = [jax.ShapeDtypeStruct((n, WIDTH), BF16),
                 jax.ShapeDtypeStruct((b, WIDTH, s), BF16),
                 jax.ShapeDtypeStruct((n, WIDTH), BF16),
                 jax.ShapeDtypeStruct((n, WIDTH), BF16),
                 jax.ShapeDtypeStruct((b, WIDTH, s), F32),
                 jax.ShapeDtypeStruct((n * N_BLOCKS, LANES), F32),
                 jax.ShapeDtypeStruct((b, WIDTH, keep), F32),
                 jax.ShapeDtypeStruct((b, WIDTH, keep), F32)]
    out_shape += [jax.ShapeDtypeStruct(w.shape, BF16) for w in to_cast]
    return pl.pallas_call(
        functools.partial(_proj_kernel, tiles_per_seq=tps),
        grid=(steps,),
        in_specs=in_specs,
        out_specs=out_specs,
        out_shape=out_shape,
        compiler_params=_params(1),
        name="proj_prompt",
    )(x2d, g1, w_in, gains, cos, sin, pool, kgain_t, cos_t, sin_t, *to_cast)


def _group_bias(f_ref):
    bias = _pair_bias(f_ref, GROUP_Q)[:, :GROUP_KEYS]
    qc = (lax.broadcasted_iota(jnp.int32, bias.shape, 0) % GROUP_Q) // CHUNK
    kc = lax.broadcasted_iota(jnp.int32, bias.shape, 1) // CHUNK
    return jnp.where((kc >= qc) & (kc <= qc + BAND_PAST_CHUNKS), bias, NEG_INF)


def _band_prompt_kernel(q_ref, kt_ref, v_ref, f_ref, *refs, lam_init):
    sample_in, lam_ref, subln_ref = refs[:10], refs[10], refs[11]
    o_ref, ya_s_ref, yb_s_ref, bias_ref, kt_heads_ref = refs[12:]
    n_groups = q_ref.shape[1] // GROUP_Q
    stages = _sample_mixers(*sample_in, f_ref, lam_ref, subln_ref, ya_s_ref, yb_s_ref,
                            pl.program_id(1) * BAND_PAIRS_PER_STEP, lam_init)
    side_job = {(0, (k + 1) * n_groups // 4): stage for k, stage in enumerate(stages)}

    def window(g):
        return max(0, g * GROUP_Q - BAND_PAST), (g + 1) * GROUP_Q

    for jj in range(BAND_PAIRS_PER_STEP):
        cols = slice(jj * LANES, (jj + 1) * LANES)
        bias_ref[jj] = _group_bias(f_ref.at[jj])
        kt = kt_ref[0, cols, :]
        feat = lax.broadcasted_iota(jnp.int32, kt.shape, 0)
        kt_heads_ref[jj, 0] = jnp.where(feat < HEAD_DIM, kt, jnp.zeros_like(kt))
        kt_heads_ref[jj, 1] = jnp.where(feat >= HEAD_DIM, kt, jnp.zeros_like(kt))

        def scores(g):
            lo, hi = window(g)
            q = q_ref[0, g * GROUP_Q:(g + 1) * GROUP_Q, cols]
            s = _dot(q, jnp.concatenate([kt_heads_ref[jj, 0, :, lo:hi], kt_heads_ref[jj, 1, :, lo:hi]], axis=1))
            s = jnp.concatenate([s[:, :hi - lo], s[:, hi - lo:]], axis=0)
            return s + bias_ref[jj, :, GROUP_KEYS - (hi - lo):]

        nxt = scores(0)
        for g in range(n_groups):
            s = nxt
            if g + 1 < n_groups:
                nxt = scores(g + 1)
            lo, hi = window(g)
            (p,), l = _softmax_parts([s])
            o = _dot(p.astype(BF16), v_ref[0, lo:hi, cols]) * (1.0 / l)
            o_ref[0, g * GROUP_Q:(g + 1) * GROUP_Q, cols] = _pick_halves(o).astype(o_ref.dtype)
            if (jj, g) in side_job:
                side_job[jj, g]()


def _band_prompt(qa, kat, va, f, new, caches, lam_vecs, subln, lam_init, t):
    b, s, _ = qa.shape
    pairs = BAND_PAIRS_PER_STEP
    assert new[0].shape[0] == b * t
    blk = pl.BlockSpec((1, s, pairs * LANES), lambda i, j: (i, 0, j))
    blk_t = pl.BlockSpec((1, pairs * LANES, s), lambda i, j: (i, j, 0))
    row = pl.BlockSpec((t, pairs * LANES), lambda i, j: (i, j))
    feat = lambda c: pl.BlockSpec((1, pairs * LANES, c.shape[2]), lambda i, j: (i, j, 0))
    whole = lambda c: pl.BlockSpec((1,) + c.shape[1:], lambda i, j: (i, 0, 0))
    cak, cav, cbk, cbv4 = caches
    return pl.pallas_call(
        functools.partial(_band_prompt_kernel, lam_init=lam_init),
        grid=(b, N_BLOCKS // pairs),
        in_specs=[blk, blk_t, blk, pl.BlockSpec((pairs, 2, BIAS_SPAN), lambda i, j: (j, 0, 0))]
        + [row] * 6 + [feat(cak), feat(cav), feat(cbk), whole(cbv4)]
        + [_const_spec(lam_vecs.shape), _const_spec(subln.shape)],
        out_specs=[blk, row, row],
        out_shape=[jax.ShapeDtypeStruct((b, s, WIDTH), BF16)] + [jax.ShapeDtypeStruct((b * t, WIDTH), BF16)] * 2,
        scratch_shapes=[pltpu.VMEM((pairs, 2 * GROUP_Q, GROUP_KEYS), F32), pltpu.VMEM((pairs, 2, LANES, s), BF16)],
        compiler_params=_params(2),
        name="band_prompt",
    )(qa, kat, va, f, *new, *caches, lam_vecs, subln)


def _lambda(lam_ref, lam_init):
    e1 = jnp.exp(jnp.sum(lam_ref[0:1, :] * lam_ref[1:2, :], axis=-1, keepdims=True))
    e2 = jnp.exp(jnp.sum(lam_ref[2:3, :] * lam_ref[3:4, :], axis=-1, keepdims=True))
    return e1 - e2 + lam_init


def _diff_out(o, subln, lam_init):
    return _rms(o, subln) * (1.0 - lam_init)


def _diff_prompt_kernel(q_ref, kt_ref, v4_ref, lam_ref, subln_ref, o_ref, kbf_ref, vbf_ref, *, lam_init):
    s_len = q_ref.shape[1]
    tq = Q_BLOCK_B
    n_blocks = s_len // tq
    lam = _lambda(lam_ref, lam_init)
    row = lax.broadcasted_iota(jnp.int32, (2 * tq, tq), 0) % tq
    col = lax.broadcasted_iota(jnp.int32, (2 * tq, tq), 1)
    diag_mask = jnp.where(col // CHUNK <= row // CHUNK, 0.0, NEG_INF).astype(F32)

    for jj in range(DIFF_HEADS_PER_STEP):
        cols = slice(jj * LANES, (jj + 1) * LANES)
        head = pl.program_id(1) * DIFF_HEADS_PER_STEP + jj
        kbf_ref[jj] = kt_ref[0, cols, :].astype(BF16)
        vbf_ref[jj] = v4_ref[0, pl.ds(head, s_len, stride=N_BLOCKS), :].astype(BF16)

        def scores(i):
            r0 = i * tq
            qs = _lane_halves(q_ref[0, r0:r0 + tq, cols])
            parts = [_dot(qs, kbf_ref[jj, :, r0:r0 + tq]) + diag_mask]
            if i > 0:
                parts.append(_dot(qs, kbf_ref[jj, :, 0:r0]))
            return parts

        nxt = scores(0)
        for i in range(n_blocks):
            r0 = i * tq
            parts = nxt
            if i + 1 < n_blocks:
                nxt = scores(i + 1)
            a, inv = _diff_combine(*_softmax_parts(parts), lam, tq)
            o = _dot(a[0], vbf_ref[jj, r0:r0 + tq, :])
            if i > 0:
                o = o + _dot(a[1], vbf_ref[jj, 0:r0, :])
            o_ref[0, r0:r0 + tq, cols] = _diff_out(o * inv, subln_ref[...], lam_init).astype(o_ref.dtype)


def _diff_prompt(qb, kbt, vb4, lam_vecs, subln, lam_init):
    b, s, _ = qb.shape
    heads = DIFF_HEADS_PER_STEP
    blk = pl.BlockSpec((1, s, heads * LANES), lambda i, j: (i, 0, j))
    return pl.pallas_call(
        functools.partial(_diff_prompt_kernel, lam_init=lam_init),
        grid=(b, N_BLOCKS // heads),
        in_specs=[blk, pl.BlockSpec((1, heads * LANES, s), lambda i, j: (i, j, 0)),
                  pl.BlockSpec((1, s * N_BLOCKS, LANES), lambda i, j: (i, 0, 0)),
                  _const_spec(lam_vecs.shape), _const_spec(subln.shape)],
        out_specs=blk,
        out_shape=jax.ShapeDtypeStruct((b, s, WIDTH), BF16),
        scratch_shapes=[pltpu.VMEM((heads, LANES, s), BF16), pltpu.VMEM((heads, s, LANES), BF16)],
        compiler_params=_params(2),
        name="diff_prompt",
    )(qb, kbt, vb4, lam_vecs, subln)


def _sample_mixers(qa_ref, ka_ref, va_ref, qb_ref, kb_ref, vb_ref, cakt_ref, cavt_ref, cbkt_ref, cbv4_ref,
                   f_ref, lam_ref, subln_ref, ya_ref, yb_ref, first_head, lam_init):
    t = qa_ref.shape[0]
    lc = cakt_ref.shape[2]
    past = cbkt_ref.shape[2]
    blocks = [slice(j * LANES, (j + 1) * LANES) for j in range(qa_ref.shape[1] // LANES)]
    st = {}

    def scores():
        sa = []
        for j, cols in enumerate(blocks):
            bias = _pair_bias(f_ref.at[j], t)
            qm = _lane_halves(qa_ref[:, cols])
            sa.append([_dot(qm, cakt_ref[0, cols, :].astype(BF16)) + bias[:, :lc],
                       _dot_nt(qm, ka_ref[:, cols].astype(BF16)) + bias[:, lc:lc + t]])
        sb = []
        for cols in blocks:
            qs = _lane_halves(qb_ref[:, cols])
            sb.append([_dot(qs, cbkt_ref[0, cols, :].astype(BF16)), _dot_nt(qs, kb_ref[:, cols].astype(BF16))])
        st["s"] = (sa, sb)

    def softmax():
        lam = _lambda(lam_ref, lam_init)
        sa, sb = st["s"]
        st["p"] = ([_softmax_parts(s) for s in sa], [_diff_combine(*_softmax_parts(s), lam, t) for s in sb])

    def output():
        pa, pb = st["p"]
        for cols, ((pc, pn), l) in zip(blocks, pa):
            o = _dot_nt(pc.astype(BF16), cavt_ref[0, cols, :].astype(BF16)) + _dot(pn.astype(BF16),
                                                                                 va_ref[:, cols].astype(BF16))
            ya_ref[:, cols] = _pick_halves(o * (1.0 / l)).astype(ya_ref.dtype)
        for j, (cols, ((ac, an), inv)) in enumerate(zip(blocks, pb)):
            vc = cbv4_ref[0, pl.ds(first_head + j, past, stride=N_BLOCKS), :].astype(BF16)
            o = (_dot(ac, vc) + _dot(an, vb_ref[:, cols].astype(BF16))) * inv
            yb_ref[:, cols] = _diff_out(o, subln_ref[...], lam_init).astype(yb_ref.dtype)

    return scores, softmax, output


def _merge_mlp_kernel(x_ref, ya_ref, yb_ref, g1_ref, wg_ref, bg_ref, wpa_ref, wpb_ref, wo_ref,
                      g2_ref, w1_ref, w2_ref, o_ref):
    x = x_ref[...]
    h = _rms(x, g1_ref[...]).astype(BF16)
    m = jax.nn.sigmoid(_dot(h, wg_ref[:, :D_MODEL]) + bg_ref[:, :D_MODEL]) * _dot(ya_ref[...], wpa_ref[...])
    m = m + jax.nn.sigmoid(_dot(h, wg_ref[:, D_MODEL:]) + bg_ref[:, D_MODEL:]) * _dot(yb_ref[...], wpb_ref[...])
    x1 = x + _dot(m.astype(BF16), wo_ref[...])
    hn = _rms(x1, g2_ref[...]).astype(BF16)
    acc = x1
    for c in range(D_FF // D_MODEL):
        cols = slice(c * D_MODEL, (c + 1) * D_MODEL)
        u = jnp.maximum(_dot(hn, w1_ref[:, cols]), 0.0)
        acc = acc + _dot((u * u).astype(BF16), w2_ref[cols, :])
    o_ref[...] = acc


def _merge_mlp(x2d, ya, yb, tm, g1, wg, bg, wpa, wpb, wo, g2, w1, w2):
    n = x2d.shape[0]
    row = lambda w: pl.BlockSpec((tm, w), lambda i: (i, 0))
    consts = (g1, wg, bg, wpa, wpb, wo, g2, w1, w2)
    return pl.pallas_call(
        _merge_mlp_kernel,
        grid=(n // tm,),
        in_specs=[row(D_MODEL), row(WIDTH), row(WIDTH)] + [_const_spec(c.shape) for c in consts],
        out_specs=row(D_MODEL),
        out_shape=jax.ShapeDtypeStruct((n, D_MODEL), F32),
        compiler_params=_params(1),
        name="merge_mlp",
    )(x2d, ya, yb, *consts)


def _rope_tables(pos):
    half = HEAD_DIM // 2
    inv_freq = ROPE_THETA ** (-jnp.arange(half, dtype=F32) / half)
    ang = pos.astype(F32)[:, None] * inv_freq[None, :]
    cos = jnp.cos(ang)
    sin = jnp.sin(ang)
    reps = LANES // HEAD_DIM
    return (jnp.tile(jnp.concatenate([cos, cos], axis=1), (1, reps)),
            jnp.tile(jnp.concatenate([-sin, sin], axis=1), (1, reps)), cos.T, sin.T)


def _band_offset_table(table):
    n_far = BAND_PAST + GROUP_Q - 1 - REL_PAST + 1
    n_fut = BIAS_SPAN - n_far - (N_REL - 1)
    h = table.shape[0]
    f = jnp.concatenate([jnp.broadcast_to(table[:, N_REL - 1:], (h, n_far)), table[:, N_REL - 2::-1],
                         jnp.broadcast_to(table[:, :1], (h, n_fut))], axis=1).astype(F32)
    return f.reshape(h // 2, 2, BIAS_SPAN)


def _pool_matrix():
    i = np.arange(MXU_DIM)
    return jnp.asarray((i[:, None] // HEAD_DIM == i[None, :] // HEAD_DIM) / HEAD_DIM, dtype=BF16)


def kernel(x_prompt, x_sample, cache_a_k, cache_a_v, cache_b_k, cache_b_v, ln1_g, w_in, qn_a, kn_a, rel_bias, qn_b, kn_b, lam_q1, lam_k1, lam_q2, lam_k2, subln_g, w_gate, b_gate, w_proj_a, w_proj_b, w_out, ln2_g, w_ff1, w_ff2):
    depth = w_in.shape[0]
    b, s, d = x_prompt.shape
    db, t, _ = x_sample.shape
    past_len = cache_b_k.shape[2]
    lc = cache_a_k.shape[2]
    keep = min(BAND_PAST, s)
    assert d == D_MODEL and s % (2 * ROW_TILE) == 0 and keep <= 2 * ROW_TILE and keep % LANES == 0
    assert lc == BAND_PAST and t <= GROUP_Q

    cos_p, sin_p, cos_pt, sin_pt = _rope_tables(jnp.arange(s))
    cos_s, sin_s = (jnp.tile(a, (db, 1)) for a in _rope_tables(past_len + jnp.arange(t))[:2])
    lane_const = lambda g: jnp.broadcast_to(g[:, None], (HEAD_DIM, LANES))
    pool = _pool_matrix()
    tile_heads = lambda g: jnp.tile(g, WIDTH // HEAD_DIM)

    xp = x_prompt.reshape(b * s, d)
    xs = x_sample.reshape(db * t, d)
    outs = [[] for _ in range(8)]
    for l in range(depth):
        lam_init = 0.8 - 0.6 * math.exp(-0.3 * l)
        g1 = ln1_g[l][None]
        g2 = ln2_g[l][None]
        gains = jnp.stack([tile_heads(qn_a[l]), tile_heads(kn_a[l]), tile_heads(qn_b[l]), tile_heads(kn_b[l])])
        lam_vecs = jnp.stack([lam_q1[l], lam_k1[l], lam_q2[l], lam_k2[l]])
        subln = subln_g[l][None]
        bias_f = _band_offset_table(rel_bias[l])
        w_in_l = w_in[l].astype(BF16)

        kgain_t = jnp.stack([lane_const(kn_a[l]), lane_const(kn_b[l])])
        later_w = (w_gate[l], w_proj_a[l], w_proj_b[l], w_out[l], w_ff1[l], w_ff2[l])
        proj_out = _proj_prompt(xp, b, s, keep, g1, w_in_l, gains, cos_p, sin_p, pool, kgain_t, cos_pt, sin_pt, later_w)
        qa, kat, va, qb, kbt, vb4, kat_tail, vat_tail = proj_out[:8]
        wg, wpa, wpb, wo, w1, w2 = proj_out[8:]
        merge_w = (g1, wg, b_gate[l][None], wpa, wpb, wo, g2, w1, w2)
        as3 = lambda a: a.reshape(b, s, WIDTH)
        new = _proj_sample(xs, g1, w_in_l, gains, cos_s, sin_s, pool)
        caches = (cache_a_k[l].transpose(0, 2, 3, 1).reshape(db, WIDTH, lc),
                  cache_a_v[l].transpose(0, 2, 3, 1).reshape(db, WIDTH, lc),
                  cache_b_k[l].transpose(0, 2, 3, 4, 1).reshape(db, WIDTH, past_len),
                  cache_b_v[l].reshape(db, past_len * N_BLOCKS, LANES))
        ya, ya_s, yb_s = _band_prompt(as3(qa), kat, as3(va), bias_f, new, caches, lam_vecs, subln, lam_init, t)
        yb = _diff_prompt(as3(qb), kbt, vb4.reshape(b, s * N_BLOCKS, LANES), lam_vecs, subln, lam_init)
        xp = _merge_mlp(xp, ya.reshape(b * s, WIDTH), yb.reshape(b * s, WIDTH), 2 * ROW_TILE, *merge_w)
        token_major_a = lambda a: a.reshape(b, N_HEADS_A, HEAD_DIM, keep).transpose(0, 3, 1, 2)
        outs[0].append(token_major_a(kat_tail))
        outs[1].append(token_major_a(vat_tail))
        outs[2].append(kbt.reshape(b, N_HEADS_B, 2, HEAD_DIM, s).transpose(0, 4, 1, 2, 3))
        outs[3].append(vb4.reshape(b, s, N_HEADS_B, 2 * HEAD_DIM))

        xs = _merge_mlp(xs, ya_s, yb_s, db * t, *merge_w)
        outs[4].append(new[1].reshape(db, t, N_HEADS_A, HEAD_DIM))
        outs[5].append(new[2].reshape(db, t, N_HEADS_A, HEAD_DIM))
        outs[6].append(new[4].reshape(db, t, N_HEADS_B, 2, HEAD_DIM))
        outs[7].append(new[5].reshape(db, t, N_HEADS_B, 2 * HEAD_DIM))

    return (xp.reshape(b, s, d), xs.reshape(db, t, d)) + tuple(jnp.stack(o) for o in outs)
```

```python
import functools
import math

import jax
import jax.numpy as jnp
import numpy as np
from jax import lax
from jax.experimental import pallas as pl
from jax.experimental.pallas import tpu as pltpu

D_MODEL = 1024
CHUNK = 64
HEAD_DIM = 64
N_HEADS_A = 8
N_HEADS_B = 4
BAND_PAST_CHUNKS = 8
BAND_PAST = BAND_PAST_CHUNKS * CHUNK
REL_FUT = CHUNK - 1
REL_PAST = 256
N_REL = REL_FUT + REL_PAST + 1
WIDTH = 512
N_GROUPS = 6
D_FF = 4 * D_MODEL
ROPE_THETA = 10000.0
EPS = 1e-6
NEG_INF = -1e30
LOG2E = math.log2(math.e)
Q_SCALE = HEAD_DIM ** -0.5 * LOG2E

LANES = 128
BF16_ROWS = 16
N_BLOCKS = WIDTH // LANES
MXU_DIM = 256
GROUP_CHUNKS = 2
GROUP_Q = GROUP_CHUNKS * CHUNK
GROUP_KEYS = BAND_PAST + GROUP_Q
BIAS_SPAN = GROUP_KEYS + GROUP_Q
BAND_PAIRS_PER_STEP = 2
Q_BLOCK_B = 256
DIFF_HEADS_PER_STEP = 1
ROW_TILE = 512
VMEM_LIMIT = 60 * 1024 * 1024

BF16 = jnp.bfloat16
F32 = jnp.float32


def _dot(a, b):
    return jnp.dot(a, b, preferred_element_type=F32)


def _dot_nt(a, b):
    return lax.dot_general(a, b, (((1,), (1,)), ((), ())), preferred_element_type=F32)


def _rms(x, g):
    return x * lax.rsqrt(jnp.mean(x * x, axis=-1, keepdims=True) + EPS) * g


def _const_spec(shape):
    nd = len(shape)
    return pl.BlockSpec(shape, lambda *_: (0,) * nd, pipeline_mode=pl.Buffered(1))


def _params(n_grid_axes):
    return pltpu.CompilerParams(dimension_semantics=("arbitrary",) * n_grid_axes, vmem_limit_bytes=VMEM_LIMIT)


def _lane_halves(q):
    lane = lax.broadcasted_iota(jnp.int32, q.shape, 1)
    zero = jnp.zeros_like(q)
    return jnp.concatenate([jnp.where(lane < HEAD_DIM, q, zero), jnp.where(lane >= HEAD_DIM, q, zero)], axis=0)


def _pick_halves(o):
    m = o.shape[0] // 2
    lane = lax.broadcasted_iota(jnp.int32, (m, LANES), 1)
    return jnp.where(lane < HEAD_DIM, o[:m], o[m:])


def _softmax_parts(parts):
    m = functools.reduce(jnp.maximum, [jnp.max(s, axis=-1, keepdims=True) for s in parts])
    ps = [jnp.exp2(s - m) for s in parts]
    l = functools.reduce(lambda a, b: a + b, [jnp.sum(p, axis=-1, keepdims=True) for p in ps])
    return ps, l


def _diff_combine(ps, l, lam, t):
    c = lam * l[:t] / l[t:]
    return [(p[:t] - p[t:] * c).astype(BF16) for p in ps], 1.0 / l[:t]


def _pair_bias(f_ref, rows):
    def one(hd):
        x = jnp.broadcast_to(f_ref[hd:hd + 1, :], (rows, BIAS_SPAN))
        return pltpu.roll(x, BIAS_SPAN - GROUP_Q + 1, 1, stride=1, stride_axis=0)

    return jnp.concatenate([one(0), one(1)], axis=0) * LOG2E


def _proj_kernel(x_ref, g1_ref, w_ref, gains_ref, cos_ref, sin_ref, pool_ref, *refs, tiles_per_seq):
    h = _rms(x_ref[...], g1_ref[...]).astype(BF16)
    cos = jnp.concatenate([cos_ref[...]] * N_BLOCKS, axis=1)
    sin = jnp.concatenate([sin_ref[...]] * N_BLOCKS, axis=1)
    pool = pool_ref[...]

    def group(i):
        return _dot(h, w_ref[:, i * WIDTH:(i + 1) * WIDTH])

    def head_norm(z, i):
        zz = (z * z).astype(BF16)
        ms = jnp.concatenate([_dot(zz[:, k * MXU_DIM:(k + 1) * MXU_DIM], pool) for k in range(WIDTH // MXU_DIM)],
                             axis=1)
        return z * lax.rsqrt(ms + EPS) * gains_ref[i:i + 1, :]

    def rope(z):
        lane = lax.broadcasted_iota(jnp.int32, z.shape, 1)
        half = HEAD_DIM // 2
        partner = jnp.where(lane % HEAD_DIM < half, pltpu.roll(z, WIDTH - half, 1), pltpu.roll(z, half, 1))
        return z * cos + partner * sin

    if tiles_per_seq == 0:
        qa_ref, ka_ref, va_ref, qb_ref, kb_ref, vb_ref = refs
        qa_ref[...] = (head_norm(group(0), 0) * Q_SCALE).astype(BF16)
        ka_ref[...] = head_norm(group(1), 1)
        va_ref[...] = group(2)
        qb_ref[...] = (rope(head_norm(group(3), 2)) * Q_SCALE).astype(BF16)
        kb_ref[...] = rope(head_norm(group(4), 3))
        vb_ref[...] = group(5)
        return

    kgain_ref, cos_t_ref, sin_t_ref = refs[:3]
    n_cast = (len(refs) - 3 - 8) // 2
    qa_ref, kat_ref, va_ref, qb_ref, kbt_ref, vb4_ref, kat_tail_ref, vat_tail_ref = refs[3 + n_cast:11 + n_cast]
    for src_ref, dst_ref in zip(refs[3:3 + n_cast], refs[11 + n_cast:]):
        dst_ref[...] = src_ref[...].astype(BF16)
    tm = x_ref.shape[0]
    n_heads = WIDTH // HEAD_DIM

    def head_norm_t(zt, i):
        z3 = zt.reshape(n_heads, HEAD_DIM, tm)
        ms = jnp.mean(z3 * z3, axis=1, keepdims=True)
        gain = jnp.concatenate([kgain_ref[i]] * (tm // LANES), axis=1)
        return z3 * lax.rsqrt(ms + EPS) * gain[None]

    def rope_t(z3):
        half = HEAD_DIM // 2
        cos_t = cos_t_ref[...][None]
        sin_t = sin_t_ref[...][None]
        x1, x2 = z3[:, :half], z3[:, half:]
        return jnp.concatenate([x1 * cos_t - x2 * sin_t, x2 * cos_t + x1 * sin_t], axis=1)

    kbt_ref[0] = rope_t(head_norm_t(group(4).T, 1)).reshape(WIDTH, tm)
    kat = head_norm_t(group(1).T, 0).reshape(WIDTH, tm)
    kat_ref[0] = kat.astype(BF16)
    va = group(2)
    va_ref[...] = va.astype(BF16)
    qb_ref[...] = (rope(head_norm(group(3), 2)) * Q_SCALE).astype(BF16)
    qa_ref[...] = (head_norm(group(0), 0) * Q_SCALE).astype(BF16)
    vb = group(5)
    for hd in range(N_BLOCKS):
        vb4_ref[pl.ds(hd, tm, stride=N_BLOCKS), :] = vb[:, hd * LANES:(hd + 1) * LANES]

    @pl.when(pl.program_id(0) % tiles_per_seq == tiles_per_seq - 1)
    def _():
        keep = kat_tail_ref.shape[2]
        kat_tail_ref[0] = kat[:, tm - keep:]
        vat_tail_ref[0] = va[tm - keep:].T


def _proj_specs(tm, n_pos_blocks, consts):
    row = lambda w: pl.BlockSpec((tm, w), lambda i: (i, 0))
    pos = pl.BlockSpec((tm, LANES), lambda i: (i % n_pos_blocks, 0))
    g1, w_in, gains, pool = consts
    return row, [row(D_MODEL), _const_spec(g1.shape), _const_spec(w_in.shape), _const_spec(gains.shape),
                 pos, pos, _const_spec(pool.shape)]


def _proj_sample(x2d, g1, w_in, gains, cos, sin, pool):
    n = x2d.shape[0]
    row, in_specs = _proj_specs(n, 1, (g1, w_in, gains, pool))
    out_dtypes = (BF16, F32, F32, BF16, F32, F32)
    return pl.pallas_call(
        functools.partial(_proj_kernel, tiles_per_seq=0),
        grid=(1,),
        in_specs=in_specs,
        out_specs=[row(WIDTH)] * N_GROUPS,
        out_shape=[jax.ShapeDtypeStruct((n, WIDTH), dt) for dt in out_dtypes],
        compiler_params=_params(1),
        name="proj_sample",
    )(x2d, g1, w_in, gains, cos, sin, pool)


def _proj_prompt(x2d, b, s, keep, g1, w_in, gains, cos, sin, pool, kgain_t, cos_t, sin_t, to_cast):
    n = x2d.shape[0]
    tm = 2 * ROW_TILE
    tps = s // tm
    steps = n // tm
    row, in_specs = _proj_specs(tm, tps, (g1, w_in, gains, pool))
    pos_t = pl.BlockSpec((HEAD_DIM // 2, tm), lambda i: (0, i % tps))
    assert all(w.shape[0] % (BF16_ROWS * steps) == 0 for w in to_cast)
    cast_specs = [pl.BlockSpec((w.shape[0] // steps, w.shape[1]), lambda i: (i, 0)) for w in to_cast]
    in_specs = in_specs + [_const_spec(kgain_t.shape), pos_t, pos_t] + cast_specs
    feat = pl.BlockSpec((1, WIDTH, tm), lambda i: (i // tps, 0, i % tps))
    tail = pl.BlockSpec((1, WIDTH, keep), lambda i: (i // tps, 0, 0))
    out_specs = [row(WIDTH), feat, row(WIDTH), row(WIDTH), feat,
                 pl.BlockSpec((tm * N_BLOCKS, LANES), lambda i: (i, 0)), tail, tail] + cast_specs
    out_shape = [jax.ShapeDtypeStruct((n, WIDTH), BF16),
                 jax.ShapeDtypeStruct((b, WIDTH, s), BF16),
                 jax.ShapeDtypeStruct((n, WIDTH), BF16),
                 jax.ShapeDtypeStruct((n, WIDTH), BF16),
                 jax.ShapeDtypeStruct((b, WIDTH, s), F32),
                 jax.ShapeDtypeStruct((n * N_BLOCKS, LANES), F32),
                 jax.ShapeDtypeStruct((b, WIDTH, keep), F32),
                 jax.ShapeDtypeStruct((b, WIDTH, keep), F32)]
    out_shape += [jax.ShapeDtypeStruct(w.shape, BF16) for w in to_cast]
    return pl.pallas_call(
        functools.partial(_proj_kernel, tiles_per_seq=tps),
        grid=(steps,),
        in_specs=in_specs,
        out_specs=out_specs,
        out_shape=out_shape,
        compiler_params=_params(1),
        name="proj_prompt",
    )(x2d, g1, w_in, gains, cos, sin, pool, kgain_t, cos_t, sin_t, *to_cast)


def _group_bias(f_ref):
    bias = _pair_bias(f_ref, GROUP_Q)[:, :GROUP_KEYS]
    qc = (lax.broadcasted_iota(jnp.int32, bias.shape, 0) % GROUP_Q) // CHUNK
    kc = lax.broadcasted_iota(jnp.int32, bias.shape, 1) // CHUNK
    return jnp.where((kc >= qc) & (kc <= qc + BAND_PAST_CHUNKS), bias, NEG_INF)


def _band_prompt_kernel(q_ref, kt_ref, v_ref, f_ref, *refs, lam_init):
    sample_in, lam_ref, subln_ref = refs[:10], refs[10], refs[11]
    o_ref, ya_s_ref, yb_s_ref, bias_ref, kt_heads_ref = refs[12:]
    n_groups = q_ref.shape[1] // GROUP_Q
    stages = _sample_mixers(*sample_in, f_ref, lam_ref, subln_ref, ya_s_ref, yb_s_ref,
                            pl.program_id(1) * BAND_PAIRS_PER_STEP, lam_init)
    side_job = {(0, (k + 1) * n_groups // 4): stage for k, stage in enumerate(stages)}

    def window(g):
        return max(0, g * GROUP_Q - BAND_PAST), (g + 1) * GROUP_Q

    for jj in range(BAND_PAIRS_PER_STEP):
        cols = slice(jj * LANES, (jj + 1) * LANES)
        bias_ref[jj] = _group_bias(f_ref.at[jj])
        kt = kt_ref[0, cols, :]
        feat = lax.broadcasted_iota(jnp.int32, kt.shape, 0)
        kt_heads_ref[jj, 0] = jnp.where(feat < HEAD_DIM, kt, jnp.zeros_like(kt))
        kt_heads_ref[jj, 1] = jnp.where(feat >= HEAD_DIM, kt, jnp.zeros_like(kt))

        def scores(g):
            lo, hi = window(g)
            q = q_ref[0, g * GROUP_Q:(g + 1) * GROUP_Q, cols]
            s = _dot(q, jnp.concatenate([kt_heads_ref[jj, 0, :, lo:hi], kt_heads_ref[jj, 1, :, lo:hi]], axis=1))
            s = jnp.concatenate([s[:, :hi - lo], s[:, hi - lo:]], axis=0)
            return s + bias_ref[jj, :, GROUP_KEYS - (hi - lo):]

        nxt = scores(0)
        for g in range(n_groups):
            s = nxt
            if g + 1 < n_groups:
                nxt = scores(g + 1)
            lo, hi = window(g)
            (p,), l = _softmax_parts([s])
            o = _dot(p.astype(BF16), v_ref[0, lo:hi, cols]) * (1.0 / l)
            o_ref[0, g * GROUP_Q:(g + 1) * GROUP_Q, cols] = _pick_halves(o).astype(o_ref.dtype)
            if (jj, g) in side_job:
                side_job[jj, g]()


def _band_prompt(qa, kat, va, f, new, caches, lam_vecs, subln, lam_init, t):
    b, s, _ = qa.shape
    pairs = BAND_PAIRS_PER_STEP
    assert new[0].shape[0] == b * t
    blk = pl.BlockSpec((1, s, pairs * LANES), lambda i, j: (i, 0, j))
    blk_t = pl.BlockSpec((1, pairs * LANES, s), lambda i, j: (i, j, 0))
    row = pl.BlockSpec((t, pairs * LANES), lambda i, j: (i, j))
    feat = lambda c: pl.BlockSpec((1, pairs * LANES, c.shape[2]), lambda i, j: (i, j, 0))
    whole = lambda c: pl.BlockSpec((1,) + c.shape[1:], lambda i, j: (i, 0, 0))
    cak, cav, cbk, cbv4 = caches
    return pl.pallas_call(
        functools.partial(_band_prompt_kernel, lam_init=lam_init),
        grid=(b, N_BLOCKS // pairs),
        in_specs=[blk, blk_t, blk, pl.BlockSpec((pairs, 2, BIAS_SPAN), lambda i, j: (j, 0, 0))]
        + [row] * 6 + [feat(cak), feat(cav), feat(cbk), whole(cbv4)]
        + [_const_spec(lam_vecs.shape), _const_spec(subln.shape)],
        out_specs=[blk, row, row],
        out_shape=[jax.ShapeDtypeStruct((b, s, WIDTH), BF16)] + [jax.ShapeDtypeStruct((b * t, WIDTH), BF16)] * 2,
        scratch_shapes=[pltpu.VMEM((pairs, 2 * GROUP_Q, GROUP_KEYS), F32), pltpu.VMEM((pairs, 2, LANES, s), BF16)],
        compiler_params=_params(2),
        name="band_prompt",
    )(qa, kat, va, f, *new, *caches, lam_vecs, subln)


def _lambda(lam_ref, lam_init):
    e1 = jnp.exp(jnp.sum(lam_ref[0:1, :] * lam_ref[1:2, :], axis=-1, keepdims=True))
    e2 = jnp.exp(jnp.sum(lam_ref[2:3, :] * lam_ref[3:4, :], axis=-1, keepdims=True))
    return e1 - e2 + lam_init


def _diff_out(o, subln, lam_init):
    return _rms(o, subln) * (1.0 - lam_init)


def _diff_prompt_kernel(q_ref, kt_ref, v4_ref, lam_ref, subln_ref, o_ref, kbf_ref, vbf_ref, *, lam_init):
    s_len = q_ref.shape[1]
    tq = Q_BLOCK_B
    n_blocks = s_len // tq
    lam = _lambda(lam_ref, lam_init)
    row = lax.broadcasted_iota(jnp.int32, (2 * tq, tq), 0) % tq
    col = lax.broadcasted_iota(jnp.int32, (2 * tq, tq), 1)
    diag_mask = jnp.where(col // CHUNK <= row // CHUNK, 0.0, NEG_INF).astype(F32)

    for jj in range(DIFF_HEADS_PER_STEP):
        cols = slice(jj * LANES, (jj + 1) * LANES)
        head = pl.program_id(1) * DIFF_HEADS_PER_STEP + jj
        kbf_ref[jj] = kt_ref[0, cols, :].astype(BF16)
        vbf_ref[jj] = v4_ref[0, pl.ds(head, s_len, stride=N_BLOCKS), :].astype(BF16)

        def scores(i):
            r0 = i * tq
            qs = _lane_halves(q_ref[0, r0:r0 + tq, cols])
            parts = [_dot(qs, kbf_ref[jj, :, r0:r0 + tq]) + diag_mask]
            if i > 0:
                parts.append(_dot(qs, kbf_ref[jj, :, 0:r0]))
            return parts

        nxt = scores(0)
        for i in range(n_blocks):
            r0 = i * tq
            parts = nxt
            if i + 1 < n_blocks:
                nxt = scores(i + 1)
            a, inv = _diff_combine(*_softmax_parts(parts), lam, tq)
            o = _dot(a[0], vbf_ref[jj, r0:r0 + tq, :])
            if i > 0:
                o = o + _dot(a[1], vbf_ref[jj, 0:r0, :])
            o_ref[0, r0:r0 + tq, cols] = _diff_out(o * inv, subln_ref[...], lam_init).astype(o_ref.dtype)


def _diff_prompt(qb, kbt, vb4, lam_vecs, subln, lam_init):
    b, s, _ = qb.shape
    heads = DIFF_HEADS_PER_STEP
    blk = pl.BlockSpec((1, s, heads * LANES), lambda i, j: (i, 0, j))
    return pl.pallas_call(
        functools.partial(_diff_prompt_kernel, lam_init=lam_init),
        grid=(b, N_BLOCKS // heads),
        in_specs=[blk, pl.BlockSpec((1, heads * LANES, s), lambda i, j: (i, j, 0)),
                  pl.BlockSpec((1, s * N_BLOCKS, LANES), lambda i, j: (i, 0, 0)),
                  _const_spec(lam_vecs.shape), _const_spec(subln.shape)],
        out_specs=blk,
        out_shape=jax.ShapeDtypeStruct((b, s, WIDTH), BF16),
        scratch_shapes=[pltpu.VMEM((heads, LANES, s), BF16), pltpu.VMEM((heads, s, LANES), BF16)],
        compiler_params=_params(2),
        name="diff_prompt",
    )(qb, kbt, vb4, lam_vecs, subln)


def _sample_mixers(qa_ref, ka_ref, va_ref, qb_ref, kb_ref, vb_ref, cakt_ref, cavt_ref, cbkt_ref, cbv4_ref,
                   f_ref, lam_ref, subln_ref, ya_ref, yb_ref, first_head, lam_init):
    t = qa_ref.shape[0]
    lc = cakt_ref.shape[2]
    past = cbkt_ref.shape[2]
    blocks = [slice(j * LANES, (j + 1) * LANES) for j in range(qa_ref.shape[1] // LANES)]
    st = {}

    def scores():
        sa = []
        for j, cols in enumerate(blocks):
            bias = _pair_bias(f_ref.at[j], t)
            qm = _lane_halves(qa_ref[:, cols])
            sa.append([_dot(qm, cakt_ref[0, cols, :].astype(BF16)) + bias[:, :lc],
                       _dot_nt(qm, ka_ref[:, cols].astype(BF16)) + bias[:, lc:lc + t]])
        sb = []
        for cols in blocks:
            qs = _lane_halves(qb_ref[:, cols])
            sb.append([_dot(qs, cbkt_ref[0, cols, :].astype(BF16)), _dot_nt(qs, kb_ref[:, cols].astype(BF16))])
        st["s"] = (sa, sb)

    def softmax():
        lam = _lambda(lam_ref, lam_init)
        sa, sb = st["s"]
        st["p"] = ([_softmax_parts(s) for s in sa], [_diff_combine(*_softmax_parts(s), lam, t) for s in sb])

    def output():
        pa, pb = st["p"]
        for cols, ((pc, pn), l) in zip(blocks, pa):
            o = _dot_nt(pc.astype(BF16), cavt_ref[0, cols, :].astype(BF16)) + _dot(pn.astype(BF16),
                                                                                 va_ref[:, cols].astype(BF16))
            ya_ref[:, cols] = _pick_halves(o * (1.0 / l)).astype(ya_ref.dtype)
        for j, (cols, ((ac, an), inv)) in enumerate(zip(blocks, pb)):
            vc = cbv4_ref[0, pl.ds(first_head + j, past, stride=N_BLOCKS), :].astype(BF16)
            o = (_dot(ac, vc) + _dot(an, vb_ref[:, cols].astype(BF16))) * inv
            yb_ref[:, cols] = _diff_out(o, subln_ref[...], lam_init).astype(yb_ref.dtype)

    return scores, softmax, output


def _merge_mlp_kernel(x_ref, ya_ref, yb_ref, g1_ref, wg_ref, bg_ref, wpa_ref, wpb_ref, wo_ref,
                      g2_ref, w1_ref, w2_ref, o_ref):
    x = x_ref[...]
    h = _rms(x, g1_ref[...]).astype(BF16)
    m = jax.nn.sigmoid(_dot(h, wg_ref[:, :D_MODEL]) + bg_ref[:, :D_MODEL]) * _dot(ya_ref[...], wpa_ref[...])
    m = m + jax.nn.sigmoid(_dot(h, wg_ref[:, D_MODEL:]) + bg_ref[:, D_MODEL:]) * _dot(yb_ref[...], wpb_ref[...])
    x1 = x + _dot(m.astype(BF16), wo_ref[...])
    hn = _rms(x1, g2_ref[...]).astype(BF16)
    acc = x1
    for c in range(D_FF // D_MODEL):
        cols = slice(c * D_MODEL, (c + 1) * D_MODEL)
        u = jnp.maximum(_dot(hn, w1_ref[:, cols]), 0.0)
        acc = acc + _dot((u * u).astype(BF16), w2_ref[cols, :])
    o_ref[...] = acc


def _merge_mlp(x2d, ya, yb, tm, g1, wg, bg, wpa, wpb, wo, g2, w1, w2):
    n = x2d.shape[0]
    row = lambda w: pl.BlockSpec((tm, w), lambda i: (i, 0))
    consts = (g1, wg, bg, wpa, wpb, wo, g2, w1, w2)
    return pl.pallas_call(
        _merge_mlp_kernel,
        grid=(n // tm,),
        in_specs=[row(D_MODEL), row(WIDTH), row(WIDTH)] + [_const_spec(c.shape) for c in consts],
        out_specs=row(D_MODEL),
        out_shape=jax.ShapeDtypeStruct((n, D_MODEL), F32),
        compiler_params=_params(1),
        name="merge_mlp",
    )(x2d, ya, yb, *consts)


def _rope_tables(pos):
    half = HEAD_DIM // 2
    inv_freq = ROPE_THETA ** (-jnp.arange(half, dtype=F32) / half)
    ang = pos.astype(F32)[:, None] * inv_freq[None, :]
    cos = jnp.cos(ang)
    sin = jnp.sin(ang)
    reps = LANES // HEAD_DIM
    return (jnp.tile(jnp.concatenate([cos, cos], axis=1), (1, reps)),
            jnp.tile(jnp.concatenate([-sin, sin], axis=1), (1, reps)), cos.T, sin.T)


def _band_offset_table(table):
    n_far = BAND_PAST + GROUP_Q - 1 - REL_PAST + 1
    n_fut = BIAS_SPAN - n_far - (N_REL - 1)
    h = table.shape[0]
    f = jnp.concatenate([jnp.broadcast_to(table[:, N_REL - 1:], (h, n_far)), table[:, N_REL - 2::-1],
                         jnp.broadcast_to(table[:, :1], (h, n_fut))], axis=1).astype(F32)
    return f.reshape(h // 2, 2, BIAS_SPAN)


def _pool_matrix():
    i = np.arange(MXU_DIM)
    return jnp.asarray((i[:, None] // HEAD_DIM == i[None, :] // HEAD_DIM) / HEAD_DIM, dtype=BF16)


def kernel(x_prompt, x_sample, cache_a_k, cache_a_v, cache_b_k, cache_b_v, ln1_g, w_in, qn_a, kn_a, rel_bias, qn_b, kn_b, lam_q1, lam_k1, lam_q2, lam_k2, subln_g, w_gate, b_gate, w_proj_a, w_proj_b, w_out, ln2_g, w_ff1, w_ff2):
    depth = w_in.shape[0]
    b, s, d = x_prompt.shape
    db, t, _ = x_sample.shape
    past_len = cache_b_k.shape[2]
    lc = cache_a_k.shape[2]
    keep = min(BAND_PAST, s)
    assert d == D_MODEL and s % (2 * ROW_TILE) == 0 and keep <= 2 * ROW_TILE and keep % LANES == 0
    assert lc == BAND_PAST and t <= GROUP_Q

    cos_p, sin_p, cos_pt, sin_pt = _rope_tables(jnp.arange(s))
    cos_s, sin_s = (jnp.tile(a, (db, 1)) for a in _rope_tables(past_len + jnp.arange(t))[:2])
    lane_const = lambda g: jnp.broadcast_to(g[:, None], (HEAD_DIM, LANES))
    pool = _pool_matrix()
    tile_heads = lambda g: jnp.tile(g, WIDTH // HEAD_DIM)

    xp = x_prompt.reshape(b * s, d)
    xs = x_sample.reshape(db * t, d)
    outs = [[] for _ in range(8)]
    for l in range(depth):
        lam_init = 0.8 - 0.6 * math.exp(-0.3 * l)
        g1 = ln1_g[l][None]
        g2 = ln2_g[l][None]
        gains = jnp.stack([tile_heads(qn_a[l]), tile_heads(kn_a[l]), tile_heads(qn_b[l]), tile_heads(kn_b[l])])
        lam_vecs = jnp.stack([lam_q1[l], lam_k1[l], lam_q2[l], lam_k2[l]])
        subln = subln_g[l][None]
        bias_f = _band_offset_table(rel_bias[l])
        w_in_l = w_in[l].astype(BF16)

        kgain_t = jnp.stack([lane_const(kn_a[l]), lane_const(kn_b[l])])
        later_w = (w_gate[l], w_proj_a[l], w_proj_b[l], w_out[l], w_ff1[l], w_ff2[l])
        proj_out = _proj_prompt(xp, b, s, keep, g1, w_in_l, gains, cos_p, sin_p, pool, kgain_t, cos_pt, sin_pt, later_w)
        qa, kat, va, qb, kbt, vb4, kat_tail, vat_tail = proj_out[:8]
        wg, wpa, wpb, wo, w1, w2 = proj_out[8:]
        merge_w = (g1, wg, b_gate[l][None], wpa, wpb, wo, g2, w1, w2)
        as3 = lambda a: a.reshape(b, s, WIDTH)
        new = _proj_sample(xs, g1, w_in_l, gains, cos_s, sin_s, pool)
        caches = (cache_a_k[l].transpose(0, 2, 3, 1).reshape(db, WIDTH, lc),
                  cache_a_v[l].transpose(0, 2, 3, 1).reshape(db, WIDTH, lc),
                  cache_b_k[l].transpose(0, 2, 3, 4, 1).reshape(db, WIDTH, past_len),
                  cache_b_v[l].reshape(db, past_len * N_BLOCKS, LANES))
        ya, ya_s, yb_s = _band_prompt(as3(qa), kat, as3(va), bias_f, new, caches, lam_vecs, subln, lam_init, t)
        yb = _diff_prompt(as3(qb), kbt, vb4.reshape(b, s * N_BLOCKS, LANES), lam_vecs, subln, lam_init)
        xp = _merge_mlp(xp, ya.reshape(b * s, WIDTH), yb.reshape(b * s, WIDTH), 2 * ROW_TILE, *merge_w)
        token_major_a = lambda a: a.reshape(b, N_HEADS_A, HEAD_DIM, keep).transpose(0, 3, 1, 2)
        outs[0].append(token_major_a(kat_tail))
        outs[1].append(token_major_a(vat_tail))
        outs[2].append(kbt.reshape(b, N_HEADS_B, 2, HEAD_DIM, s).transpose(0, 4, 1, 2, 3))
        outs[3].append(vb4.reshape(b, s, N_HEADS_B, 2 * HEAD_DIM))

        xs = _merge_mlp(xs, ya_s, yb_s, db * t, *merge_w)
        outs[4].append(new[1].reshape(db, t, N_HEADS_A, HEAD_DIM))
        outs[5].append(new[2].reshape(db, t, N_HEADS_A, HEAD_DIM))
        outs[6].append(new[4].reshape(db, t, N_HEADS_B, 2, HEAD_DIM))
        outs[7].append(new[5].reshape(db, t, N_HEADS_B, 2 * HEAD_DIM))

    return (xp.reshape(b, s, d), xs.reshape(db, t, d)) + tuple(jnp.stack(o) for o in outs)
```

```python
import functools
import math

import jax
import jax.numpy as jnp
import numpy as np
from jax import lax
from jax.experimental import pallas as pl
from jax.experimental.pallas import tpu as pltpu

D_MODEL = 1024
CHUNK = 64
HEAD_DIM = 64
N_HEADS_A = 8
N_HEADS_B = 4
BAND_PAST_CHUNKS = 8
BAND_PAST = BAND_PAST_CHUNKS * CHUNK
REL_FUT = CHUNK - 1
REL_PAST = 256
N_REL = REL_FUT + REL_PAST + 1
WIDTH = 512
N_GROUPS = 6
D_FF = 4 * D_MODEL
ROPE_THETA = 10000.0
EPS = 1e-6
NEG_INF = -1e30
LOG2E = math.log2(math.e)
Q_SCALE = HEAD_DIM ** -0.5 * LOG2E

LANES = 128
BF16_ROWS = 16
N_BLOCKS = WIDTH // LANES
MXU_DIM = 256
GROUP_CHUNKS = 2
GROUP_Q = GROUP_CHUNKS * CHUNK
GROUP_KEYS = BAND_PAST + GROUP_Q
BIAS_SPAN = GROUP_KEYS + GROUP_Q
BAND_PAIRS_PER_STEP = 2
Q_BLOCK_B = 128
DIFF_HEADS_PER_STEP = 1
ROW_TILE = 512
VMEM_LIMIT = 60 * 1024 * 1024

BF16 = jnp.bfloat16
F32 = jnp.float32


def _dot(a, b):
    return jnp.dot(a, b, preferred_element_type=F32)


def _dot_nt(a, b):
    return lax.dot_general(a, b, (((1,), (1,)), ((), ())), preferred_element_type=F32)


def _rms(x, g):
    return x * lax.rsqrt(jnp.mean(x * x, axis=-1, keepdims=True) + EPS) * g


def _const_spec(shape):
    nd = len(shape)
    return pl.BlockSpec(shape, lambda *_: (0,) * nd, pipeline_mode=pl.Buffered(1))


def _params(n_grid_axes):
    return pltpu.CompilerParams(dimension_semantics=("arbitrary",) * n_grid_axes, vmem_limit_bytes=VMEM_LIMIT)


def _lane_halves(q):
    lane = lax.broadcasted_iota(jnp.int32, q.shape, 1)
    zero = jnp.zeros_like(q)
    return jnp.concatenate([jnp.where(lane < HEAD_DIM, q, zero), jnp.where(lane >= HEAD_DIM, q, zero)], axis=0)


def _pick_halves(o):
    m = o.shape[0] // 2
    lane = lax.broadcasted_iota(jnp.int32, (m, LANES), 1)
    return jnp.where(lane < HEAD_DIM, o[:m], o[m:])


def _softmax_parts(parts):
    m = functools.reduce(jnp.maximum, [jnp.max(s, axis=-1, keepdims=True) for s in parts])
    ps = [jnp.exp2(s - m) for s in parts]
    l = functools.reduce(lambda a, b: a + b, [jnp.sum(p, axis=-1, keepdims=True) for p in ps])
    return ps, l


def _diff_combine(ps, l, lam, t):
    c = lam * l[:t] / l[t:]
    return [(p[:t] - p[t:] * c).astype(BF16) for p in ps], 1.0 / l[:t]


def _pair_bias(f_ref, rows):
    def one(hd):
        x = jnp.broadcast_to(f_ref[hd:hd + 1, :], (rows, BIAS_SPAN))
        return pltpu.roll(x, BIAS_SPAN - GROUP_Q + 1, 1, stride=1, stride_axis=0)

    return jnp.concatenate([one(0), one(1)], axis=0) * LOG2E


def _proj_kernel(x_ref, g1_ref, w_ref, gains_ref, cos_ref, sin_ref, pool_ref, *refs, tiles_per_seq):
    h = _rms(x_ref[...], g1_ref[...]).astype(BF16)
    cos = jnp.concatenate([cos_ref[...]] * N_BLOCKS, axis=1)
    sin = jnp.concatenate([sin_ref[...]] * N_BLOCKS, axis=1)
    pool = pool_ref[...]

    def group(i):
        return _dot(h, w_ref[:, i * WIDTH:(i + 1) * WIDTH])

    def head_norm(z, i):
        zz = (z * z).astype(BF16)
        ms = jnp.concatenate([_dot(zz[:, k * MXU_DIM:(k + 1) * MXU_DIM], pool) for k in range(WIDTH // MXU_DIM)],
                             axis=1)
        return z * lax.rsqrt(ms + EPS) * gains_ref[i:i + 1, :]

    def rope(z):
        lane = lax.broadcasted_iota(jnp.int32, z.shape, 1)
        half = HEAD_DIM // 2
        partner = jnp.where(lane % HEAD_DIM < half, pltpu.roll(z, WIDTH - half, 1), pltpu.roll(z, half, 1))
        return z * cos + partner * sin

    if tiles_per_seq == 0:
        qa_ref, ka_ref, va_ref, qb_ref, kb_ref, vb_ref = refs
        qa_ref[...] = (head_norm(group(0), 0) * Q_SCALE).astype(BF16)
        ka_ref[...] = head_norm(group(1), 1)
        va_ref[...] = group(2)
        qb_ref[...] = (rope(head_norm(group(3), 2)) * Q_SCALE).astype(BF16)
        kb_ref[...] = rope(head_norm(group(4), 3))
        vb_ref[...] = group(5)
        return

    kgain_ref, cos_t_ref, sin_t_ref = refs[:3]
    n_cast = (len(refs) - 3 - 8) // 2
    qa_ref, kat_ref, va_ref, qb_ref, kbt_ref, vb4_ref, kat_tail_ref, vat_tail_ref = refs[3 + n_cast:11 + n_cast]
    for src_ref, dst_ref in zip(refs[3:3 + n_cast], refs[11 + n_cast:]):
        dst_ref[...] = src_ref[...].astype(BF16)
    tm = x_ref.shape[0]
    n_heads = WIDTH // HEAD_DIM

    def head_norm_t(zt, i):
        z3 = zt.reshape(n_heads, HEAD_DIM, tm)
        ms = jnp.mean(z3 * z3, axis=1, keepdims=True)
        gain = jnp.concatenate([kgain_ref[i]] * (tm // LANES), axis=1)
        return z3 * lax.rsqrt(ms + EPS) * gain[None]

    def rope_t(z3):
        half = HEAD_DIM // 2
        cos_t = cos_t_ref[...][None]
        sin_t = sin_t_ref[...][None]
        x1, x2 = z3[:, :half], z3[:, half:]
        return jnp.concatenate([x1 * cos_t - x2 * sin_t, x2 * cos_t + x1 * sin_t], axis=1)

    kbt_ref[0] = rope_t(head_norm_t(group(4).T, 1)).reshape(WIDTH, tm)
    kat = head_norm_t(group(1).T, 0).reshape(WIDTH, tm)
    kat_ref[0] = kat.astype(BF16)
    va = group(2)
    va_ref[...] = va.astype(BF16)
    qb_ref[...] = (rope(head_norm(group(3), 2)) * Q_SCALE).astype(BF16)
    qa_ref[...] = (head_norm(group(0), 0) * Q_SCALE).astype(BF16)
    vb = group(5)
    for hd in range(N_BLOCKS):
        vb4_ref[pl.ds(hd, tm, stride=N_BLOCKS), :] = vb[:, hd * LANES:(hd + 1) * LANES]

    @pl.when(pl.program_id(0) % tiles_per_seq == tiles_per_seq - 1)
    def _():
        keep = kat_tail_ref.shape[2]
        kat_tail_ref[0] = kat[:, tm - keep:]
        vat_tail_ref[0] = va[tm - keep:].T


def _proj_specs(tm, n_pos_blocks, consts):
    row = lambda w: pl.BlockSpec((tm, w), lambda i: (i, 0))
    pos = pl.BlockSpec((tm, LANES), lambda i: (i % n_pos_blocks, 0))
    g1, w_in, gains, pool = consts
    return row, [row(D_MODEL), _const_spec(g1.shape), _const_spec(w_in.shape), _const_spec(gains.shape),
                 pos, pos, _const_spec(pool.shape)]


def _proj_sample(x2d, g1, w_in, gains, cos, sin, pool):
    n = x2d.shape[0]
    row, in_specs = _proj_specs(n, 1, (g1, w_in, gains, pool))
    out_dtypes = (BF16, F32, F32, BF16, F32, F32)
    return pl.pallas_call(
        functools.partial(_proj_kernel, tiles_per_seq=0),
        grid=(1,),
        in_specs=in_specs,
        out_specs=[row(WIDTH)] * N_GROUPS,
        out_shape=[jax.ShapeDtypeStruct((n, WIDTH), dt) for dt in out_dtypes],
        compiler_params=_params(1),
        name="proj_sample",
    )(x2d, g1, w_in, gains, cos, sin, pool)


def _proj_prompt(x2d, b, s, keep, g1, w_in, gains, cos, sin, pool, kgain_t, cos_t, sin_t, to_cast):
    n = x2d.shape[0]
    tm = 2 * ROW_TILE
    tps = s // tm
    steps = n // tm
    row, in_specs = _proj_specs(tm, tps, (g1, w_in, gains, pool))
    pos_t = pl.BlockSpec((HEAD_DIM // 2, tm), lambda i: (0, i % tps))
    assert all(w.shape[0] % (BF16_ROWS * steps) == 0 for w in to_cast)
    cast_specs = [pl.BlockSpec((w.shape[0] // steps, w.shape[1]), lambda i: (i, 0)) for w in to_cast]
    in_specs = in_specs + [_const_spec(kgain_t.shape), pos_t, pos_t] + cast_specs
    feat = pl.BlockSpec((1, WIDTH, tm), lambda i: (i // tps, 0, i % tps))
    tail = pl.BlockSpec((1, WIDTH, keep), lambda i: (i // tps, 0, 0))
    out_specs = [row(WIDTH), feat, row(WIDTH), row(WIDTH), feat,
                 pl.BlockSpec((tm * N_BLOCKS, LANES), lambda i: (i, 0)), tail, tail] + cast_specs
    out_shape = [jax.ShapeDtypeStruct((n, WIDTH), BF16),
                 jax.ShapeDtypeStruct((b, WIDTH, s), BF16),
                 jax.ShapeDtypeStruct((n, WIDTH), BF16),
                 jax.ShapeDtypeStruct((n, WIDTH), BF16),
                 jax.ShapeDtypeStruct((b, WIDTH, s), F32),
                 jax.ShapeDtypeStruct((n * N_BLOCKS, LANES), F32),
                 jax.ShapeDtypeStruct((b, WIDTH, keep), F32),
                 jax.ShapeDtypeStruct((b, WIDTH, keep), F32)]
    out_shape += [jax.ShapeDtypeStruct(w.shape, BF16) for w in to_cast]
    return pl.pallas_call(
        functools.partial(_proj_kernel, tiles_per_seq=tps),
        grid=(steps,),
        in_specs=in_specs,
        out_specs=out_specs,
        out_shape=out_shape,
        compiler_params=_params(1),
        name="proj_prompt",
    )(x2d, g1, w_in, gains, cos, sin, pool, kgain_t, cos_t, sin_t, *to_cast)


def _group_bias(f_ref):
    bias = _pair_bias(f_ref, GROUP_Q)[:, :GROUP_KEYS]
    qc = (lax.broadcasted_iota(jnp.int32, bias.shape, 0) % GROUP_Q) // CHUNK
    kc = lax.broadcasted_iota(jnp.int32, bias.shape, 1) // CHUNK
    return jnp.where((kc >= qc) & (kc <= qc + BAND_PAST_CHUNKS), bias, NEG_INF)


def _band_prompt_kernel(q_ref, kt_ref, v_ref, f_ref, *refs, lam_init):
    sample_in, lam_ref, subln_ref = refs[:10], refs[10], refs[11]
    o_ref, ya_s_ref, yb_s_ref, bias_ref, kt_heads_ref = refs[12:]
    n_groups = q_ref.shape[1] // GROUP_Q
    stages = _sample_mixers(*sample_in, f_ref, lam_ref, subln_ref, ya_s_ref, yb_s_ref,
                            pl.program_id(1) * BAND_PAIRS_PER_STEP, lam_init)
    side_job = {(0, (k + 1) * n_groups // 4): stage for k, stage in enumerate(stages)}

    def window(g):
        return max(0, g * GROUP_Q - BAND_PAST), (g + 1) * GROUP_Q

    for jj in range(BAND_PAIRS_PER_STEP):
        cols = slice(jj * LANES, (jj + 1) * LANES)
        bias_ref[jj] = _group_bias(f_ref.at[jj])
        kt = kt_ref[0, cols, :]
        feat = lax.broadcasted_iota(jnp.int32, kt.shape, 0)
        kt_heads_ref[jj, 0] = jnp.where(feat < HEAD_DIM, kt, jnp.zeros_like(kt))
        kt_heads_ref[jj, 1] = jnp.where(feat >= HEAD_DIM, kt, jnp.zeros_like(kt))

        def scores(g):
            lo, hi = window(g)
            q = q_ref[0, g * GROUP_Q:(g + 1) * GROUP_Q, cols]
            s = _dot(q, jnp.concatenate([kt_heads_ref[jj, 0, :, lo:hi], kt_heads_ref[jj, 1, :, lo:hi]], axis=1))
            s = jnp.concatenate([s[:, :hi - lo], s[:, hi - lo:]], axis=0)
            return s + bias_ref[jj, :, GROUP_KEYS - (hi - lo):]

        nxt = scores(0)
        for g in range(n_groups):
            s = nxt
            if g + 1 < n_groups:
                nxt = scores(g + 1)
            lo, hi = window(g)
            (p,), l = _softmax_parts([s])
            o = _dot(p.astype(BF16), v_ref[0, lo:hi, cols]) * (1.0 / l)
            o_ref[0, g * GROUP_Q:(g + 1) * GROUP_Q, cols] = _pick_halves(o).astype(o_ref.dtype)
            if (jj, g) in side_job:
                side_job[jj, g]()


def _band_prompt(qa, kat, va, f, new, caches, lam_vecs, subln, lam_init, t):
    b, s, _ = qa.shape
    pairs = BAND_PAIRS_PER_STEP
    assert new[0].shape[0] == b * t
    blk = pl.BlockSpec((1, s, pairs * LANES), lambda i, j: (i, 0, j))
    blk_t = pl.BlockSpec((1, pairs * LANES, s), lambda i, j: (i, j, 0))
    row = pl.BlockSpec((t, pairs * LANES), lambda i, j: (i, j))
    feat = lambda c: pl.BlockSpec((1, pairs * LANES, c.shape[2]), lambda i, j: (i, j, 0))
    whole = lambda c: pl.BlockSpec((1,) + c.shape[1:], lambda i, j: (i, 0, 0))
    cak, cav, cbk, cbv4 = caches
    return pl.pallas_call(
        functools.partial(_band_prompt_kernel, lam_init=lam_init),
        grid=(b, N_BLOCKS // pairs),
        in_specs=[blk, blk_t, blk, pl.BlockSpec((pairs, 2, BIAS_SPAN), lambda i, j: (j, 0, 0))]
        + [row] * 6 + [feat(cak), feat(cav), feat(cbk), whole(cbv4)]
        + [_const_spec(lam_vecs.shape), _const_spec(subln.shape)],
        out_specs=[blk, row, row],
        out_shape=[jax.ShapeDtypeStruct((b, s, WIDTH), BF16)] + [jax.ShapeDtypeStruct((b * t, WIDTH), BF16)] * 2,
        scratch_shapes=[pltpu.VMEM((pairs, 2 * GROUP_Q, GROUP_KEYS), F32), pltpu.VMEM((pairs, 2, LANES, s), BF16)],
        compiler_params=_params(2),
        name="band_prompt",
    )(qa, kat, va, f, *new, *caches, lam_vecs, subln)


def _lambda(lam_ref, lam_init):
    e1 = jnp.exp(jnp.sum(lam_ref[0:1, :] * lam_ref[1:2, :], axis=-1, keepdims=True))
    e2 = jnp.exp(jnp.sum(lam_ref[2:3, :] * lam_ref[3:4, :], axis=-1, keepdims=True))
    return e1 - e2 + lam_init


def _diff_out(o, subln, lam_init):
    return _rms(o, subln) * (1.0 - lam_init)


def _diff_prompt_kernel(q_ref, kt_ref, v4_ref, lam_ref, subln_ref, o_ref, kbf_ref, vbf_ref, *, lam_init):
    s_len = q_ref.shape[1]
    tq = Q_BLOCK_B
    n_blocks = s_len // tq
    lam = _lambda(lam_ref, lam_init)
    row = lax.broadcasted_iota(jnp.int32, (2 * tq, tq), 0) % tq
    col = lax.broadcasted_iota(jnp.int32, (2 * tq, tq), 1)
    diag_mask = jnp.where(col // CHUNK <= row // CHUNK, 0.0, NEG_INF).astype(F32)

    for jj in range(DIFF_HEADS_PER_STEP):
        cols = slice(jj * LANES, (jj + 1) * LANES)
        head = pl.program_id(1) * DIFF_HEADS_PER_STEP + jj
        kbf_ref[jj] = kt_ref[0, cols, :].astype(BF16)
        vbf_ref[jj] = v4_ref[0, pl.ds(head, s_len, stride=N_BLOCKS), :].astype(BF16)

        def scores(i):
            r0 = i * tq
            qs = _lane_halves(q_ref[0, r0:r0 + tq, cols])
            parts = [_dot(qs, kbf_ref[jj, :, r0:r0 + tq]) + diag_mask]
            if i > 0:
                parts.append(_dot(qs, kbf_ref[jj, :, 0:r0]))
            return parts

        nxt = scores(0)
        for i in range(n_blocks):
            r0 = i * tq
            parts = nxt
            if i + 1 < n_blocks:
                nxt = scores(i + 1)
            a, inv = _diff_combine(*_softmax_parts(parts), lam, tq)
            o = _dot(a[0], vbf_ref[jj, r0:r0 + tq, :])
            if i > 0:
                o = o + _dot(a[1], vbf_ref[jj, 0:r0, :])
            o_ref[0, r0:r0 + tq, cols] = _diff_out(o * inv, subln_ref[...], lam_init).astype(o_ref.dtype)


def _diff_prompt(qb, kbt, vb4, lam_vecs, subln, lam_init):
    b, s, _ = qb.shape
    heads = DIFF_HEADS_PER_STEP
    blk = pl.BlockSpec((1, s, heads * LANES), lambda i, j: (i, 0, j))
    return pl.pallas_call(
        functools.partial(_diff_prompt_kernel, lam_init=lam_init),
        grid=(b, N_BLOCKS // heads),
        in_specs=[blk, pl.BlockSpec((1, heads * LANES, s), lambda i, j: (i, j, 0)),
                  pl.BlockSpec((1, s * N_BLOCKS, LANES), lambda i, j: (i, 0, 0)),
                  _const_spec(lam_vecs.shape), _const_spec(subln.shape)],
        out_specs=blk,
        out_shape=jax.ShapeDtypeStruct((b, s, WIDTH), BF16),
        scratch_shapes=[pltpu.VMEM((heads, LANES, s), BF16), pltpu.VMEM((heads, s, LANES), BF16)],
        compiler_params=_params(2),
        name="diff_prompt",
    )(qb, kbt, vb4, lam_vecs, subln)


def _sample_mixers(qa_ref, ka_ref, va_ref, qb_ref, kb_ref, vb_ref, cakt_ref, cavt_ref, cbkt_ref, cbv4_ref,
                   f_ref, lam_ref, subln_ref, ya_ref, yb_ref, first_head, lam_init):
    t = qa_ref.shape[0]
    lc = cakt_ref.shape[2]
    past = cbkt_ref.shape[2]
    blocks = [slice(j * LANES, (j + 1) * LANES) for j in range(qa_ref.shape[1] // LANES)]
    st = {}

    def scores():
        sa = []
        for j, cols in enumerate(blocks):
            bias = _pair_bias(f_ref.at[j], t)
            qm = _lane_halves(qa_ref[:, cols])
            sa.append([_dot(qm, cakt_ref[0, cols, :].astype(BF16)) + bias[:, :lc],
                       _dot_nt(qm, ka_ref[:, cols].astype(BF16)) + bias[:, lc:lc + t]])
        sb = []
        for cols in blocks:
            qs = _lane_halves(qb_ref[:, cols])
            sb.append([_dot(qs, cbkt_ref[0, cols, :].astype(BF16)), _dot_nt(qs, kb_ref[:, cols].astype(BF16))])
        st["s"] = (sa, sb)

    def softmax():
        lam = _lambda(lam_ref, lam_init)
        sa, sb = st["s"]
        st["p"] = ([_softmax_parts(s) for s in sa], [_diff_combine(*_softmax_parts(s), lam, t) for s in sb])

    def output():
        pa, pb = st["p"]
        for cols, ((pc, pn), l) in zip(blocks, pa):
            o = _dot_nt(pc.astype(BF16), cavt_ref[0, cols, :].astype(BF16)) + _dot(pn.astype(BF16),
                                                                                 va_ref[:, cols].astype(BF16))
            ya_ref[:, cols] = _pick_halves(o * (1.0 / l)).astype(ya_ref.dtype)
        for j, (cols, ((ac, an), inv)) in enumerate(zip(blocks, pb)):
            vc = cbv4_ref[0, pl.ds(first_head + j, past, stride=N_BLOCKS), :].astype(BF16)
            o = (_dot(ac, vc) + _dot(an, vb_ref[:, cols].astype(BF16))) * inv
            yb_ref[:, cols] = _diff_out(o, subln_ref[...], lam_init).astype(yb_ref.dtype)

    return scores, softmax, output


def _merge_mlp_kernel(x_ref, ya_ref, yb_ref, g1_ref, wg_ref, bg_ref, wpa_ref, wpb_ref, wo_ref,
                      g2_ref, w1_ref, w2_ref, o_ref):
    x = x_ref[...]
    h = _rms(x, g1_ref[...]).astype(BF16)
    m = jax.nn.sigmoid(_dot(h, wg_ref[:, :D_MODEL]) + bg_ref[:, :D_MODEL]) * _dot(ya_ref[...], wpa_ref[...])
    m = m + jax.nn.sigmoid(_dot(h, wg_ref[:, D_MODEL:]) + bg_ref[:, D_MODEL:]) * _dot(yb_ref[...], wpb_ref[...])
    x1 = x + _dot(m.astype(BF16), wo_ref[...])
    hn = _rms(x1, g2_ref[...]).astype(BF16)
    acc = x1
    for c in range(D_FF // D_MODEL):
        cols = slice(c * D_MODEL, (c + 1) * D_MODEL)
        u = jnp.maximum(_dot(hn, w1_ref[:, cols]), 0.0)
        acc = acc + _dot((u * u).astype(BF16), w2_ref[cols, :])
    o_ref[...] = acc


def _merge_mlp(x2d, ya, yb, tm, g1, wg, bg, wpa, wpb, wo, g2, w1, w2):
    n = x2d.shape[0]
    row = lambda w: pl.BlockSpec((tm, w), lambda i: (i, 0))
    consts = (g1, wg, bg, wpa, wpb, wo, g2, w1, w2)
    return pl.pallas_call(
        _merge_mlp_kernel,
        grid=(n // tm,),
        in_specs=[row(D_MODEL), row(WIDTH), row(WIDTH)] + [_const_spec(c.shape) for c in consts],
        out_specs=row(D_MODEL),
        out_shape=jax.ShapeDtypeStruct((n, D_MODEL), F32),
        compiler_params=_params(1),
        name="merge_mlp",
    )(x2d, ya, yb, *consts)


def _rope_tables(pos):
    half = HEAD_DIM // 2
    inv_freq = ROPE_THETA ** (-jnp.arange(half, dtype=F32) / half)
    ang = pos.astype(F32)[:, None] * inv_freq[None, :]
    cos = jnp.cos(ang)
    sin = jnp.sin(ang)
    reps = LANES // HEAD_DIM
    return (jnp.tile(jnp.concatenate([cos, cos], axis=1), (1, reps)),
            jnp.tile(jnp.concatenate([-sin, sin], axis=1), (1, reps)), cos.T, sin.T)


def _band_offset_table(table):
    n_far = BAND_PAST + GROUP_Q - 1 - REL_PAST + 1
    n_fut = BIAS_SPAN - n_far - (N_REL - 1)
    h = table.shape[0]
    f = jnp.concatenate([jnp.broadcast_to(table[:, N_REL - 1:], (h, n_far)), table[:, N_REL - 2::-1],
                         jnp.broadcast_to(table[:, :1], (h, n_fut))], axis=1).astype(F32)
    return f.reshape(h // 2, 2, BIAS_SPAN)


def _pool_matrix():
    i = np.arange(MXU_DIM)
    return jnp.asarray((i[:, None] // HEAD_DIM == i[None, :] // HEAD_DIM) / HEAD_DIM, dtype=BF16)


def kernel(x_prompt, x_sample, cache_a_k, cache_a_v, cache_b_k, cache_b_v, ln1_g, w_in, qn_a, kn_a, rel_bias, qn_b, kn_b, lam_q1, lam_k1, lam_q2, lam_k2, subln_g, w_gate, b_gate, w_proj_a, w_proj_b, w_out, ln2_g, w_ff1, w_ff2):
    depth = w_in.shape[0]
    b, s, d = x_prompt.shape
    db, t, _ = x_sample.shape
    past_len = cache_b_k.shape[2]
    lc = cache_a_k.shape[2]
    keep = min(BAND_PAST, s)
    assert d == D_MODEL and s % (2 * ROW_TILE) == 0 and keep <= 2 * ROW_TILE and keep % LANES == 0
    assert lc == BAND_PAST and t <= GROUP_Q

    cos_p, sin_p, cos_pt, sin_pt = _rope_tables(jnp.arange(s))
    cos_s, sin_s = (jnp.tile(a, (db, 1)) for a in _rope_tables(past_len + jnp.arange(t))[:2])
    lane_const = lambda g: jnp.broadcast_to(g[:, None], (HEAD_DIM, LANES))
    pool = _pool_matrix()
    tile_heads = lambda g: jnp.tile(g, WIDTH // HEAD_DIM)

    xp = x_prompt.reshape(b * s, d)
    xs = x_sample.reshape(db * t, d)
    outs = [[] for _ in range(8)]
    for l in range(depth):
        lam_init = 0.8 - 0.6 * math.exp(-0.3 * l)
        g1 = ln1_g[l][None]
        g2 = ln2_g[l][None]
        gains = jnp.stack([tile_heads(qn_a[l]), tile_heads(kn_a[l]), tile_heads(qn_b[l]), tile_heads(kn_b[l])])
        lam_vecs = jnp.stack([lam_q1[l], lam_k1[l], lam_q2[l], lam_k2[l]])
        subln = subln_g[l][None]
        bias_f = _band_offset_table(rel_bias[l])
        w_in_l = w_in[l].astype(BF16)

        kgain_t = jnp.stack([lane_const(kn_a[l]), lane_const(kn_b[l])])
        later_w = (w_gate[l], w_proj_a[l], w_proj_b[l], w_out[l], w_ff1[l], w_ff2[l])
        proj_out = _proj_prompt(xp, b, s, keep, g1, w_in_l, gains, cos_p, sin_p, pool, kgain_t, cos_pt, sin_pt, later_w)
        qa, kat, va, qb, kbt, vb4, kat_tail, vat_tail = proj_out[:8]
        wg, wpa, wpb, wo, w1, w2 = proj_out[8:]
        merge_w = (g1, wg, b_gate[l][None], wpa, wpb, wo, g2, w1, w2)
        as3 = lambda a: a.reshape(b, s, WIDTH)
        new = _proj_sample(xs, g1, w_in_l, gains, cos_s, sin_s, pool)
        caches = (cache_a_k[l].transpose(0, 2, 3, 1).reshape(db, WIDTH, lc),
                  cache_a_v[l].transpose(0, 2, 3, 1).reshape(db, WIDTH, lc),
                  cache_b_k[l].transpose(0, 2, 3, 4, 1).reshape(db, WIDTH, past_len),
                  cache_b_v[l].reshape(db, past_len * N_BLOCKS, LANES))
        ya, ya_s, yb_s = _band_prompt(as3(qa), kat, as3(va), bias_f, new, caches, lam_vecs, subln, lam_init, t)
        yb = _diff_prompt(as3(qb), kbt, vb4.reshape(b, s * N_BLOCKS, LANES), lam_vecs, subln, lam_init)
        xp = _merge_mlp(xp, ya.reshape(b * s, WIDTH), yb.reshape(b * s, WIDTH), 2 * ROW_TILE, *merge_w)
        token_major_a = lambda a: a.reshape(b, N_HEADS_A, HEAD_DIM, keep).transpose(0, 3, 1, 2)
        outs[0].append(token_major_a(kat_tail))
        outs[1].append(token_major_a(vat_tail))
        outs[2].append(kbt.reshape(b, N_HEADS_B, 2, HEAD_DIM, s).transpose(0, 4, 1, 2, 3))
        outs[3].append(vb4.reshape(b, s, N_HEADS_B, 2 * HEAD_DIM))

        xs = _merge_mlp(xs, ya_s, yb_s, db * t, *merge_w)
        outs[4].append(new[1].reshape(db, t, N_HEADS_A, HEAD_DIM))
        outs[5].append(new[2].reshape(db, t, N_HEADS_A, HEAD_DIM))
        outs[6].append(new[4].reshape(db, t, N_HEADS_B, 2, HEAD_DIM))
        outs[7].append(new[5].reshape(db, t, N_HEADS_B, 2 * HEAD_DIM))

    return (xp.reshape(b, s, d), xs.reshape(db, t, d)) + tuple(jnp.stack(o) for o in outs)
```

```python
import functools
import math

import jax
import jax.numpy as jnp
import numpy as np
from jax import lax
from jax.experimental import pallas as pl
from jax.experimental.pallas import tpu as pltpu

D_MODEL = 1024
CHUNK = 64
HEAD_DIM = 64
N_HEADS_A = 8
N_HEADS_B = 4
BAND_PAST_CHUNKS = 8
BAND_PAST = BAND_PAST_CHUNKS * CHUNK
REL_FUT = CHUNK - 1
REL_PAST = 256
N_REL = REL_FUT + REL_PAST + 1
WIDTH = 512
N_GROUPS = 6
D_FF = 4 * D_MODEL
ROPE_THETA = 10000.0
EPS = 1e-6
NEG_INF = -1e30
LOG2E = math.log2(math.e)
Q_SCALE = HEAD_DIM ** -0.5 * LOG2E

LANES = 128
BF16_ROWS = 16
N_BLOCKS = WIDTH // LANES
MXU_DIM = 256
GROUP_CHUNKS = 2
GROUP_Q = GROUP_CHUNKS * CHUNK
GROUP_KEYS = BAND_PAST + GROUP_Q
BIAS_SPAN = GROUP_KEYS + GROUP_Q
BAND_PAIRS_PER_STEP = 2
Q_BLOCK_B = 256
DIFF_HEADS_PER_STEP = 1
ROW_TILE = 512
VMEM_LIMIT = 60 * 1024 * 1024

BF16 = jnp.bfloat16
F32 = jnp.float32


def _dot(a, b):
    return jnp.dot(a, b, preferred_element_type=F32)


def _dot_nt(a, b):
    return lax.dot_general(a, b, (((1,), (1,)), ((), ())), preferred_element_type=F32)


def _rms(x, g):
    return x * lax.rsqrt(jnp.mean(x * x, axis=-1, keepdims=True) + EPS) * g


def _const_spec(shape):
    nd = len(shape)
    return pl.BlockSpec(shape, lambda *_: (0,) * nd, pipeline_mode=pl.Buffered(1))


def _params(n_grid_axes):
    return pltpu.CompilerParams(dimension_semantics=("arbitrary",) * n_grid_axes, vmem_limit_bytes=VMEM_LIMIT)


def _lane_halves(q):
    lane = lax.broadcasted_iota(jnp.int32, q.shape, 1)
    zero = jnp.zeros_like(q)
    return jnp.concatenate([jnp.where(lane < HEAD_DIM, q, zero), jnp.where(lane >= HEAD_DIM, q, zero)], axis=0)


def _pick_halves(o):
    m = o.shape[0] // 2
    lane = lax.broadcasted_iota(jnp.int32, (m, LANES), 1)
    return jnp.where(lane < HEAD_DIM, o[:m], o[m:])


def _softmax_parts(parts):
    m = functools.reduce(jnp.maximum, [jnp.max(s, axis=-1, keepdims=True) for s in parts])
    ps = [jnp.exp2(s - m) for s in parts]
    l = functools.reduce(lambda a, b: a + b, [jnp.sum(p, axis=-1, keepdims=True) for p in ps])
    return ps, l


def _diff_combine(ps, l, lam, t):
    c = lam * l[:t] / l[t:]
    return [(p[:t] - p[t:] * c).astype(BF16) for p in ps], 1.0 / l[:t]


def _pair_bias(f_ref, rows):
    def one(hd):
        x = jnp.broadcast_to(f_ref[hd:hd + 1, :], (rows, BIAS_SPAN))
        return pltpu.roll(x, BIAS_SPAN - GROUP_Q + 1, 1, stride=1, stride_axis=0)

    return jnp.concatenate([one(0), one(1)], axis=0) * LOG2E


def _proj_kernel(x_ref, g1_ref, w_ref, gains_ref, cos_ref, sin_ref, pool_ref, *refs, tiles_per_seq):
    h = _rms(x_ref[...], g1_ref[...]).astype(BF16)
    cos = jnp.concatenate([cos_ref[...]] * N_BLOCKS, axis=1)
    sin = jnp.concatenate([sin_ref[...]] * N_BLOCKS, axis=1)
    pool = pool_ref[...]

    def group(i):
        return _dot(h, w_ref[:, i * WIDTH:(i + 1) * WIDTH])

    def head_norm(z, i):
        zz = (z * z).astype(BF16)
        ms = jnp.concatenate([_dot(zz[:, k * MXU_DIM:(k + 1) * MXU_DIM], pool) for k in range(WIDTH // MXU_DIM)],
                             axis=1)
        return z * lax.rsqrt(ms + EPS) * gains_ref[i:i + 1, :]

    def rope(z):
        lane = lax.broadcasted_iota(jnp.int32, z.shape, 1)
        half = HEAD_DIM // 2
        partner = jnp.where(lane % HEAD_DIM < half, pltpu.roll(z, WIDTH - half, 1), pltpu.roll(z, half, 1))
        return z * cos + partner * sin

    if tiles_per_seq == 0:
        qa_ref, ka_ref, va_ref, qb_ref, kb_ref, vb_ref = refs
        qa_ref[...] = (head_norm(group(0), 0) * Q_SCALE).astype(BF16)
        ka_ref[...] = head_norm(group(1), 1)
        va_ref[...] = group(2)
        qb_ref[...] = (rope(head_norm(group(3), 2)) * Q_SCALE).astype(BF16)
        kb_ref[...] = rope(head_norm(group(4), 3))
        vb_ref[...] = group(5)
        return

    kgain_ref, cos_t_ref, sin_t_ref = refs[:3]
    n_cast = (len(refs) - 3 - 8) // 2
    qa_ref, kat_ref, va_ref, qb_ref, kbt_ref, vb4_ref, kat_tail_ref, vat_tail_ref = refs[3 + n_cast:11 + n_cast]
    for src_ref, dst_ref in zip(refs[3:3 + n_cast], refs[11 + n_cast:]):
        dst_ref[...] = src_ref[...].astype(BF16)
    tm = x_ref.shape[0]
    n_heads = WIDTH // HEAD_DIM

    def head_norm_t(zt, i):
        z3 = zt.reshape(n_heads, HEAD_DIM, tm)
        ms = jnp.mean(z3 * z3, axis=1, keepdims=True)
        gain = jnp.concatenate([kgain_ref[i]] * (tm // LANES), axis=1)
        return z3 * lax.rsqrt(ms + EPS) * gain[None]

    def rope_t(z3):
        half = HEAD_DIM // 2
        cos_t = cos_t_ref[...][None]
        sin_t = sin_t_ref[...][None]
        x1, x2 = z3[:, :half], z3[:, half:]
        return jnp.concatenate([x1 * cos_t - x2 * sin_t, x2 * cos_t + x1 * sin_t], axis=1)

    kbt_ref[0] = rope_t(head_norm_t(group(4).T, 1)).reshape(WIDTH, tm)
    kat = head_norm_t(group(1).T, 0).reshape(WIDTH, tm)
    kat_ref[0] = kat.astype(BF16)
    va = group(2)
    va_ref[...] = va.astype(BF16)
    qb_ref[...] = (rope(head_norm(group(3), 2)) * Q_SCALE).astype(BF16)
    qa_ref[...] = (head_norm(group(0), 0) * Q_SCALE).astype(BF16)
    vb = group(5)
    for hd in range(N_BLOCKS):
        vb4_ref[pl.ds(hd, tm, stride=N_BLOCKS), :] = vb[:, hd * LANES:(hd + 1) * LANES]

    @pl.when(pl.program_id(0) % tiles_per_seq == tiles_per_seq - 1)
    def _():
        keep = kat_tail_ref.shape[2]
        kat_tail_ref[0] = kat[:, tm - keep:]
        vat_tail_ref[0] = va[tm - keep:].T


def _proj_specs(tm, n_pos_blocks, consts):
    row = lambda w: pl.BlockSpec((tm, w), lambda i: (i, 0))
    pos = pl.BlockSpec((tm, LANES), lambda i: (i % n_pos_blocks, 0))
    g1, w_in, gains, pool = consts
    return row, [row(D_MODEL), _const_spec(g1.shape), _const_spec(w_in.shape), _const_spec(gains.shape),
                 pos, pos, _const_spec(pool.shape)]


def _proj_sample(x2d, g1, w_in, gains, cos, sin, pool):
    n = x2d.shape[0]
    row, in_specs = _proj_specs(n, 1, (g1, w_in, gains, pool))
    out_dtypes = (BF16, F32, F32, BF16, F32, F32)
    return pl.pallas_call(
        functools.partial(_proj_kernel, tiles_per_seq=0),
        grid=(1,),
        in_specs=in_specs,
        out_specs=[row(WIDTH)] * N_GROUPS,
        out_shape=[jax.ShapeDtypeStruct((n, WIDTH), dt) for dt in out_dtypes],
        compiler_params=_params(1),
        name="proj_sample",
    )(x2d, g1, w_in, gains, cos, sin, pool)


def _proj_prompt(x2d, b, s, keep, g1, w_in, gains, cos, sin, pool, kgain_t, cos_t, sin_t, to_cast):
    n = x2d.shape[0]
    tm = 2 * ROW_TILE
    tps = s // tm
    steps = n // tm
    row, in_specs = _proj_specs(tm, tps, (g1, w_in, gains, pool))
    pos_t = pl.BlockSpec((HEAD_DIM // 2, tm), lambda i: (0, i % tps))
    assert all(w.shape[0] % (BF16_ROWS * steps) == 0 for w in to_cast)
    cast_specs = [pl.BlockSpec((w.shape[0] // steps, w.shape[1]), lambda i: (i, 0)) for w in to_cast]
    in_specs = in_specs + [_const_spec(kgain_t.shape), pos_t, pos_t] + cast_specs
    feat = pl.BlockSpec((1, WIDTH, tm), lambda i: (i // tps, 0, i % tps))
    tail = pl.BlockSpec((1, WIDTH, keep), lambda i: (i // tps, 0, 0))
    out_specs = [row(WIDTH), feat, row(WIDTH), row(WIDTH), feat,
                 pl.BlockSpec((tm * N_BLOCKS, LANES), lambda i: (i, 0)), tail, tail] + cast_specs
    out_shape = [jax.ShapeDtypeStruct((n, WIDTH), BF16),
                 jax.ShapeDtypeStruct((b, WIDTH, s), BF16),
                 jax.ShapeDtypeStruct((n, WIDTH), BF16),
                 jax.ShapeDtypeStruct((n, WIDTH), BF16),
                 jax.ShapeDtypeStruct((b, WIDTH, s), F32),
                 jax.ShapeDtypeStruct((n * N_BLOCKS, LANES), F32),
                 jax.ShapeDtypeStruct((b, WIDTH, keep), F32),
                 jax.ShapeDtypeStruct((b, WIDTH, keep), F32)]
    out_shape += [jax.ShapeDtypeStruct(w.shape, BF16) for w in to_cast]
    return pl.pallas_call(
        functools.partial(_proj_kernel, tiles_per_seq=tps),
        grid=(steps,),
        in_specs=in_specs,
        out_specs=out_specs,
        out_shape=out_shape,
        compiler_params=_params(1),
        name="proj_prompt",
    )(x2d, g1, w_in, gains, cos, sin, pool, kgain_t, cos_t, sin_t, *to_cast)


def _group_bias(f_ref):
    bias = _pair_bias(f_ref, GROUP_Q)[:, :GROUP_KEYS]
    qc = (lax.broadcasted_iota(jnp.int32, bias.shape, 0) % GROUP_Q) // CHUNK
    kc = lax.broadcasted_iota(jnp.int32, bias.shape, 1) // CHUNK
    return jnp.where((kc >= qc) & (kc <= qc + BAND_PAST_CHUNKS), bias, NEG_INF)


def _band_prompt_kernel(q_ref, kt_ref, v_ref, f_ref, *refs, lam_init):
    sample_in, lam_ref, subln_ref = refs[:10], refs[10], refs[11]
    o_ref, ya_s_ref, yb_s_ref, bias_ref, kt_heads_ref = refs[12:]
    n_groups = q_ref.shape[1] // GROUP_Q
    stages = _sample_mixers(*sample_in, f_ref, lam_ref, subln_ref, ya_s_ref, yb_s_ref,
                            pl.program_id(1) * BAND_PAIRS_PER_STEP, lam_init)
    side_job = {(0, (k + 1) * n_groups // 4): stage for k, stage in enumerate(stages)}

    def window(g):
        return max(0, g * GROUP_Q - BAND_PAST), (g + 1) * GROUP_Q

    for jj in range(BAND_PAIRS_PER_STEP):
        cols = slice(jj * LANES, (jj + 1) * LANES)
        pair = pl.program_id(1) * BAND_PAIRS_PER_STEP + jj

        @pl.when(pl.program_id(0) == 0)
        def _():
            bias_ref[pair] = _group_bias(f_ref.at[jj])

        kt = kt_ref[0, cols, :]
        feat = lax.broadcasted_iota(jnp.int32, kt.shape, 0)
        kt_heads_ref[jj, 0] = jnp.where(feat < HEAD_DIM, kt, jnp.zeros_like(kt))
        kt_heads_ref[jj, 1] = jnp.where(feat >= HEAD_DIM, kt, jnp.zeros_like(kt))

        def scores(g):
            lo, hi = window(g)
            q = q_ref[0, g * GROUP_Q:(g + 1) * GROUP_Q, cols]
            s = _dot(q, jnp.concatenate([kt_heads_ref[jj, 0, :, lo:hi], kt_heads_ref[jj, 1, :, lo:hi]], axis=1))
            s = jnp.concatenate([s[:, :hi - lo], s[:, hi - lo:]], axis=0)
            return s + bias_ref[pair, :, GROUP_KEYS - (hi - lo):]

        nxt = scores(0)
        for g in range(n_groups):
            s = nxt
            if g + 1 < n_groups:
                nxt = scores(g + 1)
            lo, hi = window(g)
            (p,), l = _softmax_parts([s])
            o = _dot(p.astype(BF16), v_ref[0, lo:hi, cols]) * (1.0 / l)
            o_ref[0, g * GROUP_Q:(g + 1) * GROUP_Q, cols] = _pick_halves(o).astype(o_ref.dtype)
            if (jj, g) in side_job:
                side_job[jj, g]()


def _band_prompt(qa, kat, va, f, new, caches, lam_vecs, subln, lam_init, t):
    b, s, _ = qa.shape
    pairs = BAND_PAIRS_PER_STEP
    assert new[0].shape[0] == b * t
    blk = pl.BlockSpec((1, s, pairs * LANES), lambda i, j: (i, 0, j))
    blk_t = pl.BlockSpec((1, pairs * LANES, s), lambda i, j: (i, j, 0))
    row = pl.BlockSpec((t, pairs * LANES), lambda i, j: (i, j))
    feat = lambda c: pl.BlockSpec((1, pairs * LANES, c.shape[2]), lambda i, j: (i, j, 0))
    whole = lambda c: pl.BlockSpec((1,) + c.shape[1:], lambda i, j: (i, 0, 0))
    cak, cav, cbk, cbv4 = caches
    return pl.pallas_call(
        functools.partial(_band_prompt_kernel, lam_init=lam_init),
        grid=(b, N_BLOCKS // pairs),
        in_specs=[blk, blk_t, blk, pl.BlockSpec((pairs, 2, BIAS_SPAN), lambda i, j: (j, 0, 0))]
        + [row] * 6 + [feat(cak), feat(cav), feat(cbk), whole(cbv4)]
        + [_const_spec(lam_vecs.shape), _const_spec(subln.shape)],
        out_specs=[blk, row, row],
        out_shape=[jax.ShapeDtypeStruct((b, s, WIDTH), BF16)] + [jax.ShapeDtypeStruct((b * t, WIDTH), BF16)] * 2,
        scratch_shapes=[pltpu.VMEM((N_BLOCKS, 2 * GROUP_Q, GROUP_KEYS), F32), pltpu.VMEM((pairs, 2, LANES, s), BF16)],
        compiler_params=_params(2),
        name="band_prompt",
    )(qa, kat, va, f, *new, *caches, lam_vecs, subln)


def _lambda(lam_ref, lam_init):
    e1 = jnp.exp(jnp.sum(lam_ref[0:1, :] * lam_ref[1:2, :], axis=-1, keepdims=True))
    e2 = jnp.exp(jnp.sum(lam_ref[2:3, :] * lam_ref[3:4, :], axis=-1, keepdims=True))
    return e1 - e2 + lam_init


def _diff_out(o, subln, lam_init):
    return _rms(o, subln) * (1.0 - lam_init)


def _diff_prompt_kernel(q_ref, kt_ref, v4_ref, lam_ref, subln_ref, o_ref, kbf_ref, vbf_ref, *, lam_init):
    s_len = q_ref.shape[1]
    tq = Q_BLOCK_B
    n_blocks = s_len // tq
    lam = _lambda(lam_ref, lam_init)
    row = lax.broadcasted_iota(jnp.int32, (2 * tq, tq), 0) % tq
    col = lax.broadcasted_iota(jnp.int32, (2 * tq, tq), 1)
    diag_mask = jnp.where(col // CHUNK <= row // CHUNK, 0.0, NEG_INF).astype(F32)

    for jj in range(DIFF_HEADS_PER_STEP):
        cols = slice(jj * LANES, (jj + 1) * LANES)
        head = pl.program_id(1) * DIFF_HEADS_PER_STEP + jj
        kbf_ref[jj] = kt_ref[0, cols, :].astype(BF16)
        vbf_ref[jj] = v4_ref[0, pl.ds(head, s_len, stride=N_BLOCKS), :].astype(BF16)

        def scores(i):
            r0 = i * tq
            qs = _lane_halves(q_ref[0, r0:r0 + tq, cols])
            parts = [_dot(qs, kbf_ref[jj, :, r0:r0 + tq]) + diag_mask]
            if i > 0:
                parts.append(_dot(qs, kbf_ref[jj, :, 0:r0]))
            return parts

        nxt = scores(0)
        for i in range(n_blocks):
            r0 = i * tq
            parts = nxt
            if i + 1 < n_blocks:
                nxt = scores(i + 1)
            a, inv = _diff_combine(*_softmax_parts(parts), lam, tq)
            o = _dot(a[0], vbf_ref[jj, r0:r0 + tq, :])
            if i > 0:
                o = o + _dot(a[1], vbf_ref[jj, 0:r0, :])
            o_ref[0, r0:r0 + tq, cols] = _diff_out(o * inv, subln_ref[...], lam_init).astype(o_ref.dtype)


def _diff_prompt(qb, kbt, vb4, lam_vecs, subln, lam_init):
    b, s, _ = qb.shape
    heads = DIFF_HEADS_PER_STEP
    blk = pl.BlockSpec((1, s, heads * LANES), lambda i, j: (i, 0, j))
    return pl.pallas_call(
        functools.partial(_diff_prompt_kernel, lam_init=lam_init),
        grid=(b, N_BLOCKS // heads),
        in_specs=[blk, pl.BlockSpec((1, heads * LANES, s), lambda i, j: (i, j, 0)),
                  pl.BlockSpec((1, s * N_BLOCKS, LANES), lambda i, j: (i, 0, 0)),
                  _const_spec(lam_vecs.shape), _const_spec(subln.shape)],
        out_specs=blk,
        out_shape=jax.ShapeDtypeStruct((b, s, WIDTH), BF16),
        scratch_shapes=[pltpu.VMEM((heads, LANES, s), BF16), pltpu.VMEM((heads, s, LANES), BF16)],
        compiler_params=_params(2),
        name="diff_prompt",
    )(qb, kbt, vb4, lam_vecs, subln)


def _sample_mixers(qa_ref, ka_ref, va_ref, qb_ref, kb_ref, vb_ref, cakt_ref, cavt_ref, cbkt_ref, cbv4_ref,
                   f_ref, lam_ref, subln_ref, ya_ref, yb_ref, first_head, lam_init):
    t = qa_ref.shape[0]
    lc = cakt_ref.shape[2]
    past = cbkt_ref.shape[2]
    blocks = [slice(j * LANES, (j + 1) * LANES) for j in range(qa_ref.shape[1] // LANES)]
    st = {}

    def scores():
        sa = []
        for j, cols in enumerate(blocks):
            bias = _pair_bias(f_ref.at[j], t)
            qm = _lane_halves(qa_ref[:, cols])
            sa.append([_dot(qm, cakt_ref[0, cols, :].astype(BF16)) + bias[:, :lc],
                       _dot_nt(qm, ka_ref[:, cols].astype(BF16)) + bias[:, lc:lc + t]])
        sb = []
        for cols in blocks:
            qs = _lane_halves(qb_ref[:, cols])
            sb.append([_dot(qs, cbkt_ref[0, cols, :].astype(BF16)), _dot_nt(qs, kb_ref[:, cols].astype(BF16))])
        st["s"] = (sa, sb)

    def softmax():
        lam = _lambda(lam_ref, lam_init)
        sa, sb = st["s"]
        st["p"] = ([_softmax_parts(s) for s in sa], [_diff_combine(*_softmax_parts(s), lam, t) for s in sb])

    def output():
        pa, pb = st["p"]
        for cols, ((pc, pn), l) in zip(blocks, pa):
            o = _dot_nt(pc.astype(BF16), cavt_ref[0, cols, :].astype(BF16)) + _dot(pn.astype(BF16),
                                                                                 va_ref[:, cols].astype(BF16))
            ya_ref[:, cols] = _pick_halves(o * (1.0 / l)).astype(ya_ref.dtype)
        for j, (cols, ((ac, an), inv)) in enumerate(zip(blocks, pb)):
            vc = cbv4_ref[0, pl.ds(first_head + j, past, stride=N_BLOCKS), :].astype(BF16)
            o = (_dot(ac, vc) + _dot(an, vb_ref[:, cols].astype(BF16))) * inv
            yb_ref[:, cols] = _diff_out(o, subln_ref[...], lam_init).astype(yb_ref.dtype)

    return scores, softmax, output


def _merge_mlp_kernel(x_ref, ya_ref, yb_ref, g1_ref, wg_ref, bg_ref, wpa_ref, wpb_ref, wo_ref,
                      g2_ref, w1_ref, w2_ref, o_ref):
    x = x_ref[...]
    h = _rms(x, g1_ref[...]).astype(BF16)
    m = jax.nn.sigmoid(_dot(h, wg_ref[:, :D_MODEL]) + bg_ref[:, :D_MODEL]) * _dot(ya_ref[...], wpa_ref[...])
    m = m + jax.nn.sigmoid(_dot(h, wg_ref[:, D_MODEL:]) + bg_ref[:, D_MODEL:]) * _dot(yb_ref[...], wpb_ref[...])
    x1 = x + _dot(m.astype(BF16), wo_ref[...])
    hn = _rms(x1, g2_ref[...]).astype(BF16)
    acc = x1
    for c in range(D_FF // D_MODEL):
        cols = slice(c * D_MODEL, (c + 1) * D_MODEL)
        u = jnp.maximum(_dot(hn, w1_ref[:, cols]), 0.0)
        acc = acc + _dot((u * u).astype(BF16), w2_ref[cols, :])
    o_ref[...] = acc


def _merge_mlp(x2d, ya, yb, tm, g1, wg, bg, wpa, wpb, wo, g2, w1, w2):
    n = x2d.shape[0]
    row = lambda w: pl.BlockSpec((tm, w), lambda i: (i, 0))
    consts = (g1, wg, bg, wpa, wpb, wo, g2, w1, w2)
    return pl.pallas_call(
        _merge_mlp_kernel,
        grid=(n // tm,),
        in_specs=[row(D_MODEL), row(WIDTH), row(WIDTH)] + [_const_spec(c.shape) for c in consts],
        out_specs=row(D_MODEL),
        out_shape=jax.ShapeDtypeStruct((n, D_MODEL), F32),
        compiler_params=_params(1),
        name="merge_mlp",
    )(x2d, ya, yb, *consts)


def _rope_tables(pos):
    half = HEAD_DIM // 2
    inv_freq = ROPE_THETA ** (-jnp.arange(half, dtype=F32) / half)
    ang = pos.astype(F32)[:, None] * inv_freq[None, :]
    cos = jnp.cos(ang)
    sin = jnp.sin(ang)
    reps = LANES // HEAD_DIM
    return (jnp.tile(jnp.concatenate([cos, cos], axis=1), (1, reps)),
            jnp.tile(jnp.concatenate([-sin, sin], axis=1), (1, reps)), cos.T, sin.T)


def _band_offset_table(table):
    n_far = BAND_PAST + GROUP_Q - 1 - REL_PAST + 1
    n_fut = BIAS_SPAN - n_far - (N_REL - 1)
    h = table.shape[0]
    f = jnp.concatenate([jnp.broadcast_to(table[:, N_REL - 1:], (h, n_far)), table[:, N_REL - 2::-1],
                         jnp.broadcast_to(table[:, :1], (h, n_fut))], axis=1).astype(F32)
    return f.reshape(h // 2, 2, BIAS_SPAN)


def _pool_matrix():
    i = np.arange(MXU_DIM)
    return jnp.asarray((i[:, None] // HEAD_DIM == i[None, :] // HEAD_DIM) / HEAD_DIM, dtype=BF16)


def kernel(x_prompt, x_sample, cache_a_k, cache_a_v, cache_b_k, cache_b_v, ln1_g, w_in, qn_a, kn_a, rel_bias, qn_b, kn_b, lam_q1, lam_k1, lam_q2, lam_k2, subln_g, w_gate, b_gate, w_proj_a, w_proj_b, w_out, ln2_g, w_ff1, w_ff2):
    depth = w_in.shape[0]
    b, s, d = x_prompt.shape
    db, t, _ = x_sample.shape
    past_len = cache_b_k.shape[2]
    lc = cache_a_k.shape[2]
    keep = min(BAND_PAST, s)
    assert d == D_MODEL and s % (2 * ROW_TILE) == 0 and keep <= 2 * ROW_TILE and keep % LANES == 0
    assert lc == BAND_PAST and t <= GROUP_Q

    cos_p, sin_p, cos_pt, sin_pt = _rope_tables(jnp.arange(s))
    cos_s, sin_s = (jnp.tile(a, (db, 1)) for a in _rope_tables(past_len + jnp.arange(t))[:2])
    lane_const = lambda g: jnp.broadcast_to(g[:, None], (HEAD_DIM, LANES))
    pool = _pool_matrix()
    tile_heads = lambda g: jnp.tile(g, WIDTH // HEAD_DIM)

    xp = x_prompt.reshape(b * s, d)
    xs = x_sample.reshape(db * t, d)
    outs = [[] for _ in range(8)]
    for l in range(depth):
        lam_init = 0.8 - 0.6 * math.exp(-0.3 * l)
        g1 = ln1_g[l][None]
        g2 = ln2_g[l][None]
        gains = jnp.stack([tile_heads(qn_a[l]), tile_heads(kn_a[l]), tile_heads(qn_b[l]), tile_heads(kn_b[l])])
        lam_vecs = jnp.stack([lam_q1[l], lam_k1[l], lam_q2[l], lam_k2[l]])
        subln = subln_g[l][None]
        bias_f = _band_offset_table(rel_bias[l])
        w_in_l = w_in[l].astype(BF16)

        kgain_t = jnp.stack([lane_const(kn_a[l]), lane_const(kn_b[l])])
        later_w = (w_gate[l], w_proj_a[l], w_proj_b[l], w_out[l], w_ff1[l], w_ff2[l])
        proj_out = _proj_prompt(xp, b, s, keep, g1, w_in_l, gains, cos_p, sin_p, pool, kgain_t, cos_pt, sin_pt, later_w)
        qa, kat, va, qb, kbt, vb4, kat_tail, vat_tail = proj_out[:8]
        wg, wpa, wpb, wo, w1, w2 = proj_out[8:]
        merge_w = (g1, wg, b_gate[l][None], wpa, wpb, wo, g2, w1, w2)
        as3 = lambda a: a.reshape(b, s, WIDTH)
        new = _proj_sample(xs, g1, w_in_l, gains, cos_s, sin_s, pool)
        caches = (cache_a_k[l].transpose(0, 2, 3, 1).reshape(db, WIDTH, lc),
                  cache_a_v[l].transpose(0, 2, 3, 1).reshape(db, WIDTH, lc),
                  cache_b_k[l].transpose(0, 2, 3, 4, 1).reshape(db, WIDTH, past_len),
                  cache_b_v[l].reshape(db, past_len * N_BLOCKS, LANES))
        ya, ya_s, yb_s = _band_prompt(as3(qa), kat, as3(va), bias_f, new, caches, lam_vecs, subln, lam_init, t)
        yb = _diff_prompt(as3(qb), kbt, vb4.reshape(b, s * N_BLOCKS, LANES), lam_vecs, subln, lam_init)
        xp = _merge_mlp(xp, ya.reshape(b * s, WIDTH), yb.reshape(b * s, WIDTH), 2 * ROW_TILE, *merge_w)
        token_major_a = lambda a: a.reshape(b, N_HEADS_A, HEAD_DIM, keep).transpose(0, 3, 1, 2)
        outs[0].append(token_major_a(kat_tail))
        outs[1].append(token_major_a(vat_tail))
        outs[2].append(kbt.reshape(b, N_HEADS_B, 2, HEAD_DIM, s).transpose(0, 4, 1, 2, 3))
        outs[3].append(vb4.reshape(b, s, N_HEADS_B, 2 * HEAD_DIM))

        xs = _merge_mlp(xs, ya_s, yb_s, db * t, *merge_w)
        outs[4].append(new[1].reshape(db, t, N_HEADS_A, HEAD_DIM))
        outs[5].append(new[2].reshape(db, t, N_HEADS_A, HEAD_DIM))
        outs[6].append(new[4].reshape(db, t, N_HEADS_B, 2, HEAD_DIM))
        outs[7].append(new[5].reshape(db, t, N_HEADS_B, 2 * HEAD_DIM))

    return (xp.reshape(b, s, d), xs.reshape(db, t, d)) + tuple(jnp.stack(o) for o in outs)
```

```python
import functools
import math

import jax
import jax.numpy as jnp
import numpy as np
from jax import lax
from jax.experimental import pallas as pl
from jax.experimental.pallas import tpu as pltpu

D_MODEL = 1024
CHUNK = 64
HEAD_DIM = 64
N_HEADS_A = 8
N_HEADS_B = 4
BAND_PAST_CHUNKS = 8
BAND_PAST = BAND_PAST_CHUNKS * CHUNK
REL_FUT = CHUNK - 1
REL_PAST = 256
N_REL = REL_FUT + REL_PAST + 1
WIDTH = 512
N_GROUPS = 6
D_FF = 4 * D_MODEL
ROPE_THETA = 10000.0
EPS = 1e-6
NEG_INF = -1e30
LOG2E = math.log2(math.e)
Q_SCALE = HEAD_DIM ** -0.5 * LOG2E

LANES = 128
BF16_ROWS = 16
N_BLOCKS = WIDTH // LANES
MXU_DIM = 256
GROUP_CHUNKS = 2
GROUP_Q = GROUP_CHUNKS * CHUNK
GROUP_KEYS = BAND_PAST + GROUP_Q
BIAS_SPAN = GROUP_KEYS + GROUP_Q
BAND_PAIRS_PER_STEP = 2
Q_BLOCK_B = 256
DIFF_HEADS_PER_STEP = 1
ROW_TILE = 512
VMEM_LIMIT = 60 * 1024 * 1024

BF16 = jnp.bfloat16
F32 = jnp.float32


def _dot(a, b):
    return jnp.dot(a, b, preferred_element_type=F32)


def _dot_nt(a, b):
    return lax.dot_general(a, b, (((1,), (1,)), ((), ())), preferred_element_type=F32)


def _rms(x, g):
    return x * lax.rsqrt(jnp.mean(x * x, axis=-1, keepdims=True) + EPS) * g


def _const_spec(shape):
    nd = len(shape)
    return pl.BlockSpec(shape, lambda *_: (0,) * nd, pipeline_mode=pl.Buffered(1))


def _params(n_grid_axes):
    return pltpu.CompilerParams(dimension_semantics=("arbitrary",) * n_grid_axes, vmem_limit_bytes=VMEM_LIMIT)


def _lane_halves(q):
    lane = lax.broadcasted_iota(jnp.int32, q.shape, 1)
    zero = jnp.zeros_like(q)
    return jnp.concatenate([jnp.where(lane < HEAD_DIM, q, zero), jnp.where(lane >= HEAD_DIM, q, zero)], axis=0)


def _pick_halves(o):
    m = o.shape[0] // 2
    lane = lax.broadcasted_iota(jnp.int32, (m, LANES), 1)
    return jnp.where(lane < HEAD_DIM, o[:m], o[m:])


def _softmax_parts(parts):
    m = functools.reduce(jnp.maximum, [jnp.max(s, axis=-1, keepdims=True) for s in parts])
    ps = [jnp.exp2(s - m) for s in parts]
    l = functools.reduce(lambda a, b: a + b, [jnp.sum(p, axis=-1, keepdims=True) for p in ps])
    return ps, l


def _diff_combine(ps, l, lam, t):
    c = lam * l[:t] / l[t:]
    return [(p[:t] - p[t:] * c).astype(BF16) for p in ps], 1.0 / l[:t]


def _pair_bias(f_ref, rows):
    def one(hd):
        x = jnp.broadcast_to(f_ref[hd:hd + 1, :], (rows, BIAS_SPAN))
        return pltpu.roll(x, BIAS_SPAN - GROUP_Q + 1, 1, stride=1, stride_axis=0)

    return jnp.concatenate([one(0), one(1)], axis=0) * LOG2E


def _proj_kernel(x_ref, g1_ref, w_ref, gains_ref, cos_ref, sin_ref, pool_ref, *refs, tiles_per_seq):
    h = _rms(x_ref[...], g1_ref[...]).astype(BF16)
    cos = jnp.concatenate([cos_ref[...]] * N_BLOCKS, axis=1)
    sin = jnp.concatenate([sin_ref[...]] * N_BLOCKS, axis=1)
    pool = pool_ref[...]

    def group(i):
        return _dot(h, w_ref[:, i * WIDTH:(i + 1) * WIDTH])

    def head_norm(z, i):
        zz = (z * z).astype(BF16)
        ms = jnp.concatenate([_dot(zz[:, k * MXU_DIM:(k + 1) * MXU_DIM], pool) for k in range(WIDTH // MXU_DIM)],
                             axis=1)
        return z * lax.rsqrt(ms + EPS) * gains_ref[i:i + 1, :]

    def rope(z):
        lane = lax.broadcasted_iota(jnp.int32, z.shape, 1)
        half = HEAD_DIM // 2
        partner = jnp.where(lane % HEAD_DIM < half, pltpu.roll(z, WIDTH - half, 1), pltpu.roll(z, half, 1))
        return z * cos + partner * sin

    if tiles_per_seq == 0:
        qa_ref, ka_ref, va_ref, qb_ref, kb_ref, vb_ref = refs
        qa_ref[...] = (head_norm(group(0), 0) * Q_SCALE).astype(BF16)
        ka_ref[...] = head_norm(group(1), 1)
        va_ref[...] = group(2)
        qb_ref[...] = (rope(head_norm(group(3), 2)) * Q_SCALE).astype(BF16)
        kb_ref[...] = rope(head_norm(group(4), 3))
        vb_ref[...] = group(5)
        return

    kgain_ref, cos_t_ref, sin_t_ref = refs[:3]
    n_cast = (len(refs) - 3 - 8) // 2
    qa_ref, kat_ref, va_ref, qb_ref, kbt_ref, vb4_ref, kat_tail_ref, vat_tail_ref = refs[3 + n_cast:11 + n_cast]
    for src_ref, dst_ref in zip(refs[3:3 + n_cast], refs[11 + n_cast:]):
        dst_ref[...] = src_ref[...].astype(BF16)
    tm = x_ref.shape[0]
    n_heads = WIDTH // HEAD_DIM

    def head_norm_t(zt, i):
        z3 = zt.reshape(n_heads, HEAD_DIM, tm)
        ms = jnp.mean(z3 * z3, axis=1, keepdims=True)
        gain = jnp.concatenate([kgain_ref[i]] * (tm // LANES), axis=1)
        return z3 * lax.rsqrt(ms + EPS) * gain[None]

    def rope_t(z3):
        half = HEAD_DIM // 2
        cos_t = cos_t_ref[...][None]
        sin_t = sin_t_ref[...][None]
        x1, x2 = z3[:, :half], z3[:, half:]
        return jnp.concatenate([x1 * cos_t - x2 * sin_t, x2 * cos_t + x1 * sin_t], axis=1)

    kbt_ref[0] = rope_t(head_norm_t(group(4).T, 1)).reshape(WIDTH, tm)
    kat = head_norm_t(group(1).T, 0).reshape(WIDTH, tm)
    kat_ref[0] = kat.astype(BF16)
    va = group(2)
    va_ref[...] = va.astype(BF16)
    qb_ref[...] = (rope(head_norm(group(3), 2)) * Q_SCALE).astype(BF16)
    qa_ref[...] = (head_norm(group(0), 0) * Q_SCALE).astype(BF16)
    vb = group(5)
    for hd in range(N_BLOCKS):
        vb4_ref[pl.ds(hd, tm, stride=N_BLOCKS), :] = vb[:, hd * LANES:(hd + 1) * LANES]

    @pl.when(pl.program_id(0) % tiles_per_seq == tiles_per_seq - 1)
    def _():
        keep = kat_tail_ref.shape[2]
        kat_tail_ref[0] = kat[:, tm - keep:]
        vat_tail_ref[0] = va[tm - keep:].T


def _proj_specs(tm, n_pos_blocks, consts):
    row = lambda w: pl.BlockSpec((tm, w), lambda i: (i, 0))
    pos = pl.BlockSpec((tm, LANES), lambda i: (i % n_pos_blocks, 0))
    g1, w_in, gains, pool = consts
    return row, [row(D_MODEL), _const_spec(g1.shape), _const_spec(w_in.shape), _const_spec(gains.shape),
                 pos, pos, _const_spec(pool.shape)]


def _proj_sample(x2d, g1, w_in, gains, cos, sin, pool):
    n = x2d.shape[0]
    row, in_specs = _proj_specs(n, 1, (g1, w_in, gains, pool))
    out_dtypes = (BF16, F32, F32, BF16, F32, F32)
    return pl.pallas_call(
        functools.partial(_proj_kernel, tiles_per_seq=0),
        grid=(1,),
        in_specs=in_specs,
        out_specs=[row(WIDTH)] * N_GROUPS,
        out_shape=[jax.ShapeDtypeStruct((n, WIDTH), dt) for dt in out_dtypes],
        compiler_params=_params(1),
        name="proj_sample",
    )(x2d, g1, w_in, gains, cos, sin, pool)


def _proj_prompt(x2d, b, s, keep, g1, w_in, gains, cos, sin, pool, kgain_t, cos_t, sin_t, to_cast):
    n = x2d.shape[0]
    tm = 2 * ROW_TILE
    tps = s // tm
    steps = n // tm
    row, in_specs = _proj_specs(tm, tps, (g1, w_in, gains, pool))
    pos_t = pl.BlockSpec((HEAD_DIM // 2, tm), lambda i: (0, i % tps))
    assert all(w.shape[0] % (BF16_ROWS * steps) == 0 for w in to_cast)
    cast_specs = [pl.BlockSpec((w.shape[0] // steps, w.shape[1]), lambda i: (i, 0)) for w in to_cast]
    in_specs = in_specs + [_const_spec(kgain_t.shape), pos_t, pos_t] + cast_specs
    feat = pl.BlockSpec((1, WIDTH, tm), lambda i: (i // tps, 0, i % tps))
    tail = pl.BlockSpec((1, WIDTH, keep), lambda i: (i // tps, 0, 0))
    out_specs = [row(WIDTH), feat, row(WIDTH), row(WIDTH), feat,
                 pl.BlockSpec((tm * N_BLOCKS, LANES), lambda i: (i, 0)), tail, tail] + cast_specs
    out_shape = [jax.ShapeDtypeStruct((n, WIDTH), BF16),
                 jax.ShapeDtypeStruct((b, WIDTH, s), BF16),
                 jax.ShapeDtypeStruct((n, WIDTH), BF16),
                 jax.ShapeDtypeStruct((n, WIDTH), BF16),
                 jax.ShapeDtypeStruct((b, WIDTH, s), F32),
                 jax.ShapeDtypeStruct((n * N_BLOCKS, LANES), F32),
                 jax.ShapeDtypeStruct((b, WIDTH, keep), F32),
                 jax.ShapeDtypeStruct((b, WIDTH, keep), F32)]
    out_shape += [jax.ShapeDtypeStruct(w.shape, BF16) for w in to_cast]
    return pl.pallas_call(
        functools.partial(_proj_kernel, tiles_per_seq=tps),
        grid=(steps,),
        in_specs=in_specs,
        out_specs=out_specs,
        out_shape=out_shape,
        compiler_params=_params(1),
        name="proj_prompt",
    )(x2d, g1, w_in, gains, cos, sin, pool, kgain_t, cos_t, sin_t, *to_cast)


def _group_bias(f_ref):
    bias = _pair_bias(f_ref, GROUP_Q)[:, :GROUP_KEYS]
    qc = (lax.broadcasted_iota(jnp.int32, bias.shape, 0) % GROUP_Q) // CHUNK
    kc = lax.broadcasted_iota(jnp.int32, bias.shape, 1) // CHUNK
    return jnp.where((kc >= qc) & (kc <= qc + BAND_PAST_CHUNKS), bias, NEG_INF)


def _band_prompt_kernel(q_ref, kt_ref, v_ref, f_ref, *refs, lam_init):
    sample_in, lam_ref, subln_ref = refs[:10], refs[10], refs[11]
    o_ref, ya_s_ref, yb_s_ref, bias_ref, kt_heads_ref = refs[12:]
    n_groups = q_ref.shape[1] // GROUP_Q
    stages = _sample_mixers(*sample_in, f_ref, lam_ref, subln_ref, ya_s_ref, yb_s_ref,
                            pl.program_id(1) * BAND_PAIRS_PER_STEP, lam_init)
    side_job = {(0, (k + 1) * n_groups // 4): stage for k, stage in enumerate(stages)}

    def window(g):
        return max(0, g * GROUP_Q - BAND_PAST), (g + 1) * GROUP_Q

    for jj in range(BAND_PAIRS_PER_STEP):
        cols = slice(jj * LANES, (jj + 1) * LANES)
        pair = pl.program_id(1) * BAND_PAIRS_PER_STEP + jj

        @pl.when(pl.program_id(0) == 0)
        def _():
            bias_ref[pair] = _group_bias(f_ref.at[jj])

        kt = kt_ref[0, cols, :]
        feat = lax.broadcasted_iota(jnp.int32, kt.shape, 0)
        kt_heads_ref[jj, 0] = jnp.where(feat < HEAD_DIM, kt, jnp.zeros_like(kt))
        kt_heads_ref[jj, 1] = jnp.where(feat >= HEAD_DIM, kt, jnp.zeros_like(kt))

        def scores(g):
            lo, hi = window(g)
            q = q_ref[0, g * GROUP_Q:(g + 1) * GROUP_Q, cols]
            s = _dot(q, jnp.concatenate([kt_heads_ref[jj, 0, :, lo:hi], kt_heads_ref[jj, 1, :, lo:hi]], axis=1))
            s = jnp.concatenate([s[:, :hi - lo], s[:, hi - lo:]], axis=0)
            return s + bias_ref[pair, :, GROUP_KEYS - (hi - lo):]

        nxt = scores(0)
        for g in range(n_groups):
            s = nxt
            if g + 1 < n_groups:
                nxt = scores(g + 1)
            lo, hi = window(g)
            (p,), l = _softmax_parts([s])
            o = _dot(p.astype(BF16), v_ref[0, lo:hi, cols]) * (1.0 / l)
            o_ref[0, g * GROUP_Q:(g + 1) * GROUP_Q, cols] = _pick_halves(o).astype(o_ref.dtype)
            if (jj, g) in side_job:
                side_job[jj, g]()


def _band_prompt(qa, kat, va, f, new, caches, lam_vecs, subln, lam_init, t):
    b, s, _ = qa.shape
    pairs = BAND_PAIRS_PER_STEP
    assert new[0].shape[0] == b * t
    blk = pl.BlockSpec((1, s, pairs * LANES), lambda i, j: (i, 0, j))
    blk_t = pl.BlockSpec((1, pairs * LANES, s), lambda i, j: (i, j, 0))
    row = pl.BlockSpec((t, pairs * LANES), lambda i, j: (i, j))
    feat = lambda c: pl.BlockSpec((1, pairs * LANES, c.shape[2]), lambda i, j: (i, j, 0))
    whole = lambda c: pl.BlockSpec((1,) + c.shape[1:], lambda i, j: (i, 0, 0))
    cak, cav, cbk, cbv4 = caches
    return pl.pallas_call(
        functools.partial(_band_prompt_kernel, lam_init=lam_init),
        grid=(b, N_BLOCKS // pairs),
        in_specs=[blk, blk_t, blk, pl.BlockSpec((pairs, 2, BIAS_SPAN), lambda i, j: (j, 0, 0))]
        + [row] * 6 + [feat(cak), feat(cav), feat(cbk), whole(cbv4)]
        + [_const_spec(lam_vecs.shape), _const_spec(subln.shape)],
        out_specs=[blk, row, row],
        out_shape=[jax.ShapeDtypeStruct((b, s, WIDTH), BF16)] + [jax.ShapeDtypeStruct((b * t, WIDTH), BF16)] * 2,
        scratch_shapes=[pltpu.VMEM((N_BLOCKS, 2 * GROUP_Q, GROUP_KEYS), F32), pltpu.VMEM((pairs, 2, LANES, s), BF16)],
        compiler_params=_params(2),
        name="band_prompt",
    )(qa, kat, va, f, *new, *caches, lam_vecs, subln)


def _lambda(lam_ref, lam_init):
    e1 = jnp.exp(jnp.sum(lam_ref[0:1, :] * lam_ref[1:2, :], axis=-1, keepdims=True))
    e2 = jnp.exp(jnp.sum(lam_ref[2:3, :] * lam_ref[3:4, :], axis=-1, keepdims=True))
    return e1 - e2 + lam_init


def _diff_out(o, subln, lam_init):
    return _rms(o, subln) * (1.0 - lam_init)


def _diff_prompt_kernel(q_ref, kt_ref, v4_ref, lam_ref, subln_ref, o_ref, kbf_ref, vbf_ref, mask_ref, *, lam_init):
    s_len = q_ref.shape[1]
    tq = Q_BLOCK_B
    n_blocks = s_len // tq
    lam = _lambda(lam_ref, lam_init)

    @pl.when((pl.program_id(0) == 0) & (pl.program_id(1) == 0))
    def _():
        row = lax.broadcasted_iota(jnp.int32, (2 * tq, tq), 0) % tq
        col = lax.broadcasted_iota(jnp.int32, (2 * tq, tq), 1)
        mask_ref[...] = jnp.where(col // CHUNK <= row // CHUNK, 0.0, NEG_INF).astype(F32)

    for jj in range(DIFF_HEADS_PER_STEP):
        cols = slice(jj * LANES, (jj + 1) * LANES)
        head = pl.program_id(1) * DIFF_HEADS_PER_STEP + jj
        kbf_ref[jj] = kt_ref[0, cols, :].astype(BF16)
        vbf_ref[jj] = v4_ref[0, pl.ds(head, s_len, stride=N_BLOCKS), :].astype(BF16)

        def scores(i):
            r0 = i * tq
            qs = _lane_halves(q_ref[0, r0:r0 + tq, cols])
            parts = [_dot(qs, kbf_ref[jj, :, r0:r0 + tq]) + mask_ref[...]]
            if i > 0:
                parts.append(_dot(qs, kbf_ref[jj, :, 0:r0]))
            return parts

        nxt = scores(0)
        for i in range(n_blocks):
            r0 = i * tq
            parts = nxt
            if i + 1 < n_blocks:
                nxt = scores(i + 1)
            a, inv = _diff_combine(*_softmax_parts(parts), lam, tq)
            o = _dot(a[0], vbf_ref[jj, r0:r0 + tq, :])
            if i > 0:
                o = o + _dot(a[1], vbf_ref[jj, 0:r0, :])
            o_ref[0, r0:r0 + tq, cols] = _diff_out(o * inv, subln_ref[...], lam_init).astype(o_ref.dtype)


def _diff_prompt(qb, kbt, vb4, lam_vecs, subln, lam_init):
    b, s, _ = qb.shape
    heads = DIFF_HEADS_PER_STEP
    blk = pl.BlockSpec((1, s, heads * LANES), lambda i, j: (i, 0, j))
    return pl.pallas_call(
        functools.partial(_diff_prompt_kernel, lam_init=lam_init),
        grid=(b, N_BLOCKS // heads),
        in_specs=[blk, pl.BlockSpec((1, heads * LANES, s), lambda i, j: (i, j, 0)),
                  pl.BlockSpec((1, s * N_BLOCKS, LANES), lambda i, j: (i, 0, 0)),
                  _const_spec(lam_vecs.shape), _const_spec(subln.shape)],
        out_specs=blk,
        out_shape=jax.ShapeDtypeStruct((b, s, WIDTH), BF16),
        scratch_shapes=[pltpu.VMEM((heads, LANES, s), BF16), pltpu.VMEM((heads, s, LANES), BF16),
                        pltpu.VMEM((2 * Q_BLOCK_B, Q_BLOCK_B), F32)],
        compiler_params=_params(2),
        name="diff_prompt",
    )(qb, kbt, vb4, lam_vecs, subln)


def _sample_mixers(qa_ref, ka_ref, va_ref, qb_ref, kb_ref, vb_ref, cakt_ref, cavt_ref, cbkt_ref, cbv4_ref,
                   f_ref, lam_ref, subln_ref, ya_ref, yb_ref, first_head, lam_init):
    t = qa_ref.shape[0]
    lc = cakt_ref.shape[2]
    past = cbkt_ref.shape[2]
    blocks = [slice(j * LANES, (j + 1) * LANES) for j in range(qa_ref.shape[1] // LANES)]
    st = {}

    def scores():
        sa = []
        for j, cols in enumerate(blocks):
            bias = _pair_bias(f_ref.at[j], t)
            qm = _lane_halves(qa_ref[:, cols])
            sa.append([_dot(qm, cakt_ref[0, cols, :].astype(BF16)) + bias[:, :lc],
                       _dot_nt(qm, ka_ref[:, cols].astype(BF16)) + bias[:, lc:lc + t]])
        sb = []
        for cols in blocks:
            qs = _lane_halves(qb_ref[:, cols])
            sb.append([_dot(qs, cbkt_ref[0, cols, :].astype(BF16)), _dot_nt(qs, kb_ref[:, cols].astype(BF16))])
        st["s"] = (sa, sb)

    def softmax():
        lam = _lambda(lam_ref, lam_init)
        sa, sb = st["s"]
        st["p"] = ([_softmax_parts(s) for s in sa], [_diff_combine(*_softmax_parts(s), lam, t) for s in sb])

    def output():
        pa, pb = st["p"]
        for cols, ((pc, pn), l) in zip(blocks, pa):
            o = _dot_nt(pc.astype(BF16), cavt_ref[0, cols, :].astype(BF16)) + _dot(pn.astype(BF16),
                                                                                 va_ref[:, cols].astype(BF16))
            ya_ref[:, cols] = _pick_halves(o * (1.0 / l)).astype(ya_ref.dtype)
        for j, (cols, ((ac, an), inv)) in enumerate(zip(blocks, pb)):
            vc = cbv4_ref[0, pl.ds(first_head + j, past, stride=N_BLOCKS), :].astype(BF16)
            o = (_dot(ac, vc) + _dot(an, vb_ref[:, cols].astype(BF16))) * inv
            yb_ref[:, cols] = _diff_out(o, subln_ref[...], lam_init).astype(yb_ref.dtype)

    return scores, softmax, output


def _merge_mlp_kernel(x_ref, ya_ref, yb_ref, g1_ref, wg_ref, bg_ref, wpa_ref, wpb_ref, wo_ref,
                      g2_ref, w1_ref, w2_ref, o_ref):
    x = x_ref[...]
    h = _rms(x, g1_ref[...]).astype(BF16)
    m = jax.nn.sigmoid(_dot(h, wg_ref[:, :D_MODEL]) + bg_ref[:, :D_MODEL]) * _dot(ya_ref[...], wpa_ref[...])
    m = m + jax.nn.sigmoid(_dot(h, wg_ref[:, D_MODEL:]) + bg_ref[:, D_MODEL:]) * _dot(yb_ref[...], wpb_ref[...])
    x1 = x + _dot(m.astype(BF16), wo_ref[...])
    hn = _rms(x1, g2_ref[...]).astype(BF16)
    acc = x1
    for c in range(D_FF // D_MODEL):
        cols = slice(c * D_MODEL, (c + 1) * D_MODEL)
        u = jnp.maximum(_dot(hn, w1_ref[:, cols]), 0.0)
        acc = acc + _dot((u * u).astype(BF16), w2_ref[cols, :])
    o_ref[...] = acc


def _merge_mlp(x2d, ya, yb, tm, g1, wg, bg, wpa, wpb, wo, g2, w1, w2):
    n = x2d.shape[0]
    row = lambda w: pl.BlockSpec((tm, w), lambda i: (i, 0))
    consts = (g1, wg, bg, wpa, wpb, wo, g2, w1, w2)
    return pl.pallas_call(
        _merge_mlp_kernel,
        grid=(n // tm,),
        in_specs=[row(D_MODEL), row(WIDTH), row(WIDTH)] + [_const_spec(c.shape) for c in consts],
        out_specs=row(D_MODEL),
        out_shape=jax.ShapeDtypeStruct((n, D_MODEL), F32),
        compiler_params=_params(1),
        name="merge_mlp",
    )(x2d, ya, yb, *consts)


def _rope_tables(pos):
    half = HEAD_DIM // 2
    inv_freq = ROPE_THETA ** (-jnp.arange(half, dtype=F32) / half)
    ang = pos.astype(F32)[:, None] * inv_freq[None, :]
    cos = jnp.cos(ang)
    sin = jnp.sin(ang)
    reps = LANES // HEAD_DIM
    return (jnp.tile(jnp.concatenate([cos, cos], axis=1), (1, reps)),
            jnp.tile(jnp.concatenate([-sin, sin], axis=1), (1, reps)), cos.T, sin.T)


def _band_offset_table(table):
    n_far = BAND_PAST + GROUP_Q - 1 - REL_PAST + 1
    n_fut = BIAS_SPAN - n_far - (N_REL - 1)
    h = table.shape[0]
    f = jnp.concatenate([jnp.broadcast_to(table[:, N_REL - 1:], (h, n_far)), table[:, N_REL - 2::-1],
                         jnp.broadcast_to(table[:, :1], (h, n_fut))], axis=1).astype(F32)
    return f.reshape(h // 2, 2, BIAS_SPAN)


def _pool_matrix():
    i = np.arange(MXU_DIM)
    return jnp.asarray((i[:, None] // HEAD_DIM == i[None, :] // HEAD_DIM) / HEAD_DIM, dtype=BF16)


def kernel(x_prompt, x_sample, cache_a_k, cache_a_v, cache_b_k, cache_b_v, ln1_g, w_in, qn_a, kn_a, rel_bias, qn_b, kn_b, lam_q1, lam_k1, lam_q2, lam_k2, subln_g, w_gate, b_gate, w_proj_a, w_proj_b, w_out, ln2_g, w_ff1, w_ff2):
    depth = w_in.shape[0]
    b, s, d = x_prompt.shape
    db, t, _ = x_sample.shape
    past_len = cache_b_k.shape[2]
    lc = cache_a_k.shape[2]
    keep = min(BAND_PAST, s)
    assert d == D_MODEL and s % (2 * ROW_TILE) == 0 and keep <= 2 * ROW_TILE and keep % LANES == 0
    assert lc == BAND_PAST and t <= GROUP_Q

    cos_p, sin_p, cos_pt, sin_pt = _rope_tables(jnp.arange(s))
    cos_s, sin_s = (jnp.tile(a, (db, 1)) for a in _rope_tables(past_len + jnp.arange(t))[:2])
    lane_const = lambda g: jnp.broadcast_to(g[:, None], (HEAD_DIM, LANES))
    pool = _pool_matrix()
    tile_heads = lambda g: jnp.tile(g, WIDTH // HEAD_DIM)

    xp = x_prompt.reshape(b * s, d)
    xs = x_sample.reshape(db * t, d)
    outs = [[] for _ in range(8)]
    for l in range(depth):
        lam_init = 0.8 - 0.6 * math.exp(-0.3 * l)
        g1 = ln1_g[l][None]
        g2 = ln2_g[l][None]
        gains = jnp.stack([tile_heads(qn_a[l]), tile_heads(kn_a[l]), tile_heads(qn_b[l]), tile_heads(kn_b[l])])
        lam_vecs = jnp.stack([lam_q1[l], lam_k1[l], lam_q2[l], lam_k2[l]])
        subln = subln_g[l][None]
        bias_f = _band_offset_table(rel_bias[l])
        w_in_l = w_in[l].astype(BF16)

        kgain_t = jnp.stack([lane_const(kn_a[l]), lane_const(kn_b[l])])
        later_w = (w_gate[l], w_proj_a[l], w_proj_b[l], w_out[l], w_ff1[l], w_ff2[l])
        proj_out = _proj_prompt(xp, b, s, keep, g1, w_in_l, gains, cos_p, sin_p, pool, kgain_t, cos_pt, sin_pt, later_w)
        qa, kat, va, qb, kbt, vb4, kat_tail, vat_tail = proj_out[:8]
        wg, wpa, wpb, wo, w1, w2 = proj_out[8:]
        merge_w = (g1, wg, b_gate[l][None], wpa, wpb, wo, g2, w1, w2)
        as3 = lambda a: a.reshape(b, s, WIDTH)
        new = _proj_sample(xs, g1, w_in_l, gains, cos_s, sin_s, pool)
        caches = (cache_a_k[l].transpose(0, 2, 3, 1).reshape(db, WIDTH, lc),
                  cache_a_v[l].transpose(0, 2, 3, 1).reshape(db, WIDTH, lc),
                  cache_b_k[l].transpose(0, 2, 3, 4, 1).reshape(db, WIDTH, past_len),
                  cache_b_v[l].reshape(db, past_len * N_BLOCKS, LANES))
        ya, ya_s, yb_s = _band_prompt(as3(qa), kat, as3(va), bias_f, new, caches, lam_vecs, subln, lam_init, t)
        yb = _diff_prompt(as3(qb), kbt, vb4.reshape(b, s * N_BLOCKS, LANES), lam_vecs, subln, lam_init)
        xp = _merge_mlp(xp, ya.reshape(b * s, WIDTH), yb.reshape(b * s, WIDTH), 2 * ROW_TILE, *merge_w)
        token_major_a = lambda a: a.reshape(b, N_HEADS_A, HEAD_DIM, keep).transpose(0, 3, 1, 2)
        outs[0].append(token_major_a(kat_tail))
        outs[1].append(token_major_a(vat_tail))
        outs[2].append(kbt.reshape(b, N_HEADS_B, 2, HEAD_DIM, s).transpose(0, 4, 1, 2, 3))
        outs[3].append(vb4.reshape(b, s, N_HEADS_B, 2 * HEAD_DIM))

        xs = _merge_mlp(xs, ya_s, yb_s, db * t, *merge_w)
        outs[4].append(new[1].reshape(db, t, N_HEADS_A, HEAD_DIM))
        outs[5].append(new[2].reshape(db, t, N_HEADS_A, HEAD_DIM))
        outs[6].append(new[4].reshape(db, t, N_HEADS_B, 2, HEAD_DIM))
        outs[7].append(new[5].reshape(db, t, N_HEADS_B, 2 * HEAD_DIM))

    return (xp.reshape(b, s, d), xs.reshape(db, t, d)) + tuple(jnp.stack(o) for o in outs)
```

```python
import functools
import math

import jax
import jax.numpy as jnp
import numpy as np
from jax import lax
from jax.experimental import pallas as pl
from jax.experimental.pallas import tpu as pltpu

D_MODEL = 1024
CHUNK = 64
HEAD_DIM = 64
N_HEADS_A = 8
N_HEADS_B = 4
BAND_PAST_CHUNKS = 8
BAND_PAST = BAND_PAST_CHUNKS * CHUNK
REL_FUT = CHUNK - 1
REL_PAST = 256
N_REL = REL_FUT + REL_PAST + 1
WIDTH = 512
N_GROUPS = 6
D_FF = 4 * D_MODEL
ROPE_THETA = 10000.0
EPS = 1e-6
NEG_INF = -1e30
LOG2E = math.log2(math.e)
Q_SCALE = HEAD_DIM ** -0.5 * LOG2E

LANES = 128
BF16_ROWS = 16
N_BLOCKS = WIDTH // LANES
MXU_DIM = 256
GROUP_CHUNKS = 2
GROUP_Q = GROUP_CHUNKS * CHUNK
GROUP_KEYS = BAND_PAST + GROUP_Q
BIAS_SPAN = GROUP_KEYS + GROUP_Q
BAND_PAIRS_PER_STEP = 2
Q_BLOCK_B = 256
DIFF_HEADS_PER_STEP = 1
ROW_TILE = 512
VMEM_LIMIT = 60 * 1024 * 1024
FF_CHUNK = 512

BF16 = jnp.bfloat16
F32 = jnp.float32


def _dot(a, b):
    return jnp.dot(a, b, preferred_element_type=F32)


def _dot_nt(a, b):
    return lax.dot_general(a, b, (((1,), (1,)), ((), ())), preferred_element_type=F32)


def _rms(x, g):
    return x * lax.rsqrt(jnp.mean(x * x, axis=-1, keepdims=True) + EPS) * g


def _const_spec(shape):
    nd = len(shape)
    return pl.BlockSpec(shape, lambda *_: (0,) * nd, pipeline_mode=pl.Buffered(1))


def _params(n_grid_axes):
    return pltpu.CompilerParams(dimension_semantics=("arbitrary",) * n_grid_axes, vmem_limit_bytes=VMEM_LIMIT)


def _lane_halves(q):
    lane = lax.broadcasted_iota(jnp.int32, q.shape, 1)
    zero = jnp.zeros_like(q)
    return jnp.concatenate([jnp.where(lane < HEAD_DIM, q, zero), jnp.where(lane >= HEAD_DIM, q, zero)], axis=0)


def _pick_halves(o):
    m = o.shape[0] // 2
    lane = lax.broadcasted_iota(jnp.int32, (m, LANES), 1)
    return jnp.where(lane < HEAD_DIM, o[:m], o[m:])


def _softmax_parts(parts):
    m = functools.reduce(jnp.maximum, [jnp.max(s, axis=-1, keepdims=True) for s in parts])
    ps = [jnp.exp2(s - m) for s in parts]
    l = functools.reduce(lambda a, b: a + b, [jnp.sum(p, axis=-1, keepdims=True) for p in ps])
    return ps, l


def _diff_combine(ps, l, lam, t):
    c = lam * l[:t] / l[t:]
    return [(p[:t] - p[t:] * c).astype(BF16) for p in ps], 1.0 / l[:t]


def _pair_bias(f_ref, rows):
    def one(hd):
        x = jnp.broadcast_to(f_ref[hd:hd + 1, :], (rows, BIAS_SPAN))
        return pltpu.roll(x, BIAS_SPAN - GROUP_Q + 1, 1, stride=1, stride_axis=0)

    return jnp.concatenate([one(0), one(1)], axis=0) * LOG2E


def _proj_kernel(x_ref, g1_ref, w_ref, gains_ref, cos_ref, sin_ref, pool_ref, *refs, tiles_per_seq):
    h = _rms(x_ref[...], g1_ref[...]).astype(BF16)
    cos = jnp.concatenate([cos_ref[...]] * N_BLOCKS, axis=1)
    sin = jnp.concatenate([sin_ref[...]] * N_BLOCKS, axis=1)
    pool = pool_ref[...]

    def group(i):
        return _dot(h, w_ref[:, i * WIDTH:(i + 1) * WIDTH])

    def head_norm(z, i):
        zz = (z * z).astype(BF16)
        ms = jnp.concatenate([_dot(zz[:, k * MXU_DIM:(k + 1) * MXU_DIM], pool) for k in range(WIDTH // MXU_DIM)],
                             axis=1)
        return z * lax.rsqrt(ms + EPS) * gains_ref[i:i + 1, :]

    def rope(z):
        lane = lax.broadcasted_iota(jnp.int32, z.shape, 1)
        half = HEAD_DIM // 2
        partner = jnp.where(lane % HEAD_DIM < half, pltpu.roll(z, WIDTH - half, 1), pltpu.roll(z, half, 1))
        return z * cos + partner * sin

    if tiles_per_seq == 0:
        qa_ref, ka_ref, va_ref, qb_ref, kb_ref, vb_ref = refs
        qa_ref[...] = (head_norm(group(0), 0) * Q_SCALE).astype(BF16)
        ka_ref[...] = head_norm(group(1), 1)
        va_ref[...] = group(2)
        qb_ref[...] = (rope(head_norm(group(3), 2)) * Q_SCALE).astype(BF16)
        kb_ref[...] = rope(head_norm(group(4), 3))
        vb_ref[...] = group(5)
        return

    kgain_ref, cos_t_ref, sin_t_ref = refs[:3]
    n_cast = (len(refs) - 3 - 8) // 2
    qa_ref, kat_ref, va_ref, qb_ref, kbt_ref, vb4_ref, kat_tail_ref, vat_tail_ref = refs[3 + n_cast:11 + n_cast]
    for src_ref, dst_ref in zip(refs[3:3 + n_cast], refs[11 + n_cast:]):
        dst_ref[...] = src_ref[...].astype(BF16)
    tm = x_ref.shape[0]
    n_heads = WIDTH // HEAD_DIM

    def head_norm_t(zt, i):
        z3 = zt.reshape(n_heads, HEAD_DIM, tm)
        ms = jnp.mean(z3 * z3, axis=1, keepdims=True)
        gain = jnp.concatenate([kgain_ref[i]] * (tm // LANES), axis=1)
        return z3 * lax.rsqrt(ms + EPS) * gain[None]

    def rope_t(z3):
        half = HEAD_DIM // 2
        cos_t = cos_t_ref[...][None]
        sin_t = sin_t_ref[...][None]
        x1, x2 = z3[:, :half], z3[:, half:]
        return jnp.concatenate([x1 * cos_t - x2 * sin_t, x2 * cos_t + x1 * sin_t], axis=1)

    kbt_ref[0] = rope_t(head_norm_t(group(4).T, 1)).reshape(WIDTH, tm)
    kat = head_norm_t(group(1).T, 0).reshape(WIDTH, tm)
    kat_ref[0] = kat.astype(BF16)
    va = group(2)
    va_ref[...] = va.astype(BF16)
    qb_ref[...] = (rope(head_norm(group(3), 2)) * Q_SCALE).astype(BF16)
    qa_ref[...] = (head_norm(group(0), 0) * Q_SCALE).astype(BF16)
    vb = group(5)
    for hd in range(N_BLOCKS):
        vb4_ref[pl.ds(hd, tm, stride=N_BLOCKS), :] = vb[:, hd * LANES:(hd + 1) * LANES]

    @pl.when(pl.program_id(0) % tiles_per_seq == tiles_per_seq - 1)
    def _():
        keep = kat_tail_ref.shape[2]
        kat_tail_ref[0] = kat[:, tm - keep:]
        vat_tail_ref[0] = va[tm - keep:].T


def _proj_specs(tm, n_pos_blocks, consts):
    row = lambda w: pl.BlockSpec((tm, w), lambda i: (i, 0))
    pos = pl.BlockSpec((tm, LANES), lambda i: (i % n_pos_blocks, 0))
    g1, w_in, gains, pool = consts
    return row, [row(D_MODEL), _const_spec(g1.shape), _const_spec(w_in.shape), _const_spec(gains.shape),
                 pos, pos, _const_spec(pool.shape)]


def _proj_sample(x2d, g1, w_in, gains, cos, sin, pool):
    n = x2d.shape[0]
    row, in_specs = _proj_specs(n, 1, (g1, w_in, gains, pool))
    out_dtypes = (BF16, F32, F32, BF16, F32, F32)
    return pl.pallas_call(
        functools.partial(_proj_kernel, tiles_per_seq=0),
        grid=(1,),
        in_specs=in_specs,
        out_specs=[row(WIDTH)] * N_GROUPS,
        out_shape=[jax.ShapeDtypeStruct((n, WIDTH), dt) for dt in out_dtypes],
        compiler_params=_params(1),
        name="proj_sample",
    )(x2d, g1, w_in, gains, cos, sin, pool)


def _proj_prompt(x2d, b, s, keep, g1, w_in, gains, cos, sin, pool, kgain_t, cos_t, sin_t, to_cast):
    n = x2d.shape[0]
    tm = 2 * ROW_TILE
    tps = s // tm
    steps = n // tm
    row, in_specs = _proj_specs(tm, tps, (g1, w_in, gains, pool))
    pos_t = pl.BlockSpec((HEAD_DIM // 2, tm), lambda i: (0, i % tps))
    assert all(w.shape[0] % (BF16_ROWS * steps) == 0 for w in to_cast)
    cast_specs = [pl.BlockSpec((w.shape[0] // steps, w.shape[1]), lambda i: (i, 0)) for w in to_cast]
    in_specs = in_specs + [_const_spec(kgain_t.shape), pos_t, pos_t] + cast_specs
    feat = pl.BlockSpec((1, WIDTH, tm), lambda i: (i // tps, 0, i % tps))
    tail = pl.BlockSpec((1, WIDTH, keep), lambda i: (i // tps, 0, 0))
    out_specs = [row(WIDTH), feat, row(WIDTH), row(WIDTH), feat,
                 pl.BlockSpec((tm * N_BLOCKS, LANES), lambda i: (i, 0)), tail, tail] + cast_specs
    out_shape = [jax.ShapeDtypeStruct((n, WIDTH), BF16),
                 jax.ShapeDtypeStruct((b, WIDTH, s), BF16),
                 jax.ShapeDtypeStruct((n, WIDTH), BF16),
                 jax.ShapeDtypeStruct((n, WIDTH), BF16),
                 jax.ShapeDtypeStruct((b, WIDTH, s), F32),
                 jax.ShapeDtypeStruct((n * N_BLOCKS, LANES), F32),
                 jax.ShapeDtypeStruct((b, WIDTH, keep), F32),
                 jax.ShapeDtypeStruct((b, WIDTH, keep), F32)]
    out_shape += [jax.ShapeDtypeStruct(w.shape, BF16) for w in to_cast]
    return pl.pallas_call(
        functools.partial(_proj_kernel, tiles_per_seq=tps),
        grid=(steps,),
        in_specs=in_specs,
        out_specs=out_specs,
        out_shape=out_shape,
        compiler_params=_params(1),
        name="proj_prompt",
    )(x2d, g1, w_in, gains, cos, sin, pool, kgain_t, cos_t, sin_t, *to_cast)


def _group_bias(f_ref):
    bias = _pair_bias(f_ref, GROUP_Q)[:, :GROUP_KEYS]
    qc = (lax.broadcasted_iota(jnp.int32, bias.shape, 0) % GROUP_Q) // CHUNK
    kc = lax.broadcasted_iota(jnp.int32, bias.shape, 1) // CHUNK
    return jnp.where((kc >= qc) & (kc <= qc + BAND_PAST_CHUNKS), bias, NEG_INF)


def _band_prompt_kernel(q_ref, kt_ref, v_ref, f_ref, *refs, lam_init):
    sample_in, lam_ref, subln_ref = refs[:10], refs[10], refs[11]
    o_ref, ya_s_ref, yb_s_ref, bias_ref, kt_heads_ref = refs[12:]
    n_groups = q_ref.shape[1] // GROUP_Q
    stages = _sample_mixers(*sample_in, f_ref, lam_ref, subln_ref, ya_s_ref, yb_s_ref,
                            pl.program_id(1) * BAND_PAIRS_PER_STEP, lam_init)
    side_job = {(0, (k + 1) * n_groups // 4): stage for k, stage in enumerate(stages)}

    def window(g):
        return max(0, g * GROUP_Q - BAND_PAST), (g + 1) * GROUP_Q

    for jj in range(BAND_PAIRS_PER_STEP):
        cols = slice(jj * LANES, (jj + 1) * LANES)
        pair = pl.program_id(1) * BAND_PAIRS_PER_STEP + jj

        @pl.when(pl.program_id(0) == 0)
        def _():
            bias_ref[pair] = _group_bias(f_ref.at[jj])

        kt = kt_ref[0, cols, :]
        feat = lax.broadcasted_iota(jnp.int32, kt.shape, 0)
        kt_heads_ref[jj, 0] = jnp.where(feat < HEAD_DIM, kt, jnp.zeros_like(kt))
        kt_heads_ref[jj, 1] = jnp.where(feat >= HEAD_DIM, kt, jnp.zeros_like(kt))

        def scores(g):
            lo, hi = window(g)
            q = q_ref[0, g * GROUP_Q:(g + 1) * GROUP_Q, cols]
            s = _dot(q, jnp.concatenate([kt_heads_ref[jj, 0, :, lo:hi], kt_heads_ref[jj, 1, :, lo:hi]], axis=1))
            s = jnp.concatenate([s[:, :hi - lo], s[:, hi - lo:]], axis=0)
            return s + bias_ref[pair, :, GROUP_KEYS - (hi - lo):]

        nxt = scores(0)
        for g in range(n_groups):
            s = nxt
            if g + 1 < n_groups:
                nxt = scores(g + 1)
            lo, hi = window(g)
            (p,), l = _softmax_parts([s])
            o = _dot(p.astype(BF16), v_ref[0, lo:hi, cols]) * (1.0 / l)
            o_ref[0, g * GROUP_Q:(g + 1) * GROUP_Q, cols] = _pick_halves(o).astype(o_ref.dtype)
            if (jj, g) in side_job:
                side_job[jj, g]()


def _band_prompt(qa, kat, va, f, new, caches, lam_vecs, subln, lam_init, t):
    b, s, _ = qa.shape
    pairs = BAND_PAIRS_PER_STEP
    assert new[0].shape[0] == b * t
    blk = pl.BlockSpec((1, s, pairs * LANES), lambda i, j: (i, 0, j))
    blk_t = pl.BlockSpec((1, pairs * LANES, s), lambda i, j: (i, j, 0))
    row = pl.BlockSpec((t, pairs * LANES), lambda i, j: (i, j))
    feat = lambda c: pl.BlockSpec((1, pairs * LANES, c.shape[2]), lambda i, j: (i, j, 0))
    whole = lambda c: pl.BlockSpec((1,) + c.shape[1:], lambda i, j: (i, 0, 0))
    cak, cav, cbk, cbv4 = caches
    return pl.pallas_call(
        functools.partial(_band_prompt_kernel, lam_init=lam_init),
        grid=(b, N_BLOCKS // pairs),
        in_specs=[blk, blk_t, blk, pl.BlockSpec((pairs, 2, BIAS_SPAN), lambda i, j: (j, 0, 0))]
        + [row] * 6 + [feat(cak), feat(cav), feat(cbk), whole(cbv4)]
        + [_const_spec(lam_vecs.shape), _const_spec(subln.shape)],
        out_specs=[blk, row, row],
        out_shape=[jax.ShapeDtypeStruct((b, s, WIDTH), BF16)] + [jax.ShapeDtypeStruct((b * t, WIDTH), BF16)] * 2,
        scratch_shapes=[pltpu.VMEM((N_BLOCKS, 2 * GROUP_Q, GROUP_KEYS), F32), pltpu.VMEM((pairs, 2, LANES, s), BF16)],
        compiler_params=_params(2),
        name="band_prompt",
    )(qa, kat, va, f, *new, *caches, lam_vecs, subln)


def _lambda(lam_ref, lam_init):
    e1 = jnp.exp(jnp.sum(lam_ref[0:1, :] * lam_ref[1:2, :], axis=-1, keepdims=True))
    e2 = jnp.exp(jnp.sum(lam_ref[2:3, :] * lam_ref[3:4, :], axis=-1, keepdims=True))
    return e1 - e2 + lam_init


def _diff_out(o, subln, lam_init):
    return _rms(o, subln) * (1.0 - lam_init)


def _diff_prompt_kernel(q_ref, kt_ref, v4_ref, lam_ref, subln_ref, o_ref, kbf_ref, vbf_ref, *, lam_init):
    s_len = q_ref.shape[1]
    tq = Q_BLOCK_B
    n_blocks = s_len // tq
    lam = _lambda(lam_ref, lam_init)
    row = lax.broadcasted_iota(jnp.int32, (2 * tq, tq), 0) % tq
    col = lax.broadcasted_iota(jnp.int32, (2 * tq, tq), 1)
    diag_mask = jnp.where(col // CHUNK <= row // CHUNK, 0.0, NEG_INF).astype(F32)

    for jj in range(DIFF_HEADS_PER_STEP):
        cols = slice(jj * LANES, (jj + 1) * LANES)
        head = pl.program_id(1) * DIFF_HEADS_PER_STEP + jj
        kbf_ref[jj] = kt_ref[0, cols, :].astype(BF16)
        vbf_ref[jj] = v4_ref[0, pl.ds(head, s_len, stride=N_BLOCKS), :].astype(BF16)

        def scores(i):
            r0 = i * tq
            qs = _lane_halves(q_ref[0, r0:r0 + tq, cols])
            parts = [_dot(qs, kbf_ref[jj, :, r0:r0 + tq]) + diag_mask]
            if i > 0:
                parts.append(_dot(qs, kbf_ref[jj, :, 0:r0]))
            return parts

        nxt = scores(0)
        for i in range(n_blocks):
            r0 = i * tq
            parts = nxt
            if i + 1 < n_blocks:
                nxt = scores(i + 1)
            a, inv = _diff_combine(*_softmax_parts(parts), lam, tq)
            o = _dot(a[0], vbf_ref[jj, r0:r0 + tq, :])
            if i > 0:
                o = o + _dot(a[1], vbf_ref[jj, 0:r0, :])
            o_ref[0, r0:r0 + tq, cols] = _diff_out(o * inv, subln_ref[...], lam_init).astype(o_ref.dtype)


def _diff_prompt(qb, kbt, vb4, lam_vecs, subln, lam_init):
    b, s, _ = qb.shape
    heads = DIFF_HEADS_PER_STEP
    blk = pl.BlockSpec((1, s, heads * LANES), lambda i, j: (i, 0, j))
    return pl.pallas_call(
        functools.partial(_diff_prompt_kernel, lam_init=lam_init),
        grid=(b, N_BLOCKS // heads),
        in_specs=[blk, pl.BlockSpec((1, heads * LANES, s), lambda i, j: (i, j, 0)),
                  pl.BlockSpec((1, s * N_BLOCKS, LANES), lambda i, j: (i, 0, 0)),
                  _const_spec(lam_vecs.shape), _const_spec(subln.shape)],
        out_specs=blk,
        out_shape=jax.ShapeDtypeStruct((b, s, WIDTH), BF16),
        scratch_shapes=[pltpu.VMEM((heads, LANES, s), BF16), pltpu.VMEM((heads, s, LANES), BF16)],
        compiler_params=_params(2),
        name="diff_prompt",
    )(qb, kbt, vb4, lam_vecs, subln)


def _sample_mixers(qa_ref, ka_ref, va_ref, qb_ref, kb_ref, vb_ref, cakt_ref, cavt_ref, cbkt_ref, cbv4_ref,
                   f_ref, lam_ref, subln_ref, ya_ref, yb_ref, first_head, lam_init):
    t = qa_ref.shape[0]
    lc = cakt_ref.shape[2]
    past = cbkt_ref.shape[2]
    blocks = [slice(j * LANES, (j + 1) * LANES) for j in range(qa_ref.shape[1] // LANES)]
    st = {}

    def scores():
        sa = []
        for j, cols in enumerate(blocks):
            bias = _pair_bias(f_ref.at[j], t)
            qm = _lane_halves(qa_ref[:, cols])
            sa.append([_dot(qm, cakt_ref[0, cols, :].astype(BF16)) + bias[:, :lc],
                       _dot_nt(qm, ka_ref[:, cols].astype(BF16)) + bias[:, lc:lc + t]])
        sb = []
        for cols in blocks:
            qs = _lane_halves(qb_ref[:, cols])
            sb.append([_dot(qs, cbkt_ref[0, cols, :].astype(BF16)), _dot_nt(qs, kb_ref[:, cols].astype(BF16))])
        st["s"] = (sa, sb)

    def softmax():
        lam = _lambda(lam_ref, lam_init)
        sa, sb = st["s"]
        st["p"] = ([_softmax_parts(s) for s in sa], [_diff_combine(*_softmax_parts(s), lam, t) for s in sb])

    def output():
        pa, pb = st["p"]
        for cols, ((pc, pn), l) in zip(blocks, pa):
            o = _dot_nt(pc.astype(BF16), cavt_ref[0, cols, :].astype(BF16)) + _dot(pn.astype(BF16),
                                                                                 va_ref[:, cols].astype(BF16))
            ya_ref[:, cols] = _pick_halves(o * (1.0 / l)).astype(ya_ref.dtype)
        for j, (cols, ((ac, an), inv)) in enumerate(zip(blocks, pb)):
            vc = cbv4_ref[0, pl.ds(first_head + j, past, stride=N_BLOCKS), :].astype(BF16)
            o = (_dot(ac, vc) + _dot(an, vb_ref[:, cols].astype(BF16))) * inv
            yb_ref[:, cols] = _diff_out(o, subln_ref[...], lam_init).astype(yb_ref.dtype)

    return scores, softmax, output


def _merge_mlp_kernel(x_ref, ya_ref, yb_ref, g1_ref, wg_ref, bg_ref, wpa_ref, wpb_ref, wo_ref,
                      g2_ref, w1_ref, w2_ref, o_ref):
    x = x_ref[...]
    h = _rms(x, g1_ref[...]).astype(BF16)
    m = jax.nn.sigmoid(_dot(h, wg_ref[:, :D_MODEL]) + bg_ref[:, :D_MODEL]) * _dot(ya_ref[...], wpa_ref[...])
    m = m + jax.nn.sigmoid(_dot(h, wg_ref[:, D_MODEL:]) + bg_ref[:, D_MODEL:]) * _dot(yb_ref[...], wpb_ref[...])
    x1 = x + _dot(m.astype(BF16), wo_ref[...])
    hn = _rms(x1, g2_ref[...]).astype(BF16)
    acc = x1
    for c in range(D_FF // FF_CHUNK):
        cols = slice(c * FF_CHUNK, (c + 1) * FF_CHUNK)
        u = jnp.maximum(_dot(hn, w1_ref[:, cols]), 0.0)
        acc = acc + _dot((u * u).astype(BF16), w2_ref[cols, :])
    o_ref[...] = acc


def _merge_mlp(x2d, ya, yb, tm, g1, wg, bg, wpa, wpb, wo, g2, w1, w2):
    n = x2d.shape[0]
    row = lambda w: pl.BlockSpec((tm, w), lambda i: (i, 0))
    consts = (g1, wg, bg, wpa, wpb, wo, g2, w1, w2)
    return pl.pallas_call(
        _merge_mlp_kernel,
        grid=(n // tm,),
        in_specs=[row(D_MODEL), row(WIDTH), row(WIDTH)] + [_const_spec(c.shape) for c in consts],
        out_specs=row(D_MODEL),
        out_shape=jax.ShapeDtypeStruct((n, D_MODEL), F32),
        compiler_params=_params(1),
        name="merge_mlp",
    )(x2d, ya, yb, *consts)


def _rope_tables(pos):
    half = HEAD_DIM // 2
    inv_freq = ROPE_THETA ** (-jnp.arange(half, dtype=F32) / half)
    ang = pos.astype(F32)[:, None] * inv_freq[None, :]
    cos = jnp.cos(ang)
    sin = jnp.sin(ang)
    reps = LANES // HEAD_DIM
    return (jnp.tile(jnp.concatenate([cos, cos], axis=1), (1, reps)),
            jnp.tile(jnp.concatenate([-sin, sin], axis=1), (1, reps)), cos.T, sin.T)


def _band_offset_table(table):
    n_far = BAND_PAST + GROUP_Q - 1 - REL_PAST + 1
    n_fut = BIAS_SPAN - n_far - (N_REL - 1)
    h = table.shape[0]
    f = jnp.concatenate([jnp.broadcast_to(table[:, N_REL - 1:], (h, n_far)), table[:, N_REL - 2::-1],
                         jnp.broadcast_to(table[:, :1], (h, n_fut))], axis=1).astype(F32)
    return f.reshape(h // 2, 2, BIAS_SPAN)


def _pool_matrix():
    i = np.arange(MXU_DIM)
    return jnp.asarray((i[:, None] // HEAD_DIM == i[None, :] // HEAD_DIM) / HEAD_DIM, dtype=BF16)


def kernel(x_prompt, x_sample, cache_a_k, cache_a_v, cache_b_k, cache_b_v, ln1_g, w_in, qn_a, kn_a, rel_bias, qn_b, kn_b, lam_q1, lam_k1, lam_q2, lam_k2, subln_g, w_gate, b_gate, w_proj_a, w_proj_b, w_out, ln2_g, w_ff1, w_ff2):
    depth = w_in.shape[0]
    b, s, d = x_prompt.shape
    db, t, _ = x_sample.shape
    past_len = cache_b_k.shape[2]
    lc = cache_a_k.shape[2]
    keep = min(BAND_PAST, s)
    assert d == D_MODEL and s % (2 * ROW_TILE) == 0 and keep <= 2 * ROW_TILE and keep % LANES == 0
    assert lc == BAND_PAST and t <= GROUP_Q

    cos_p, sin_p, cos_pt, sin_pt = _rope_tables(jnp.arange(s))
    cos_s, sin_s = (jnp.tile(a, (db, 1)) for a in _rope_tables(past_len + jnp.arange(t))[:2])
    lane_const = lambda g: jnp.broadcast_to(g[:, None], (HEAD_DIM, LANES))
    pool = _pool_matrix()
    tile_heads = lambda g: jnp.tile(g, WIDTH // HEAD_DIM)

    xp = x_prompt.reshape(b * s, d)
    xs = x_sample.reshape(db * t, d)
    outs = [[] for _ in range(8)]
    for l in range(depth):
        lam_init = 0.8 - 0.6 * math.exp(-0.3 * l)
        g1 = ln1_g[l][None]
        g2 = ln2_g[l][None]
        gains = jnp.stack([tile_heads(qn_a[l]), tile_heads(kn_a[l]), tile_heads(qn_b[l]), tile_heads(kn_b[l])])
        lam_vecs = jnp.stack([lam_q1[l], lam_k1[l], lam_q2[l], lam_k2[l]])
        subln = subln_g[l][None]
        bias_f = _band_offset_table(rel_bias[l])
        w_in_l = w_in[l].astype(BF16)

        kgain_t = jnp.stack([lane_const(kn_a[l]), lane_const(kn_b[l])])
        later_w = (w_gate[l], w_proj_a[l], w_proj_b[l], w_out[l], w_ff1[l], w_ff2[l])
        proj_out = _proj_prompt(xp, b, s, keep, g1, w_in_l, gains, cos_p, sin_p, pool, kgain_t, cos_pt, sin_pt, later_w)
        qa, kat, va, qb, kbt, vb4, kat_tail, vat_tail = proj_out[:8]
        wg, wpa, wpb, wo, w1, w2 = proj_out[8:]
        merge_w = (g1, wg, b_gate[l][None], wpa, wpb, wo, g2, w1, w2)
        as3 = lambda a: a.reshape(b, s, WIDTH)
        new = _proj_sample(xs, g1, w_in_l, gains, cos_s, sin_s, pool)
        caches = (cache_a_k[l].transpose(0, 2, 3, 1).reshape(db, WIDTH, lc),
                  cache_a_v[l].transpose(0, 2, 3, 1).reshape(db, WIDTH, lc),
                  cache_b_k[l].transpose(0, 2, 3, 4, 1).reshape(db, WIDTH, past_len),
                  cache_b_v[l].reshape(db, past_len * N_BLOCKS, LANES))
        ya, ya_s, yb_s = _band_prompt(as3(qa), kat, as3(va), bias_f, new, caches, lam_vecs, subln, lam_init, t)
        yb = _diff_prompt(as3(qb), kbt, vb4.reshape(b, s * N_BLOCKS, LANES), lam_vecs, subln, lam_init)
        xp = _merge_mlp(xp, ya.reshape(b * s, WIDTH), yb.reshape(b * s, WIDTH), 2 * ROW_TILE, *merge_w)
        token_major_a = lambda a: a.reshape(b, N_HEADS_A, HEAD_DIM, keep).transpose(0, 3, 1, 2)
        outs[0].append(token_major_a(kat_tail))
        outs[1].append(token_major_a(vat_tail))
        outs[2].append(kbt.reshape(b, N_HEADS_B, 2, HEAD_DIM, s).transpose(0, 4, 1, 2, 3))
        outs[3].append(vb4.reshape(b, s, N_HEADS_B, 2 * HEAD_DIM))

        xs = _merge_mlp(xs, ya_s, yb_s, db * t, *merge_w)
        outs[4].append(new[1].reshape(db, t, N_HEADS_A, HEAD_DIM))
        outs[5].append(new[2].reshape(db, t, N_HEADS_A, HEAD_DIM))
        outs[6].append(new[4].reshape(db, t, N_HEADS_B, 2, HEAD_DIM))
        outs[7].append(new[5].reshape(db, t, N_HEADS_B, 2 * HEAD_DIM))

    return (xp.reshape(b, s, d), xs.reshape(db, t, d)) + tuple(jnp.stack(o) for o in outs)
```
